```python
import math
import jax, jax.numpy as jnp
from jax import lax
import numpy as np

D_MODEL = 1024
BATCH = 32
SEQ = 256
DEPTH = 2
DEC_BATCH = 8
DEC_SEQ = 4096
PAST_LEN = 512

GRID_W = 64
BRANCH_W = 512
N_BRANCH = 3
H_RET = 4
RET_DK = 64
RET_DV = 128
RET_CHUNK = 128
H_DIFF = 4
DIFF_HD = 64
H_MLA = 8
MLA_NOPE = 64
MLA_ROPE = 32
MLA_V = 64
Q_LORA = 384
KV_LORA = 256
MLA_SCALE = (MLA_NOPE + MLA_ROPE) ** -0.5
D_FF = -(-8 * D_MODEL // (3 * 256)) * 256
Q_BLOCK = 128
ROPE_BASE = 10000.0
EPS = 1e-6
IN_SIZES = (H_RET * RET_DK, H_RET * RET_DK, H_RET * RET_DV, H_RET * RET_DV,
            H_DIFF * 2 * DIFF_HD, H_DIFF * 2 * DIFF_HD, H_DIFF * 2 * DIFF_HD,
            Q_LORA, KV_LORA, MLA_ROPE, N_BRANCH * D_MODEL)
N_IN = sum(IN_SIZES)

kernel_name = 'hybrid_ret_diff_mla_dit_step'


def _rmsnorm(x, g):
    xf = x.astype(jnp.float32)
    y = xf * lax.rsqrt(jnp.mean(xf * xf, axis=-1, keepdims=True) + EPS)
    return (y * g.astype(jnp.float32)).astype(x.dtype)


def _head_rmsnorm(o):
    of = o.astype(jnp.float32)
    return of * lax.rsqrt(jnp.mean(of * of, axis=-1, keepdims=True) + EPS)


def _group_layernorm(o):
    mu = jnp.mean(o, axis=-1, keepdims=True)
    var = jnp.mean(jnp.square(o - mu), axis=-1, keepdims=True)
    return (o - mu) * lax.rsqrt(var + EPS)


def _adaln(cond, w, b):
    return jax.nn.silu(cond) @ w + b


def _axial_rope_tables(n_tokens, rot_dim):
    t = jnp.arange(n_tokens)
    row = (t // GRID_W).astype(jnp.float32)
    col = (t % GRID_W).astype(jnp.float32)
    nf = rot_dim // 4
    inv = ROPE_BASE ** (-jnp.arange(nf, dtype=jnp.float32) / nf)
    ang = jnp.concatenate([row[:, None] * inv, col[:, None] * inv], axis=-1)
    return jnp.cos(ang), jnp.sin(ang)


def _rope(x, cos, sin):
    shape = (x.shape[1],) + (1,) * (x.ndim - 3) + (cos.shape[-1],)
    c = cos.reshape(shape).astype(x.dtype)
    s = sin.reshape(shape).astype(x.dtype)
    x1, x2 = jnp.split(x, 2, axis=-1)
    return jnp.concatenate([x1 * c - x2 * s, x2 * c + x1 * s], axis=-1)


def _retention(q, k, v, decay_logit, s0, strict):
    B, S, H, _ = q.shape
    dv = v.shape[-1]
    C = RET_CHUNK
    n = S // C
    lg = jax.nn.log_sigmoid(decay_logit.astype(jnp.float32))
    i = jnp.arange(C, dtype=jnp.float32)
    dist = i[:, None] - i[None, :]
    mask = (dist > 0) if strict else (dist >= 0)
    dmat = jnp.where(mask[None], jnp.exp(jnp.maximum(dist, 0.0)[None] * lg[:, None, None]), 0.0)
    xi = jnp.exp((i + 1.0)[:, None] * lg[None, :])[None, :, :, None]
    zeta = jnp.exp((C - 1.0 - i)[:, None] * lg[None, :])[None, :, :, None]
    g_chunk = jnp.exp(C * lg)[None, :, None, None]

    def chunks(a):
        return jnp.moveaxis(a.astype(jnp.float32).reshape((B, n, C) + a.shape[2:]), 1, 0)

    def step(s, qkv):
        qc, kc, vc = qkv
        sc = jnp.einsum('bihd,bjhd->bhij', qc, kc) * dmat
        inner = jnp.einsum('bhij,bjhe->bihe', sc, vc)
        cross = jnp.einsum('bihd,bhde->bihe', qc, s) * xi
        s_new = g_chunk * s + jnp.einsum('bjhd,bjhe->bhde', kc * zeta, vc)
        return s_new, inner + cross

    s_fin, out = lax.scan(step, s0.astype(jnp.float32), (chunks(q), chunks(k), chunks(v)))
    return jnp.moveaxis(out, 0, 1).reshape(B, S, H, dv), s_fin


def _query_blocks(fn, *qs):
    B, S = qs[0].shape[:2]
    nb = S // Q_BLOCK
    blocks = tuple(jnp.moveaxis(q.reshape((B, nb, Q_BLOCK) + q.shape[2:]), 1, 0) for q in qs)
    out = lax.map(lambda a: fn(*a), blocks)
    return jnp.moveaxis(out, 0, 1).reshape((B, S) + out.shape[3:])


def _diff_attention(q, k, v, lam, lam_init):
    s = jnp.einsum('bqhcd,bkhcd->bhcqk', q, k).astype(jnp.float32) * DIFF_HD ** -0.5
    p = jax.nn.softmax(s, axis=-1)
    a = p[:, :, 0] - lam * p[:, :, 1]
    o = jnp.einsum('bhqk,bkhe->bqhe', a.astype(v.dtype), v)
    return (_head_rmsnorm(o) * (1.0 - lam_init)).astype(v.dtype)


def _mla_attention(q_nope, q_pe, k_nope, k_pe, v):
    s = jnp.einsum('bqhd,bkhd->bhqk', q_nope, k_nope) + jnp.einsum('bqhr,bkr->bhqk', q_pe, k_pe)
    p = jax.nn.softmax(s.astype(jnp.float32) * MLA_SCALE, axis=-1)
    return jnp.einsum('bhqk,bkhe->bqhe', p.astype(v.dtype), v)


def _swiglu(h, w_in, w_out):
    a, b = jnp.split(h @ w_in, 2, axis=-1)
    return (jax.nn.silu(a) * b) @ w_out


def _mixers(h, lw, lam_init, rope, ctx):
    B, S, _ = h.shape
    idx = np.cumsum(IN_SIZES)[:-1].tolist()
    rq, rk, rv, rg, dq, dk, dv, cq, ckv, kpe, gate_logits = jnp.split(h @ lw['w_in'], idx, axis=-1)

    rq = rq.reshape(B, S, H_RET, RET_DK)
    rk = rk.reshape(B, S, H_RET, RET_DK) * RET_DK ** -0.5
    rv = rv.reshape(B, S, H_RET, RET_DV)
    dq = dq.reshape(B, S, H_DIFF, 2, DIFF_HD)
    dk = dk.reshape(B, S, H_DIFF, 2, DIFF_HD)
    dv = dv.reshape(B, S, H_DIFF, 2 * DIFF_HD)
    q_mla = (_rmsnorm(cq, lw['mla_q_norm']) @ lw['w_uq']).reshape(B, S, H_MLA, MLA_NOPE + MLA_ROPE)
    q_nope, q_pe = jnp.split(q_mla, [MLA_NOPE], axis=-1)
    ckv = _rmsnorm(ckv, lw['mla_kv_norm'])

    if rope is not None:
        (c64, s64), (c32, s32) = rope
        rq, rk = _rope(rq, c64, s64), _rope(rk, c64, s64)
        dq, dk = _rope(dq, c64, s64), _rope(dk, c64, s64)
        q_pe, kpe = _rope(q_pe, c32, s32), _rope(kpe, c32, s32)

    if ctx is None:
        s_f0 = jnp.zeros((B, H_RET, RET_DK, RET_DV), jnp.float32)
        s_b0 = s_f0
        dk_all, dv_all, ckv_all, kpe_all = dk, dv, ckv, kpe
    else:
        s_f0, s_b0, ck, cv, cckv, ckpe = ctx
        L = ck.shape[1]
        dk_all = jnp.concatenate([ck.reshape(B, L, H_DIFF, 2, DIFF_HD), dk], axis=1)
        dv_all = jnp.concatenate([cv, dv], axis=1)
        ckv_all = jnp.concatenate([cckv, ckv], axis=1)
        kpe_all = jnp.concatenate([ckpe, kpe], axis=1)

    o_f, s_f = _retention(rq, rk, rv, lw['ret_decay_fwd'], s_f0, strict=False)
    o_b, s_b = _retention(jnp.flip(rq, 1), jnp.flip(rk, 1), jnp.flip(rv, 1), lw['ret_decay_bwd'], s_b0, strict=True)
    o_ret = _group_layernorm(o_f + jnp.flip(o_b, 1)).reshape(B, S, BRANCH_W).astype(h.dtype)
    y_ret = jax.nn.silu(rg) * o_ret

    lq1, lk1, lq2, lk2 = lw['diff_lambda'][0], lw['diff_lambda'][1], lw['diff_lambda'][2], lw['diff_lambda'][3]
    lam = jnp.exp(jnp.sum(lq1 * lk1)) - jnp.exp(jnp.sum(lq2 * lk2)) + lam_init
    y_diff = _query_blocks(lambda q: _diff_attention(q, dk_all, dv_all, lam, lam_init), dq).reshape(B, S, BRANCH_W)

    kv = (ckv_all @ lw['w_ukv']).reshape(B, ckv_all.shape[1], H_MLA, MLA_NOPE + MLA_V)
    k_nope, v_mla = jnp.split(kv, [MLA_NOPE], axis=-1)
    y_mla = _query_blocks(lambda qn, qp: _mla_attention(qn, qp, k_nope, kpe_all, v_mla), q_nope, q_pe).reshape(B, S, BRANCH_W)

    branches = jnp.einsum('bsgc,gcd->bsgd', jnp.stack([y_ret, y_diff, y_mla], axis=2), lw['w_branch'])
    gates = jax.nn.sigmoid(gate_logits.reshape(B, S, N_BRANCH, D_MODEL))
    out = jnp.sum(gates * branches, axis=2) @ lw['w_out']
    ctx_out = None if ctx is not None else (s_f, s_b, dk.reshape(B, S, H_DIFF, 2 * DIFF_HD), dv, ckv, kpe)
    return out, ctx_out


def _layer(x, mod, lw, lam_init, rope, ctx):
    sh1, sc1, g1, sh2, sc2, g2 = jnp.split(mod, 6, axis=-1)
    h = _rmsnorm(x, lw['norm1_g']) * (1.0 + sc1) + sh1
    mix, ctx_out = _mixers(h, lw, lam_init, rope, ctx)
    x = x + g1 * mix
    h = _rmsnorm(x, lw['norm2_g']) * (1.0 + sc2) + sh2
    x = x + g2 * _swiglu(h, lw['w_ffn_in'], lw['w_ffn_out'])
    return x, ctx_out


def setup_inputs(seed: int = 0) -> dict:
    key = jax.random.key(seed)
    ks = jax.random.split(key, 32)
    f32 = jnp.float32

    def nrm(k, shape, scale):
        return jax.random.normal(k, shape, f32) * scale

    ret_init = jnp.log(jnp.exp2(5.0 + jnp.arange(H_RET, dtype=f32)) - 1.0)
    return {
        'x_prompt': nrm(ks[0], (BATCH, SEQ, D_MODEL), 1.0),
        'x_sample': nrm(ks[1], (DEC_BATCH, DEC_SEQ, D_MODEL), 1.0),
        'state_ret_fwd': nrm(ks[2], (DEC_BATCH, DEPTH, H_RET, RET_DK, RET_DV), 0.5),
        'state_ret_bwd': nrm(ks[3], (DEC_BATCH, DEPTH, H_RET, RET_DK, RET_DV), 0.5),
        'cache_diff_k': nrm(ks[4], (DEC_BATCH, DEPTH, PAST_LEN, H_DIFF, 2 * DIFF_HD), 1.0),
        'cache_diff_v': nrm(ks[5], (DEC_BATCH, DEPTH, PAST_LEN, H_DIFF, 2 * DIFF_HD), 1.0),
        'cache_mla_ckv': nrm(ks[6], (DEC_BATCH, DEPTH, PAST_LEN, KV_LORA), 1.0),
        'cache_mla_kpe': nrm(ks[7], (DEC_BATCH, DEPTH, PAST_LEN, MLA_ROPE), 1.0),
        'c': nrm(ks[8], (DEC_BATCH, D_MODEL), 1.0),
        'c_ctx': nrm(ks[9], (D_MODEL,), 1.0),
        'norm1_g': 1.0 + nrm(ks[10], (DEPTH, D_MODEL), 0.1),
        'norm2_g': 1.0 + nrm(ks[11], (DEPTH, D_MODEL), 0.1),
        'w_ada': nrm(ks[12], (DEPTH, D_MODEL, 6 * D_MODEL), 0.5 * D_MODEL ** -0.5),
        'b_ada': nrm(ks[13], (DEPTH, 6 * D_MODEL), 0.02),
        'w_in': nrm(ks[14], (DEPTH, D_MODEL, N_IN), D_MODEL ** -0.5),
        'ret_decay_fwd': ret_init[None, :] + nrm(ks[15], (DEPTH, H_RET), 0.1),
        'ret_decay_bwd': ret_init[None, :] + nrm(ks[16], (DEPTH, H_RET), 0.1),
        'diff_lambda': nrm(ks[17], (DEPTH, 4, DIFF_HD), 0.1),
        'mla_q_norm': 1.0 + nrm(ks[18], (DEPTH, Q_LORA), 0.1),
        'mla_kv_norm': 1.0 + nrm(ks[19], (DEPTH, KV_LORA), 0.1),
        'w_uq': nrm(ks[20], (DEPTH, Q_LORA, H_MLA * (MLA_NOPE + MLA_ROPE)), Q_LORA ** -0.5),
        'w_ukv': nrm(ks[21], (DEPTH, KV_LORA, H_MLA * (MLA_NOPE + MLA_V)), KV_LORA ** -0.5),
        'w_branch': nrm(ks[22], (DEPTH, N_BRANCH, BRANCH_W, D_MODEL), BRANCH_W ** -0.5),
        'w_out': nrm(ks[23], (DEPTH, D_MODEL, D_MODEL), D_MODEL ** -0.5),
        'w_ffn_in': nrm(ks[24], (DEPTH, D_MODEL, 2 * D_FF), D_MODEL ** -0.5),
        'w_ffn_out': nrm(ks[25], (DEPTH, D_FF, D_MODEL), D_FF ** -0.5),
        'final_g': 1.0 + nrm(ks[26], (D_MODEL,), 0.1),
    }


def reference(x_prompt, x_sample, state_ret_fwd, state_ret_bwd, cache_diff_k, cache_diff_v,
              cache_mla_ckv, cache_mla_kpe, c, c_ctx, norm1_g, norm2_g, w_ada, b_ada, w_in,
              ret_decay_fwd, ret_decay_bwd, diff_lambda, mla_q_norm, mla_kv_norm, w_uq, w_ukv,
              w_branch, w_out, w_ffn_in, w_ffn_out, final_g):
    rows = x_sample.shape[1] // GRID_W
    n_lat = rows * GRID_W
    rope = (_axial_rope_tables(n_lat, RET_DK), _axial_rope_tables(n_lat, MLA_ROPE))

    xp, xs = x_prompt, x_sample
    ret_f, ret_b, dks, dvs, ckvs, kpes = [], [], [], [], [], []
    for l in range(DEPTH):
        lw = {'norm1_g': norm1_g[l], 'norm2_g': norm2_g[l], 'w_in': w_in[l],
              'ret_decay_fwd': ret_decay_fwd[l], 'ret_decay_bwd': ret_decay_bwd[l],
              'diff_lambda': diff_lambda[l], 'mla_q_norm': mla_q_norm[l], 'mla_kv_norm': mla_kv_norm[l],
              'w_uq': w_uq[l], 'w_ukv': w_ukv[l], 'w_branch': w_branch[l], 'w_out': w_out[l],
              'w_ffn_in': w_ffn_in[l], 'w_ffn_out': w_ffn_out[l]}
        lam_init = 0.8 - 0.6 * math.exp(-0.3 * l)

        mod_ctx = _adaln(c_ctx[None, :], w_ada[l], b_ada[l])[:, None, :]
        xp, ctx_t = _layer(xp, mod_ctx, lw, lam_init, None, None)
        ret_f.append(ctx_t[0]); ret_b.append(ctx_t[1]); dks.append(ctx_t[2])
        dvs.append(ctx_t[3]); ckvs.append(ctx_t[4]); kpes.append(ctx_t[5])

        mod_lat = _adaln(c, w_ada[l], b_ada[l])[:, None, :]
        cache_l = (state_ret_fwd[:, l], state_ret_bwd[:, l], cache_diff_k[:, l], cache_diff_v[:, l],
                   cache_mla_ckv[:, l], cache_mla_kpe[:, l])
        xs, _ = _layer(xs, mod_lat, lw, lam_init, rope, cache_l)

    y_prompt = _rmsnorm(xp, final_g)
    y_sample = _rmsnorm(xs, final_g)
    new_ret_fwd = jnp.stack(ret_f, axis=1)
    new_ret_bwd = jnp.stack(ret_b, axis=1)
    new_diff_k = jnp.stack(dks, axis=1)
    new_diff_v = jnp.stack(dvs, axis=1)
    new_mla_ckv = jnp.stack(ckvs, axis=1)
    new_mla_kpe = jnp.stack(kpes, axis=1)
    return (y_prompt, y_sample, new_ret_fwd, new_ret_bwd, new_diff_k, new_diff_v, new_mla_ckv, new_mla_kpe)
```

```python
import functools
import math

import jax
import jax.numpy as jnp
from jax import lax
from jax.experimental import pallas as pl
from jax.experimental.pallas import tpu as pltpu

F32 = jnp.float32
BF16 = jnp.bfloat16

D_MODEL = 1024
GRID_W = 64
H_RET, RET_DK, RET_DV, RET_CHUNK = 4, 64, 128, 128
H_DIFF, DIFF_HD = 4, 64
H_MLA, MLA_NOPE, MLA_ROPE, MLA_V = 8, 64, 32, 64
Q_LORA, KV_LORA = 384, 256
D_FF = 2816
ROPE_BASE = 10000.0
EPS = 1e-6
MLA_SCALE = (MLA_NOPE + MLA_ROPE) ** -0.5
LANES = 128
NEG_BIG = -1e30

TOKEN_TILE = 512
FFN_CHUNK = 1408
DIFF_Q_TILE = 256
MLA_Q_TILE = 512
KEY_TILE = 512
VMEM_LIMIT = 56 * 1024 * 1024


def _cparams(*sem):
    return pltpu.CompilerParams(dimension_semantics=sem, vmem_limit_bytes=VMEM_LIMIT)


def _dot(a, b):
    return jnp.dot(a, b, preferred_element_type=F32)


def _dot_nt(a, b):
    return lax.dot_general(a, b, (((1,), (1,)), ((), ())), preferred_element_type=F32)


def _rms(x):
    return x * lax.rsqrt(jnp.mean(x * x, axis=-1, keepdims=True) + EPS)


def _silu(x):
    return x * (1.0 / (1.0 + jnp.exp(-x)))


def _sigmoid(x):
    return 1.0 / (1.0 + jnp.exp(-x))


def _rope(x, c_ref, sa_ref, sb_ref, shift_a, shift_b):
    return (x * c_ref[...] + pltpu.roll(x, shift_a, 1) * sa_ref[...]
            + pltpu.roll(x, shift_b, 1) * sb_ref[...])


def _mod_body(c_ref, w_ref, b_ref, o_ref):
    o_ref[0] = _dot(_silu(c_ref[...]).astype(BF16), w_ref[0]) + b_ref[0]


def _modulation(cond, w_ada, b_ada):
    depth, d, n = w_ada.shape
    rows = cond.shape[0]
    tn = 1536
    return pl.pallas_call(
        _mod_body,
        out_shape=jax.ShapeDtypeStruct((depth, rows, n), F32),
        grid=(depth, n // tn),
        in_specs=[pl.BlockSpec((rows, d), lambda l, j: (0, 0)),
                  pl.BlockSpec((1, d, tn), lambda l, j: (l, 0, j)),
                  pl.BlockSpec((1, 1, tn), lambda l, j: (l, 0, j))],
        out_specs=pl.BlockSpec((1, rows, tn), lambda l, j: (l, 0, j)),
        compiler_params=_cparams("parallel", "parallel"),
        name="adaln_mod",
    )(cond, w_ada, b_ada.reshape(depth, 1, n))


def _norm_matmul_body(*refs, has_mod, shift_row, scale_row):
    if has_mod:
        x_ref, g_ref, mod_ref, w_ref, o_ref = refs
    else:
        x_ref, g_ref, w_ref, o_ref = refs
    h = _rms(x_ref[...]) * g_ref[...]
    if has_mod:
        m = mod_ref[0]
        h = h * (1.0 + m[scale_row:scale_row + 1, :]) + m[shift_row:shift_row + 1, :]
    o_ref[...] = _dot(h.astype(BF16), w_ref[...]).astype(o_ref.dtype)


def _norm_matmul(x, x_col, k, g, w, *, mod=None, mod_row=None, shift_row=0, scale_row=1,
                 tn, out_dtype=F32, name):
    t = x.shape[0]
    n = w.shape[1]
    tm = min(TOKEN_TILE, t)
    assert t % tm == 0 and n % tn == 0
    in_specs = [pl.BlockSpec((tm, k), lambda j, i: (i, x_col)),
                pl.BlockSpec((1, k), lambda j, i: (0, 0))]
    args = [x, g.reshape(1, k)]
    if mod is not None:
        in_specs.append(pl.BlockSpec((1, 8, k), lambda j, i: (mod_row(i), 0, 0)))
        args.append(mod)
    in_specs.append(pl.BlockSpec((k, tn), lambda j, i: (0, j)))
    args.append(w)
    return pl.pallas_call(
        functools.partial(_norm_matmul_body, has_mod=mod is not None,
                          shift_row=shift_row, scale_row=scale_row),
        out_shape=jax.ShapeDtypeStruct((t, n), out_dtype),
        grid=(n // tn, t // tm),
        in_specs=in_specs,
        out_specs=pl.BlockSpec((tm, tn), lambda j, i: (i, j)),
        compiler_params=_cparams("parallel", "parallel"),
        name=name,
    )(*args)


def _ret_scan_body(*refs, use_rope, n_chunks):
    if use_rope:
        (gc_ref, kf_ref, vf_ref, kb_ref, vb_ref, zf_ref, zb_ref, s0f_ref, s0b_ref,
         cf_ref, saf_ref, sbf_ref, cb_ref, sab_ref, sbb_ref,
         sf_ref, sb_ref, finf_ref, finb_ref, st_ref) = refs
    else:
        (gc_ref, kf_ref, vf_ref, kb_ref, vb_ref, zf_ref, zb_ref, s0f_ref, s0b_ref,
         sf_ref, sb_ref, finf_ref, finb_ref, st_ref) = refs
    p = pl.program_id(1)
    t = pl.program_id(2)

    @pl.when(t == 0)
    def _():
        st_ref[0] = s0f_ref[0]
        st_ref[1] = s0b_ref[0]

    sf_ref[0, :, 0] = st_ref[0]
    sb_ref[0, :, 0] = st_ref[1]

    def update(d, k_ref, v_ref, z_ref, rope_refs):
        k = k_ref[...]
        if use_rope:
            k = _rope(k, *rope_refs, 96, 32)
        kz_t = (k * (RET_DK ** -0.5) * z_ref[0]).T.astype(BF16)
        v = v_ref[...].astype(BF16)
        for j in range(2):
            loc = _dot(kz_t[RET_DK * j:RET_DK * (j + 1)], v[:, RET_DV * j:RET_DV * (j + 1)])
            st_ref[d, j] = gc_ref[2 * p + j, d] * st_ref[d, j] + loc

    update(0, kf_ref, vf_ref, zf_ref, (cf_ref, saf_ref, sbf_ref) if use_rope else None)
    update(1, kb_ref, vb_ref, zb_ref, (cb_ref, sab_ref, sbb_ref) if use_rope else None)

    @pl.when(t == n_chunks - 1)
    def _():
        finf_ref[0] = st_ref[0]
        finb_ref[0] = st_ref[1]


def _ret_out_body(*refs, use_rope):
    if use_rope:
        (q_ref, k_ref, v_ref, g_ref, sf_ref, sb_ref, dm_ref, xif_ref, xib_ref,
         c_ref, sa_ref, sbt_ref, y_ref) = refs
    else:
        (q_ref, k_ref, v_ref, g_ref, sf_ref, sb_ref, dm_ref, xif_ref, xib_ref, y_ref) = refs
    q = q_ref[...]
    k = k_ref[...]
    if use_rope:
        q = _rope(q, c_ref, sa_ref, sbt_ref, 96, 32)
        k = _rope(k, c_ref, sa_ref, sbt_ref, 96, 32)
    kb = (k * (RET_DK ** -0.5)).astype(BF16)
    v = v_ref[...].astype(BF16)
    g = g_ref[...]
    s_f = sf_ref[0, :, 0].reshape(2 * RET_DK, RET_DV).astype(BF16)
    s_b = sb_ref[0, :, 0].reshape(2 * RET_DK, RET_DV).astype(BF16)
    lane = lax.broadcasted_iota(jnp.int32, q.shape, 1)
    for j in range(2):
        in_head = (lane >= RET_DK * j) & (lane < RET_DK * (j + 1))
        qm = jnp.where(in_head, q, 0.0).astype(BF16)
        sc = _dot_nt(qm, kb) * dm_ref[j]
        o = _dot(sc.astype(BF16), v[:, RET_DV * j:RET_DV * (j + 1)])
        o = o + _dot(qm, s_f) * xif_ref[j] + _dot(qm, s_b) * xib_ref[j]
        mu = jnp.mean(o, axis=-1, keepdims=True)
        oc = o - mu
        on = oc * lax.rsqrt(jnp.mean(oc * oc, axis=-1, keepdims=True) + EPS)
        gj = g[:, RET_DV * j:RET_DV * (j + 1)]
        y_ref[:, RET_DV * j:RET_DV * (j + 1)] = (_silu(gj) * on).astype(y_ref.dtype)


def _retention(seg, b, s, s0f, s0b, tabs, rope):
    c = RET_CHUNK
    n = s // c
    use_rope = rope is not None
    gc, zf, zb, dmat, xif, xib = tabs
    state_shape = jax.ShapeDtypeStruct((b, H_RET, n, RET_DK, RET_DV), F32)
    fin_shape = jax.ShapeDtypeStruct((b, H_RET, RET_DK, RET_DV), F32)

    def fwd(bi, t):
        return bi * n + t

    def bwd(bi, t):
        return bi * n + (n - 1 - t)

    in_specs = [
        pl.BlockSpec(memory_space=pltpu.SMEM),
        pl.BlockSpec((c, 128), lambda bi, p, t: (fwd(bi, t), 2 + p)),
        pl.BlockSpec((c, 256), lambda bi, p, t: (fwd(bi, t), 2 + p)),
        pl.BlockSpec((c, 128), lambda bi, p, t: (bwd(bi, t), 2 + p)),
        pl.BlockSpec((c, 256), lambda bi, p, t: (bwd(bi, t), 2 + p)),
        pl.BlockSpec((1, c, 128), lambda bi, p, t: (p, 0, 0)),
        pl.BlockSpec((1, c, 128), lambda bi, p, t: (p, 0, 0)),
        pl.BlockSpec((1, 2, RET_DK, RET_DV), lambda bi, p, t: (bi, p, 0, 0)),
        pl.BlockSpec((1, 2, RET_DK, RET_DV), lambda bi, p, t: (bi, p, 0, 0)),
    ]
    args = [gc, seg, seg, seg, seg, zf, zb, s0f, s0b]
    if use_rope:
        in_specs += [pl.BlockSpec((c, 128), lambda bi, p, t: (t, 0))] * 3
        in_specs += [pl.BlockSpec((c, 128), lambda bi, p, t: (n - 1 - t, 0))] * 3
        args += list(rope) + list(rope)
    sf, sb, fin_f, fin_b = pl.pallas_call(
        functools.partial(_ret_scan_body, use_rope=use_rope, n_chunks=n),
        out_shape=(state_shape, state_shape, fin_shape, fin_shape),
        grid=(b, 2, n),
        in_specs=in_specs,
        out_specs=(pl.BlockSpec((1, 2, 1, RET_DK, RET_DV), lambda bi, p, t: (bi, p, t, 0, 0)),
                   pl.BlockSpec((1, 2, 1, RET_DK, RET_DV), lambda bi, p, t: (bi, p, n - 1 - t, 0, 0)),
                   pl.BlockSpec((1, 2, RET_DK, RET_DV), lambda bi, p, t: (bi, p, 0, 0)),
                   pl.BlockSpec((1, 2, RET_DK, RET_DV), lambda bi, p, t: (bi, p, 0, 0))),
        scratch_shapes=[pltpu.VMEM((2, 2, RET_DK, RET_DV), F32)],
        compiler_params=_cparams("parallel", "parallel", "arbitrary"),
        name="ret_scan",
    )(*args)

    in_specs = [
        pl.BlockSpec((c, 128), lambda bi, p, t: (fwd(bi, t), p)),
        pl.BlockSpec((c, 128), lambda bi, p, t: (fwd(bi, t), 2 + p)),
        pl.BlockSpec((c, 256), lambda bi, p, t: (fwd(bi, t), 2 + p)),
        pl.BlockSpec((c, 256), lambda bi, p, t: (fwd(bi, t), 4 + p)),
        pl.BlockSpec((1, 2, 1, RET_DK, RET_DV), lambda bi, p, t: (bi, p, t, 0, 0)),
        pl.BlockSpec((1, 2, 1, RET_DK, RET_DV), lambda bi, p, t: (bi, p, t, 0, 0)),
        pl.BlockSpec((2, c, c), lambda bi, p, t: (p, 0, 0)),
        pl.BlockSpec((2, c, 128), lambda bi, p, t: (p, 0, 0)),
        pl.BlockSpec((2, c, 128), lambda bi, p, t: (p, 0, 0)),
    ]
    args = [seg, seg, seg, seg, sf, sb, dmat, xif, xib]
    if use_rope:
        in_specs += [pl.BlockSpec((c, 128), lambda bi, p, t: (t, 0))] * 3
        args += list(rope)
    y = pl.pallas_call(
        functools.partial(_ret_out_body, use_rope=use_rope),
        out_shape=jax.ShapeDtypeStruct((b * s, H_RET * RET_DV), BF16),
        grid=(b, 2, n),
        in_specs=in_specs,
        out_specs=pl.BlockSpec((c, 256), lambda bi, p, t: (fwd(bi, t), p)),
        compiler_params=_cparams("parallel", "parallel", "parallel"),
        name="ret_out",
    )(*args)
    return y, fin_f, fin_b


def _retention_tables(decay_f, decay_b):
    c = RET_CHUNK
    lg_f = jax.nn.log_sigmoid(decay_f.astype(F32))
    lg_b = jax.nn.log_sigmoid(decay_b.astype(F32))
    i = jnp.arange(c, dtype=F32)
    dist = i[:, None] - i[None, :]
    d_f = jnp.where(dist >= 0, jnp.exp(jnp.maximum(dist, 0.0)[None] * lg_f[:, None, None]), 0.0)
    d_b = jnp.where(dist < 0, jnp.exp(jnp.maximum(-dist, 0.0)[None] * lg_b[:, None, None]), 0.0)
    dmat = d_f + d_b
    xi_f = jnp.exp((i + 1.0)[None, :] * lg_f[:, None])
    xi_b = jnp.exp((c - i)[None, :] * lg_b[:, None])
    zeta_f = jnp.exp((c - 1.0 - i)[None, :] * lg_f[:, None])
    zeta_b = jnp.exp(i[None, :] * lg_b[:, None])
    gc = jnp.stack([jnp.exp(c * lg_f), jnp.exp(c * lg_b)], axis=1)

    def lanes(tab, width):
        return jnp.broadcast_to(tab[:, :, None], tab.shape + (width,))

    def pair(tab):
        t64 = lanes(tab, RET_DK).reshape(H_RET // 2, 2, c, RET_DK)
        return jnp.concatenate([t64[:, 0], t64[:, 1]], axis=-1)

    return gc, pair(zeta_f), pair(zeta_b), dmat, lanes(xi_f, RET_DV), lanes(xi_b, RET_DV)


def _diff_prep_body(*refs, n_cache, use_rope):
    refs = list(refs)
    if n_cache:
        ck_ref, cv_ref = refs[:2]
        refs = refs[2:]
    k_ref, v_ref = refs[:2]
    refs = refs[2:]
    if use_rope:
        rope_refs = refs[:3]
        refs = refs[3:]
    kt_ref, vt_ref = refs
    c = pl.program_id(2)

    def from_new():
        k = k_ref[...]
        if use_rope:
            k = _rope(k, *rope_refs, 96, 32)
        kt_ref[0, 0, 0] = (k * (DIFF_HD ** -0.5)).astype(BF16)
        vt_ref[0, 0, 0] = v_ref[...].T.astype(BF16)

    def from_cache():
        kt_ref[0, 0, 0] = (ck_ref[0, 0] * (DIFF_HD ** -0.5)).astype(BF16)
        vt_ref[0, 0, 0] = cv_ref[0, 0].T.astype(BF16)

    if n_cache:
        pl.when(c < n_cache)(from_cache)
        pl.when(c >= n_cache)(from_new)
    else:
        from_new()


def _diff_prep(seg, b, s, tk, layer, cache_k, cache_v, rope):
    n_new = s // tk
    n_cache = 0 if cache_k is None else cache_k.shape[2] // tk
    nk = n_cache + n_new
    use_rope = rope is not None

    def new_row(bi, c):
        return bi * n_new + jnp.maximum(c - n_cache, 0)

    in_specs, args = [], []
    if n_cache:
        spec = pl.BlockSpec((1, 1, tk, 128), lambda bi, h, c: (bi, layer, jnp.minimum(c, n_cache - 1), h))
        in_specs += [spec, spec]
        args += [cache_k, cache_v]
    in_specs += [pl.BlockSpec((tk, 128), lambda bi, h, c: (new_row(bi, c), H_DIFF + h)),
                 pl.BlockSpec((tk, 128), lambda bi, h, c: (new_row(bi, c), 2 * H_DIFF + h))]
    args += [seg, seg]
    if use_rope:
        in_specs += [pl.BlockSpec((tk, 128), lambda bi, h, c: (jnp.maximum(c - n_cache, 0), 0))] * 3
        args += list(rope)
    return pl.pallas_call(
        functools.partial(_diff_prep_body, n_cache=n_cache, use_rope=use_rope),
        out_shape=(jax.ShapeDtypeStruct((b, H_DIFF, nk, tk, 128), BF16),
                   jax.ShapeDtypeStruct((b, H_DIFF, nk, 128, tk), BF16)),
        grid=(b, H_DIFF, nk),
        in_specs=in_specs,
        out_specs=(pl.BlockSpec((1, 1, 1, tk, 128), lambda bi, h, c: (bi, h, c, 0, 0)),
                   pl.BlockSpec((1, 1, 1, 128, tk), lambda bi, h, c: (bi, h, c, 0, 0))),
        compiler_params=_cparams("parallel", "parallel", "arbitrary"),
        name="diff_prep",
    )(*args)


def _mla_prep_body(*refs, n_cache, use_rope, emit_ckv):
    refs = list(refs)
    if n_cache:
        cc_ref, cp_ref = refs[:2]
        refs = refs[2:]
    ckv_ref, kpe_ref, g_ref, wk_ref, wv_ref = refs[:5]
    refs = refs[5:]
    if use_rope:
        rope_refs = refs[:3]
        refs = refs[3:]
    kt_ref, vt_ref = refs[:2]
    c = pl.program_id(1)

    def emit(cn, kp):
        cb = cn.astype(BF16)
        kfull = _dot(cb, wk_ref[...])
        vt = _dot(cb, wv_ref[...]).T
        for h in range(H_MLA):
            kt_ref[0, h, 0] = (kfull[:, LANES * h:LANES * (h + 1)] + kp).astype(BF16)
            vt_ref[0, h, 0] = vt[MLA_V * h:MLA_V * (h + 1)].astype(BF16)

    def from_new():
        cn = _rms(ckv_ref[...]) * g_ref[...]
        kp = kpe_ref[...]
        if use_rope:
            kp = _rope(kp, *rope_refs, 112, 16)
        if emit_ckv:
            refs[2][...] = cn
        emit(cn, kp)

    def from_cache():
        emit(cc_ref[0, 0], cp_ref[0, 0])

    if n_cache:
        pl.when(c < n_cache)(from_cache)
        pl.when(c >= n_cache)(from_new)
    else:
        from_new()


def _mla_prep(seg, b, s, tk, layer, g, wk, wv, cache_ckv, cache_kpe, rope):
    n_new = s // tk
    n_cache = 0 if cache_ckv is None else cache_ckv.shape[2] // tk
    nk = n_cache + n_new
    use_rope = rope is not None
    emit_ckv = n_cache == 0

    def new_row(bi, c):
        return bi * n_new + jnp.maximum(c - n_cache, 0)

    in_specs, args = [], []
    if n_cache:
        in_specs += [pl.BlockSpec((1, 1, tk, KV_LORA), lambda bi, c: (bi, layer, jnp.minimum(c, n_cache - 1), 0)),
                     pl.BlockSpec((1, 1, tk, LANES), lambda bi, c: (bi, layer, jnp.minimum(c, n_cache - 1), 0))]
        args += [cache_ckv, cache_kpe]
    in_specs += [pl.BlockSpec((tk, KV_LORA), lambda bi, c: (new_row(bi, c), 0)),
                 pl.BlockSpec((tk, LANES), lambda bi, c: (new_row(bi, c), 2)),
                 pl.BlockSpec((1, KV_LORA), lambda bi, c: (0, 0)),
                 pl.BlockSpec(wk.shape, lambda bi, c: (0, 0)),
                 pl.BlockSpec(wv.shape, lambda bi, c: (0, 0))]
    args += [seg, seg, g.reshape(1, KV_LORA), wk, wv]
    if use_rope:
        in_specs += [pl.BlockSpec((tk, 128), lambda bi, c: (jnp.maximum(c - n_cache, 0), 0))] * 3
        args += list(rope)
    out_shape = [jax.ShapeDtypeStruct((b, H_MLA, nk, tk, LANES), BF16),
                 jax.ShapeDtypeStruct((b, H_MLA, nk, MLA_V, tk), BF16)]
    out_specs = [pl.BlockSpec((1, H_MLA, 1, tk, LANES), lambda bi, c: (bi, 0, c, 0, 0)),
                 pl.BlockSpec((1, H_MLA, 1, MLA_V, tk), lambda bi, c: (bi, 0, c, 0, 0))]
    if emit_ckv:
        out_shape.append(jax.ShapeDtypeStruct((b * s, KV_LORA), F32))
        out_specs.append(pl.BlockSpec((tk, KV_LORA), lambda bi, c: (new_row(bi, c), 0)))
    outs = pl.pallas_call(
        functools.partial(_mla_prep_body, n_cache=n_cache, use_rope=use_rope, emit_ckv=emit_ckv),
        out_shape=tuple(out_shape),
        grid=(b, nk),
        in_specs=in_specs,
        out_specs=tuple(out_specs),
        compiler_params=_cparams("parallel", "arbitrary"),
        name="mla_prep",
    )(*args)
    return outs if emit_ckv else (outs[0], outs[1], None)


def _flash_keymajor(q_t, k_tile, vt_tile, nk, m_ref, l_ref, acc_ref, scale):
    m_ref[...] = jnp.full(m_ref.shape, NEG_BIG, F32)
    l_ref[...] = jnp.zeros(l_ref.shape, F32)
    acc_ref[...] = jnp.zeros(acc_ref.shape, F32)

    def step(i, carry):
        s = _dot(k_tile(i), q_t)
        if scale is not None:
            s = s * scale
        m_prev = m_ref[...]
        m_new = jnp.maximum(m_prev, jnp.max(s, axis=0, keepdims=True))
        alpha = jnp.exp(m_prev - m_new)
        p = jnp.exp(s - m_new)
        l_ref[...] = alpha * l_ref[...] + jnp.sum(p, axis=0, keepdims=True)
        acc_ref[...] = alpha * acc_ref[...] + _dot(vt_tile(i), p.astype(BF16))
        m_ref[...] = m_new
        return carry

    lax.fori_loop(0, nk, step, 0)


def _diff_attn_body(*refs, use_rope, nk, tq, lam_init):
    refs = list(refs)
    q_ref, kt_ref, vt_ref, dl_ref = refs[:4]
    refs = refs[4:]
    if use_rope:
        rope_refs = refs[:3]
        refs = refs[3:]
    y_ref, m_ref, l_ref, acc_ref = refs
    q = q_ref[...]
    if use_rope:
        q = _rope(q, *rope_refs, 96, 32)
    lane = lax.broadcasted_iota(jnp.int32, q.shape, 1)
    q1 = jnp.where(lane < DIFF_HD, q, 0.0)
    q2 = jnp.where(lane >= DIFF_HD, q, 0.0)
    q_t = jnp.concatenate([q1, q2], axis=0).T.astype(BF16)
    _flash_keymajor(q_t, lambda i: kt_ref[0, 0, i], lambda i: vt_ref[0, 0, i], nk, m_ref, l_ref, acc_ref, None)
    o = acc_ref[...] * (1.0 / l_ref[...])
    dl = dl_ref[...]
    lam = (jnp.exp(jnp.sum(dl[0:1] * dl[1:2], axis=-1, keepdims=True))
           - jnp.exp(jnp.sum(dl[2:3] * dl[3:4], axis=-1, keepdims=True)) + lam_init)
    o = o[:, :tq] - lam * o[:, tq:]
    o = o * lax.rsqrt(jnp.mean(o * o, axis=0, keepdims=True) + EPS) * (1.0 - lam_init)
    y_ref[...] = o.T.astype(y_ref.dtype)


def _diff_attention(seg, b, s, kt, vt, dl, lam_init, rope):
    nk, tk = kt.shape[2], kt.shape[3]
    tq = min(DIFF_Q_TILE, s)
    nq = s // tq
    use_rope = rope is not None
    in_specs = [pl.BlockSpec((tq, 128), lambda bi, h, qi: (bi * nq + qi, h)),
                pl.BlockSpec((1, 1, nk, tk, 128), lambda bi, h, qi: (bi, h, 0, 0, 0)),
                pl.BlockSpec((1, 1, nk, 128, tk), lambda bi, h, qi: (bi, h, 0, 0, 0)),
                pl.BlockSpec((8, 128), lambda bi, h, qi: (0, 0))]
    args = [seg, kt, vt, dl]
    if use_rope:
        in_specs += [pl.BlockSpec((tq, 128), lambda bi, h, qi: (qi, 0))] * 3
        args += list(rope)
    return pl.pallas_call(
        functools.partial(_diff_attn_body, use_rope=use_rope, nk=nk, tq=tq, lam_init=lam_init),
        out_shape=jax.ShapeDtypeStruct((b * s, H_DIFF * 2 * DIFF_HD), BF16),
        grid=(b, H_DIFF, nq),
        in_specs=in_specs,
        out_specs=pl.BlockSpec((tq, 128), lambda bi, h, qi: (bi * nq + qi, h)),
        scratch_shapes=[pltpu.VMEM((1, 2 * tq), F32), pltpu.VMEM((1, 2 * tq), F32),
                        pltpu.VMEM((128, 2 * tq), F32)],
        compiler_params=_cparams("parallel", "parallel", "arbitrary"),
        name="diff_attn",
    )(*args)


def _mla_attn_body(*refs, use_rope, nk):
    refs = list(refs)
    q_ref, kt_ref, vt_ref = refs[:3]
    refs = refs[3:]
    if use_rope:
        rope_refs = refs[:3]
        refs = refs[3:]
    y_ref, m_ref, l_ref, acc_ref, o_ref = refs
    for j in range(2):
        q = q_ref[:, LANES * j:LANES * (j + 1)]
        if use_rope:
            q = _rope(q, *rope_refs, 112, 16)
        q_t = q.T.astype(BF16)
        _flash_keymajor(q_t, lambda i: kt_ref[0, j, i], lambda i: vt_ref[0, j, i], nk, m_ref, l_ref, acc_ref,
                        MLA_SCALE)
        o_ref[MLA_V * j:MLA_V * (j + 1), :] = acc_ref[...] * (1.0 / l_ref[...])
    y_ref[...] = o_ref[...].T.astype(y_ref.dtype)


def _mla_attention(q_all, b, s, kt, vt, rope):
    nk, tk = kt.shape[2], kt.shape[3]
    tq = min(MLA_Q_TILE, s)
    nq = s // tq
    use_rope = rope is not None
    in_specs = [pl.BlockSpec((tq, 2 * LANES), lambda bi, p, qi: (bi * nq + qi, p)),
                pl.BlockSpec((1, 2, nk, tk, LANES), lambda bi, p, qi: (bi, p, 0, 0, 0)),
                pl.BlockSpec((1, 2, nk, MLA_V, tk), lambda bi, p, qi: (bi, p, 0, 0, 0))]
    args = [q_all, kt, vt]
    if use_rope:
        in_specs += [pl.BlockSpec((tq, 128), lambda bi, p, qi: (qi, 0))] * 3
        args += list(rope)
    return pl.pallas_call(
        functools.partial(_mla_attn_body, use_rope=use_rope, nk=nk),
        out_shape=jax.ShapeDtypeStruct((b * s, H_MLA * MLA_V), BF16),
        grid=(b, H_MLA // 2, nq),
        in_specs=in_specs,
        out_specs=pl.BlockSpec((tq, 2 * MLA_V), lambda bi, p, qi: (bi * nq + qi, p)),
        scratch_shapes=[pltpu.VMEM((1, tq), F32), pltpu.VMEM((1, tq), F32),
                        pltpu.VMEM((MLA_V, tq), F32), pltpu.VMEM((2 * MLA_V, tq), F32)],
        compiler_params=_cparams("parallel", "parallel", "arbitrary"),
        name="mla_attn",
    )(*args)


def _merge_body(x_ref, yr_ref, yd_ref, ym_ref, gt_ref, mod_ref, wbr_ref, wo_ref, o_ref):
    merged = None
    for gi, y_ref in enumerate((yr_ref, yd_ref, ym_ref)):
        br = _dot(y_ref[...], wbr_ref[gi])
        term = _sigmoid(gt_ref[:, D_MODEL * gi:D_MODEL * (gi + 1)]) * br
        merged = term if merged is None else merged + term
    out = _dot(merged.astype(BF16), wo_ref[...])
    o_ref[...] = x_ref[...] + mod_ref[0][2:3, :] * out


def _merge(x, yr, yd, ym, gates, mod, mod_row, wbr, wo):
    t = x.shape[0]
    tm = min(TOKEN_TILE, t)
    row = lambda i: (i, 0)
    return pl.pallas_call(
        _merge_body,
        out_shape=jax.ShapeDtypeStruct((t, D_MODEL), F32),
        grid=(t // tm,),
        in_specs=[pl.BlockSpec((tm, D_MODEL), row),
                  pl.BlockSpec((tm, 512), row), pl.BlockSpec((tm, 512), row), pl.BlockSpec((tm, 512), row),
                  pl.BlockSpec((tm, 3 * D_MODEL), row),
                  pl.BlockSpec((1, 8, D_MODEL), lambda i: (mod_row(i), 0, 0)),
                  pl.BlockSpec(wbr.shape, lambda i: (0, 0, 0)),
                  pl.BlockSpec(wo.shape, lambda i: (0, 0))],
        out_specs=pl.BlockSpec((tm, D_MODEL), row),
        compiler_params=_cparams("parallel"),
        name="merge",
    )(x, yr, yd, ym, gates, mod, wbr, wo)


def _ffn_body(*refs, n_chunks, final):
    if final:
        x_ref, g_ref, mod_ref, wa_ref, wb_ref, wo_ref, fg_ref, o_ref, h_ref, acc_ref = refs
    else:
        x_ref, g_ref, mod_ref, wa_ref, wb_ref, wo_ref, o_ref, h_ref, acc_ref = refs
    c = pl.program_id(1)
    m = mod_ref[0]

    @pl.when(c == 0)
    def _():
        h = _rms(x_ref[...]) * g_ref[...]
        h_ref[...] = (h * (1.0 + m[4:5, :]) + m[3:4, :]).astype(BF16)
        acc_ref[...] = jnp.zeros(acc_ref.shape, F32)

    h = h_ref[...]
    u = _silu(_dot(h, wa_ref[...])) * _dot(h, wb_ref[...])
    acc_ref[...] += _dot(u.astype(BF16), wo_ref[...])

    @pl.when(c == n_chunks - 1)
    def _():
        xn = x_ref[...] + m[5:6, :] * acc_ref[...]
        if final:
            xn = _rms(xn) * fg_ref[...]
        o_ref[...] = xn


def _ffn(x, g, mod, mod_row, wa, wb, wo, final_g):
    t = x.shape[0]
    tm = min(TOKEN_TILE, t)
    fc = FFN_CHUNK
    n_chunks = D_FF // fc
    final = final_g is not None
    in_specs = [pl.BlockSpec((tm, D_MODEL), lambda i, c: (i, 0)),
                pl.BlockSpec((1, D_MODEL), lambda i, c: (0, 0)),
                pl.BlockSpec((1, 8, D_MODEL), lambda i, c: (mod_row(i), 0, 0)),
                pl.BlockSpec((D_MODEL, fc), lambda i, c: (0, c)),
                pl.BlockSpec((D_MODEL, fc), lambda i, c: (0, c)),
                pl.BlockSpec((fc, D_MODEL), lambda i, c: (c, 0))]
    args = [x, g.reshape(1, D_MODEL), mod, wa, wb, wo]
    if final:
        in_specs.append(pl.BlockSpec((1, D_MODEL), lambda i, c: (0, 0)))
        args.append(final_g.reshape(1, D_MODEL))
    return pl.pallas_call(
        functools.partial(_ffn_body, n_chunks=n_chunks, final=final),
        out_shape=jax.ShapeDtypeStruct((t, D_MODEL), F32),
        grid=(t // tm, n_chunks),
        in_specs=in_specs,
        out_specs=pl.BlockSpec((tm, D_MODEL), lambda i, c: (i, 0)),
        scratch_shapes=[pltpu.VMEM((tm, D_MODEL), BF16), pltpu.VMEM((tm, D_MODEL), F32)],
        compiler_params=_cparams("parallel", "arbitrary"),
        name="ffn",
    )(*args)


def _axial_angles(n_tokens, rot_dim):
    t = jnp.arange(n_tokens)
    row = (t // GRID_W).astype(F32)
    col = (t % GRID_W).astype(F32)
    nf = rot_dim // 4
    inv = ROPE_BASE ** (-jnp.arange(nf, dtype=F32) / nf)
    ang = jnp.concatenate([row[:, None] * inv, col[:, None] * inv], axis=-1)
    return jnp.cos(ang), jnp.sin(ang)


def _rope_tables_head64(n_tokens):
    cos, sin = _axial_angles(n_tokens, 64)
    zero = jnp.zeros_like(sin)
    c = jnp.tile(jnp.concatenate([cos, cos], axis=-1), (1, 2))
    sa = jnp.tile(jnp.concatenate([-sin, zero], axis=-1), (1, 2))
    sb = jnp.tile(jnp.concatenate([zero, sin], axis=-1), (1, 2))
    return c, sa, sb


def _rope_tables_mla(n_tokens):
    cos, sin = _axial_angles(n_tokens, MLA_ROPE)
    z16 = jnp.zeros_like(sin)
    one64 = jnp.ones((n_tokens, MLA_NOPE), F32)
    z64 = jnp.zeros((n_tokens, MLA_NOPE), F32)
    z32 = jnp.zeros((n_tokens, 32), F32)
    c = jnp.concatenate([one64, cos, cos, z32], axis=-1)
    sa = jnp.concatenate([z64, -sin, z16, z32], axis=-1)
    sb = jnp.concatenate([z64, z16, sin, z32], axis=-1)
    return c, sa, sb


def _layer_weights(l, w_in, w_uq, w_ukv, w_branch, w_out, w_ffn_in, w_ffn_out):
    wi = w_in[l]
    d = wi.shape[0]
    w_ret = wi[:, 0:1536]
    w_diff = wi[:, 1536:3072]
    w_mla = jnp.concatenate([wi[:, 3456:3712], jnp.zeros((d, 64), wi.dtype), wi[:, 3712:3744],
                             jnp.zeros((d, 32), wi.dtype), wi[:, 3072:3456]], axis=1)
    w_gate = wi[:, 3744:]
    hd = MLA_NOPE + MLA_ROPE
    wq = jnp.pad(w_uq[l].reshape(Q_LORA, H_MLA, hd), ((0, 0), (0, 0), (0, LANES - hd))).reshape(Q_LORA, H_MLA * LANES)
    wkv = w_ukv[l].reshape(KV_LORA, H_MLA, MLA_NOPE + MLA_V)
    wk = jnp.pad(wkv[:, :, :MLA_NOPE], ((0, 0), (0, 0), (0, LANES - MLA_NOPE))).reshape(KV_LORA, H_MLA * LANES)
    wv = wkv[:, :, MLA_NOPE:].reshape(KV_LORA, H_MLA * MLA_V)
    wa, wb = w_ffn_in[l][:, :D_FF], w_ffn_in[l][:, D_FF:]
    cast = lambda a: a.astype(BF16)
    return dict(ret=cast(w_ret), diff=cast(w_diff), mla=cast(w_mla), gate=cast(w_gate), wq=cast(wq), wk=cast(wk),
                wv=cast(wv), wbr=cast(w_branch[l]), wo=cast(w_out[l]), wa=cast(wa), wb=cast(wb),
                wf=cast(w_ffn_out[l]))


def _mixer_layer(x, b, s, l, lw, mod, mod_row, norm1_g, norm2_g, tabs, mla_q_norm, mla_kv_norm, dl, lam_init,
                 s0f, s0b, caches, ropes, final_g):
    tk = min(KEY_TILE, s)
    rope64, rope_mla = ropes if ropes is not None else (None, None)
    proj = functools.partial(_norm_matmul, x, 0, D_MODEL, norm1_g, mod=mod, mod_row=mod_row, shift_row=0, scale_row=1)
    seg_ret = proj(lw["ret"], tn=1536, name="proj_ret")
    seg_diff = proj(lw["diff"], tn=1536, name="proj_diff")
    seg_mla = proj(lw["mla"], tn=768, name="proj_mla")
    gates = proj(lw["gate"], tn=1536, name="proj_gate")

    y_ret, fin_f, fin_b = _retention(seg_ret, b, s, s0f, s0b, tabs, rope64)

    cache_k, cache_v, cache_ckv, cache_kpe = caches if caches is not None else (None,) * 4
    kt, vt = _diff_prep(seg_diff, b, s, tk, l, cache_k, cache_v, rope64)
    y_diff = _diff_attention(seg_diff, b, s, kt, vt, dl, lam_init, rope64)

    q_mla = _norm_matmul(seg_mla, 1, Q_LORA, mla_q_norm, lw["wq"], tn=H_MLA * LANES, name="mla_q")
    kt_m, vt_m, ckv_n = _mla_prep(seg_mla, b, s, tk, l, mla_kv_norm, lw["wk"], lw["wv"], cache_ckv, cache_kpe,
                                  rope_mla)
    y_mla = _mla_attention(q_mla, b, s, kt_m, vt_m, rope_mla)

    x = _merge(x, y_ret, y_diff, y_mla, gates, mod, mod_row, lw["wbr"], lw["wo"])
    x = _ffn(x, norm2_g, mod, mod_row, lw["wa"], lw["wb"], lw["wf"], final_g)
    return x, (fin_f, fin_b, seg_diff, seg_mla, ckv_n)


def kernel(x_prompt, x_sample, state_ret_fwd, state_ret_bwd, cache_diff_k, cache_diff_v, cache_mla_ckv,
           cache_mla_kpe, c, c_ctx, norm1_g, norm2_g, w_ada, b_ada, w_in, ret_decay_fwd, ret_decay_bwd,
           diff_lambda, mla_q_norm, mla_kv_norm, w_uq, w_ukv, w_branch, w_out, w_ffn_in, w_ffn_out, final_g):
    bp, sp, d = x_prompt.shape
    bs, ss, _ = x_sample.shape
    depth = w_in.shape[0]
    past = cache_diff_k.shape[2]

    n_rows = -(-(1 + bs) // 8) * 8
    cond = jnp.zeros((n_rows, d), F32).at[0].set(c_ctx).at[1:1 + bs].set(c)
    mod_all = _modulation(cond, w_ada.astype(BF16), b_ada)
    mod_all = jnp.pad(mod_all.reshape(depth, n_rows, 6, d), ((0, 0), (0, 0), (0, 2), (0, 0)))

    tile_p = min(TOKEN_TILE, bp * sp)
    tile_s = min(TOKEN_TILE, ss)
    assert ss % tile_s == 0 and (bp * sp) % tile_p == 0
    row_prompt = lambda i: 0
    row_sample = lambda i: 1 + (i * tile_s) // ss

    ropes = (_rope_tables_head64(ss), _rope_tables_mla(ss))
    cache_k = cache_diff_k.reshape(bs, depth, past, H_DIFF * 2 * DIFF_HD)
    cache_v = cache_diff_v.reshape(bs, depth, past, H_DIFF * 2 * DIFF_HD)
    cache_kpe = jnp.pad(cache_mla_kpe, ((0, 0), (0, 0), (0, 0), (MLA_NOPE, LANES - MLA_NOPE - MLA_ROPE)))
    zero_state = jnp.zeros((bp, H_RET, RET_DK, RET_DV), F32)

    xp = x_prompt.reshape(bp * sp, d)
    xs = x_sample.reshape(bs * ss, d)
    ret_f, ret_b, dks, dvs, ckvs, kpes = [], [], [], [], [], []
    for l in range(depth):
        lw = _layer_weights(l, w_in, w_uq, w_ukv, w_branch, w_out, w_ffn_in, w_ffn_out)
        lam_init = 0.8 - 0.6 * math.exp(-0.3 * l)
        tabs = _retention_tables(ret_decay_fwd[l], ret_decay_bwd[l])
        dl = jnp.pad(diff_lambda[l], ((0, 4), (0, LANES - DIFF_HD)))
        fg = final_g if l == depth - 1 else None
        common = (norm1_g[l], norm2_g[l], tabs, mla_q_norm[l], mla_kv_norm[l], dl, lam_init)

        xp, ctx = _mixer_layer(xp, bp, sp, l, lw, mod_all[l], row_prompt, *common, zero_state, zero_state,
                               None, None, fg)
        fin_f, fin_b, seg_diff, seg_mla, ckv_n = ctx
        ret_f.append(fin_f)
        ret_b.append(fin_b)
        dks.append(seg_diff[:, 512:1024].reshape(bp, sp, H_DIFF, 2 * DIFF_HD))
        dvs.append(seg_diff[:, 1024:1536].reshape(bp, sp, H_DIFF, 2 * DIFF_HD))
        ckvs.append(ckv_n.reshape(bp, sp, KV_LORA))
        kpes.append(seg_mla[:, 320:352].reshape(bp, sp, MLA_ROPE))

        xs, _ = _mixer_layer(xs, bs, ss, l, lw, mod_all[l], row_sample, *common, state_ret_fwd[:, l],
                             state_ret_bwd[:, l], (cache_k, cache_v, cache_mla_ckv, cache_kpe), ropes, fg)

    stack = lambda parts: jnp.stack(parts, axis=1)
    return (xp.reshape(bp, sp, d), xs.reshape(bs, ss, d), stack(ret_f), stack(ret_b), stack(dks), stack(dvs),
            stack(ckvs), stack(kpes))
```

```python
import functools
import math

import jax
import jax.numpy as jnp
from jax import lax
from jax.experimental import pallas as pl
from jax.experimental.pallas import tpu as pltpu

F32 = jnp.float32
BF16 = jnp.bfloat16

D_MODEL = 1024
GRID_W = 64
H_RET, RET_DK, RET_DV, RET_CHUNK = 4, 64, 128, 128
H_DIFF, DIFF_HD = 4, 64
H_MLA, MLA_NOPE, MLA_ROPE, MLA_V = 8, 64, 32, 64
Q_LORA, KV_LORA = 384, 256
D_FF = 2816
ROPE_BASE = 10000.0
EPS = 1e-6
MLA_SCALE = (MLA_NOPE + MLA_ROPE) ** -0.5
LANES = 128
NEG_BIG = -1e30
LOG2_E = math.log2(math.e)

TOKEN_TILE = 512
FFN_CHUNK = 1408
DIFF_Q_TILE = 512
DIFF_CHAIN_Q = 256
MLA_Q_TILE = 512
KEY_TILE = 512
RET_GROUP = 8
SUM_ROWS = 16
VMEM_LIMIT = 56 * 1024 * 1024


def _cparams(*sem, flags=None):
    return pltpu.CompilerParams(dimension_semantics=sem, vmem_limit_bytes=VMEM_LIMIT, flags=flags)


ATTN_FLAGS = None


def _dot(a, b):
    return jnp.dot(a, b, preferred_element_type=F32)


def _dot_nt(a, b):
    return lax.dot_general(a, b, (((1,), (1,)), ((), ())), preferred_element_type=F32)


def _rms(x):
    return x * lax.rsqrt(jnp.mean(x * x, axis=-1, keepdims=True) + EPS)


def _silu(x):
    return x * (1.0 / (1.0 + jnp.exp(-x)))


def _sigmoid(x):
    return 1.0 / (1.0 + jnp.exp(-x))


def _rope(x, c_ref, sa_ref, sb_ref, shift_a, shift_b, rows=slice(None)):
    return (x * c_ref[rows, :] + pltpu.roll(x, shift_a, 1) * sa_ref[rows, :]
            + pltpu.roll(x, shift_b, 1) * sb_ref[rows, :])


def _mod_body(c_ref, w_ref, b_ref, o_ref):
    o_ref[0] = _dot(_silu(c_ref[...]).astype(BF16), w_ref[0]) + b_ref[0]


def _modulation(cond, w_ada, b_ada):
    depth, d, n = w_ada.shape
    rows = cond.shape[0]
    tn = 1536
    return pl.pallas_call(
        _mod_body,
        out_shape=jax.ShapeDtypeStruct((depth, rows, n), F32),
        grid=(depth, n // tn),
        in_specs=[pl.BlockSpec((rows, d), lambda l, j: (0, 0)),
                  pl.BlockSpec((1, d, tn), lambda l, j: (l, 0, j)),
                  pl.BlockSpec((1, 1, tn), lambda l, j: (l, 0, j))],
        out_specs=pl.BlockSpec((1, rows, tn), lambda l, j: (l, 0, j)),
        compiler_params=_cparams("parallel", "parallel"),
        name="adaln_mod",
    )(cond, w_ada, b_ada.reshape(depth, 1, n))


def _norm_matmul_body(*refs, has_mod, shift_row, scale_row):
    if has_mod:
        x_ref, g_ref, mod_ref, w_ref, o_ref = refs
    else:
        x_ref, g_ref, w_ref, o_ref = refs
    h = _rms(x_ref[...]) * g_ref[...]
    if has_mod:
        m = mod_ref[0]
        h = h * (1.0 + m[scale_row:scale_row + 1, :]) + m[shift_row:shift_row + 1, :]
    o_ref[...] = _dot(h.astype(BF16), w_ref[...]).astype(o_ref.dtype)


def _norm_matmul(x, x_col, k, g, w, *, mod=None, mod_row=None, shift_row=0, scale_row=1,
                 tn, out_dtype=F32, name):
    t = x.shape[0]
    n = w.shape[1]
    tm = min(TOKEN_TILE, t)
    assert t % tm == 0 and n % tn == 0
    in_specs = [pl.BlockSpec((tm, k), lambda j, i: (i, x_col)),
                pl.BlockSpec((1, k), lambda j, i: (0, 0))]
    args = [x, g.reshape(1, k)]
    if mod is not None:
        in_specs.append(pl.BlockSpec((1, 8, k), lambda j, i: (mod_row(i), 0, 0)))
        args.append(mod)
    in_specs.append(pl.BlockSpec((k, tn), lambda j, i: (0, j)))
    args.append(w)
    return pl.pallas_call(
        functools.partial(_norm_matmul_body, has_mod=mod is not None,
                          shift_row=shift_row, scale_row=scale_row),
        out_shape=jax.ShapeDtypeStruct((t, n), out_dtype),
        grid=(n // tn, t // tm),
        in_specs=in_specs,
        out_specs=pl.BlockSpec((tm, tn), lambda j, i: (i, j)),
        compiler_params=_cparams("parallel", "parallel"),
        name=name,
    )(*args)


def _ret_scan_body(*refs, use_rope, n_steps, group):
    if use_rope:
        (gc_ref, kf_ref, vf_ref, kb_ref, vb_ref, zf_ref, zb_ref, s0f_ref, s0b_ref,
         cf_ref, saf_ref, sbf_ref, cb_ref, sab_ref, sbb_ref,
         sf_ref, sb_ref, finf_ref, finb_ref, st_ref) = refs
        rope_f, rope_b = (cf_ref, saf_ref, sbf_ref), (cb_ref, sab_ref, sbb_ref)
    else:
        (gc_ref, kf_ref, vf_ref, kb_ref, vb_ref, zf_ref, zb_ref, s0f_ref, s0b_ref,
         sf_ref, sb_ref, finf_ref, finb_ref, st_ref) = refs
        rope_f = rope_b = None
    p = pl.program_id(1)
    t = pl.program_id(2)
    c = RET_CHUNK

    @pl.when(t == 0)
    def _():
        st_ref[0] = s0f_ref[0]
        st_ref[1] = s0b_ref[0]

    def update(d, gi, k_ref, v_ref, z_ref, rope_refs, out_ref):
        rows = slice(gi * c, (gi + 1) * c)
        out_ref[0, :, gi] = st_ref[d]
        k = k_ref[rows, :]
        if use_rope:
            k = _rope(k, *rope_refs, 96, 32, rows)
        kz_t = (k * (RET_DK ** -0.5) * z_ref[0]).T.astype(BF16)
        v = v_ref[rows, :].astype(BF16)
        for j in range(2):
            loc = _dot(kz_t[RET_DK * j:RET_DK * (j + 1)], v[:, RET_DV * j:RET_DV * (j + 1)])
            st_ref[d, j] = gc_ref[2 * p + j, d] * st_ref[d, j] + loc

    for gi in range(group):
        update(0, gi, kf_ref, vf_ref, zf_ref, rope_f, sf_ref)
    for gi in reversed(range(group)):
        update(1, gi, kb_ref, vb_ref, zb_ref, rope_b, sb_ref)

    @pl.when(t == n_steps - 1)
    def _():
        finf_ref[0] = st_ref[0]
        finb_ref[0] = st_ref[1]


def _ret_out_body(*refs, use_rope, group):
    if use_rope:
        (q_ref, k_ref, v_ref, g_ref, sf_ref, sb_ref, dm_ref, xif_ref, xib_ref,
         c_ref, sa_ref, sbt_ref, y_ref) = refs
    else:
        (q_ref, k_ref, v_ref, g_ref, sf_ref, sb_ref, dm_ref, xif_ref, xib_ref, y_ref) = refs
    c = RET_CHUNK
    lane = lax.broadcasted_iota(jnp.int32, (c, 2 * RET_DK), 1)
    for gi in range(group):
        rows = slice(gi * c, (gi + 1) * c)
        q = q_ref[rows, :]
        k = k_ref[rows, :]
        if use_rope:
            q = _rope(q, c_ref, sa_ref, sbt_ref, 96, 32, rows)
            k = _rope(k, c_ref, sa_ref, sbt_ref, 96, 32, rows)
        kb = (k * (RET_DK ** -0.5)).astype(BF16)
        v = v_ref[rows, :].astype(BF16)
        g = g_ref[rows, :]
        s_f = sf_ref[0, :, gi].reshape(2 * RET_DK, RET_DV).astype(BF16)
        s_b = sb_ref[0, :, gi].reshape(2 * RET_DK, RET_DV).astype(BF16)
        for j in range(2):
            in_head = (lane >= RET_DK * j) & (lane < RET_DK * (j + 1))
            qm = jnp.where(in_head, q, 0.0).astype(BF16)
            sc = _dot_nt(qm, kb) * dm_ref[j]
            o = _dot(sc.astype(BF16), v[:, RET_DV * j:RET_DV * (j + 1)])
            o = o + _dot(qm, s_f) * xif_ref[j] + _dot(qm, s_b) * xib_ref[j]
            mu = jnp.mean(o, axis=-1, keepdims=True)
            oc = o - mu
            on = oc * lax.rsqrt(jnp.mean(oc * oc, axis=-1, keepdims=True) + EPS)
            gj = g[:, RET_DV * j:RET_DV * (j + 1)]
            y_ref[rows, RET_DV * j:RET_DV * (j + 1)] = (_silu(gj) * on).astype(y_ref.dtype)


def _retention(seg, b, s, s0f, s0b, tabs, rope):
    c = RET_CHUNK
    n = s // c
    group = math.gcd(n, RET_GROUP)
    ns = n // group
    rows = group * c
    use_rope = rope is not None
    gc, zf, zb, dmat, xif, xib = tabs
    state_shape = jax.ShapeDtypeStruct((b, H_RET, n, RET_DK, RET_DV), F32)
    fin_shape = jax.ShapeDtypeStruct((b, H_RET, RET_DK, RET_DV), F32)

    def fwd(bi, t):
        return bi * ns + t

    def bwd(bi, t):
        return bi * ns + (ns - 1 - t)

    in_specs = [
        pl.BlockSpec(memory_space=pltpu.SMEM),
        pl.BlockSpec((rows, 128), lambda bi, p, t: (fwd(bi, t), 2 + p)),
        pl.BlockSpec((rows, 256), lambda bi, p, t: (fwd(bi, t), 2 + p)),
        pl.BlockSpec((rows, 128), lambda bi, p, t: (bwd(bi, t), 2 + p)),
        pl.BlockSpec((rows, 256), lambda bi, p, t: (bwd(bi, t), 2 + p)),
        pl.BlockSpec((1, c, 128), lambda bi, p, t: (p, 0, 0)),
        pl.BlockSpec((1, c, 128), lambda bi, p, t: (p, 0, 0)),
        pl.BlockSpec((1, 2, RET_DK, RET_DV), lambda bi, p, t: (bi, p, 0, 0)),
        pl.BlockSpec((1, 2, RET_DK, RET_DV), lambda bi, p, t: (bi, p, 0, 0)),
    ]
    args = [gc, seg, seg, seg, seg, zf, zb, s0f, s0b]
    if use_rope:
        in_specs += [pl.BlockSpec((rows, 128), lambda bi, p, t: (t, 0))] * 3
        in_specs += [pl.BlockSpec((rows, 128), lambda bi, p, t: (ns - 1 - t, 0))] * 3
        args += list(rope) + list(rope)
    sf, sb, fin_f, fin_b = pl.pallas_call(
        functools.partial(_ret_scan_body, use_rope=use_rope, n_steps=ns, group=group),
        out_shape=(state_shape, state_shape, fin_shape, fin_shape),
        grid=(b, 2, ns),
        in_specs=in_specs,
        out_specs=(pl.BlockSpec((1, 2, group, RET_DK, RET_DV), lambda bi, p, t: (bi, p, t, 0, 0)),
                   pl.BlockSpec((1, 2, group, RET_DK, RET_DV), lambda bi, p, t: (bi, p, ns - 1 - t, 0, 0)),
                   pl.BlockSpec((1, 2, RET_DK, RET_DV), lambda bi, p, t: (bi, p, 0, 0)),
                   pl.BlockSpec((1, 2, RET_DK, RET_DV), lambda bi, p, t: (bi, p, 0, 0))),
        scratch_shapes=[pltpu.VMEM((2, 2, RET_DK, RET_DV), F32)],
        compiler_params=_cparams("parallel", "parallel", "arbitrary"),
        name="ret_scan",
    )(*args)

    in_specs = [
        pl.BlockSpec((rows, 128), lambda bi, p, t: (fwd(bi, t), p)),
        pl.BlockSpec((rows, 128), lambda bi, p, t: (fwd(bi, t), 2 + p)),
        pl.BlockSpec((rows, 256), lambda bi, p, t: (fwd(bi, t), 2 + p)),
        pl.BlockSpec((rows, 256), lambda bi, p, t: (fwd(bi, t), 4 + p)),
        pl.BlockSpec((1, 2, group, RET_DK, RET_DV), lambda bi, p, t: (bi, p, t, 0, 0)),
        pl.BlockSpec((1, 2, group, RET_DK, RET_DV), lambda bi, p, t: (bi, p, t, 0, 0)),
        pl.BlockSpec((2, c, c), lambda bi, p, t: (p, 0, 0)),
        pl.BlockSpec((2, c, 128), lambda bi, p, t: (p, 0, 0)),
        pl.BlockSpec((2, c, 128), lambda bi, p, t: (p, 0, 0)),
    ]
    args = [seg, seg, seg, seg, sf, sb, dmat, xif, xib]
    if use_rope:
        in_specs += [pl.BlockSpec((rows, 128), lambda bi, p, t: (t, 0))] * 3
        args += list(rope)
    y = pl.pallas_call(
        functools.partial(_ret_out_body, use_rope=use_rope, group=group),
        out_shape=jax.ShapeDtypeStruct((b * s, H_RET * RET_DV), BF16),
        grid=(b, 2, ns),
        in_specs=in_specs,
        out_specs=pl.BlockSpec((rows, 256), lambda bi, p, t: (fwd(bi, t), p)),
        compiler_params=_cparams("parallel", "parallel", "parallel"),
        name="ret_out",
    )(*args)
    return y, fin_f, fin_b


def _retention_tables(decay_f, decay_b):
    c = RET_CHUNK
    lg_f = jax.nn.log_sigmoid(decay_f.astype(F32))
    lg_b = jax.nn.log_sigmoid(decay_b.astype(F32))
    i = jnp.arange(c, dtype=F32)
    dist = i[:, None] - i[None, :]
    d_f = jnp.where(dist >= 0, jnp.exp(jnp.maximum(dist, 0.0)[None] * lg_f[:, None, None]), 0.0)
    d_b = jnp.where(dist < 0, jnp.exp(jnp.maximum(-dist, 0.0)[None] * lg_b[:, None, None]), 0.0)
    dmat = d_f + d_b
    xi_f = jnp.exp((i + 1.0)[None, :] * lg_f[:, None])
    xi_b = jnp.exp((c - i)[None, :] * lg_b[:, None])
    zeta_f = jnp.exp((c - 1.0 - i)[None, :] * lg_f[:, None])
    zeta_b = jnp.exp(i[None, :] * lg_b[:, None])
    gc = jnp.stack([jnp.exp(c * lg_f), jnp.exp(c * lg_b)], axis=1)

    def lanes(tab, width):
        return jnp.broadcast_to(tab[:, :, None], tab.shape + (width,))

    def pair(tab):
        t64 = lanes(tab, RET_DK).reshape(H_RET // 2, 2, c, RET_DK)
        return jnp.concatenate([t64[:, 0], t64[:, 1]], axis=-1)

    return gc, pair(zeta_f), pair(zeta_b), dmat, lanes(xi_f, RET_DV), lanes(xi_b, RET_DV)


def _diff_prep_body(*refs, n_cache, use_rope):
    refs = list(refs)
    if n_cache:
        ck_ref, cv_ref = refs[:2]
        refs = refs[2:]
    k_ref, v_ref = refs[:2]
    refs = refs[2:]
    if use_rope:
        rope_refs = refs[:3]
        refs = refs[3:]
    kt_ref, vt_ref = refs
    c = pl.program_id(2)
    dv = 2 * DIFF_HD
    ones = jnp.ones((SUM_ROWS, vt_ref.shape[-1]), BF16)

    def from_new():
        k = k_ref[...]
        if use_rope:
            k = _rope(k, *rope_refs, 96, 32)
        kt_ref[0, 0, 0] = (k * (DIFF_HD ** -0.5)).astype(BF16)
        vt_ref[0, 0, 0, :dv, :] = v_ref[...].T.astype(BF16)
        vt_ref[0, 0, 0, dv:, :] = ones

    def from_cache():
        kt_ref[0, 0, 0] = (ck_ref[0, 0] * (DIFF_HD ** -0.5)).astype(BF16)
        vt_ref[0, 0, 0, :dv, :] = cv_ref[0, 0].T.astype(BF16)
        vt_ref[0, 0, 0, dv:, :] = ones

    if n_cache:
        pl.when(c < n_cache)(from_cache)
        pl.when(c >= n_cache)(from_new)
    else:
        from_new()


def _diff_prep(seg, b, s, tk, layer, cache_k, cache_v, rope):
    n_new = s // tk
    n_cache = 0 if cache_k is None else cache_k.shape[2] // tk
    nk = n_cache + n_new
    use_rope = rope is not None

    def new_row(bi, c):
        return bi * n_new + jnp.maximum(c - n_cache, 0)

    in_specs, args = [], []
    if n_cache:
        spec = pl.BlockSpec((1, 1, tk, 128), lambda bi, h, c: (bi, layer, jnp.minimum(c, n_cache - 1), h))
        in_specs += [spec, spec]
        args += [cache_k, cache_v]
    in_specs += [pl.BlockSpec((tk, 128), lambda bi, h, c: (new_row(bi, c), H_DIFF + h)),
                 pl.BlockSpec((tk, 128), lambda bi, h, c: (new_row(bi, c), 2 * H_DIFF + h))]
    args += [seg, seg]
    if use_rope:
        in_specs += [pl.BlockSpec((tk, 128), lambda bi, h, c: (jnp.maximum(c - n_cache, 0), 0))] * 3
        args += list(rope)
    return pl.pallas_call(
        functools.partial(_diff_prep_body, n_cache=n_cache, use_rope=use_rope),
        out_shape=(jax.ShapeDtypeStruct((b, H_DIFF, nk, tk, 128), BF16),
                   jax.ShapeDtypeStruct((b, H_DIFF, nk, 128 + SUM_ROWS, tk), BF16)),
        grid=(b, H_DIFF, nk),
        in_specs=in_specs,
        out_specs=(pl.BlockSpec((1, 1, 1, tk, 128), lambda bi, h, c: (bi, h, c, 0, 0)),
                   pl.BlockSpec((1, 1, 1, 128 + SUM_ROWS, tk), lambda bi, h, c: (bi, h, c, 0, 0))),
        compiler_params=_cparams("parallel", "parallel", "arbitrary"),
        name="diff_prep",
    )(*args)


def _mla_prep_body(*refs, n_cache, use_rope, emit_ckv):
    refs = list(refs)
    if n_cache:
        cc_ref, cp_ref = refs[:2]
        refs = refs[2:]
    ckv_ref, kpe_ref, g_ref, wk_ref, wv_ref = refs[:5]
    refs = refs[5:]
    if use_rope:
        rope_refs = refs[:3]
        refs = refs[3:]
    kt_ref, vt_ref = refs[:2]
    c = pl.program_id(1)

    def emit(cn, kp):
        cb = cn.astype(BF16)
        kfull = _dot(cb, wk_ref[...])
        vt = _dot(cb, wv_ref[...]).T
        ones = jnp.ones((SUM_ROWS, vt.shape[-1]), BF16)
        for h in range(H_MLA):
            kt_ref[0, h, 0] = (kfull[:, LANES * h:LANES * (h + 1)] + kp).astype(BF16)
            vt_ref[0, h, 0, :MLA_V, :] = vt[MLA_V * h:MLA_V * (h + 1)].astype(BF16)
            vt_ref[0, h, 0, MLA_V:, :] = ones

    def from_new():
        cn = _rms(ckv_ref[...]) * g_ref[...]
        kp = kpe_ref[...]
        if use_rope:
            kp = _rope(kp, *rope_refs, 112, 16)
        if emit_ckv:
            refs[2][...] = cn
        emit(cn, kp)

    def from_cache():
        emit(cc_ref[0, 0], cp_ref[0, 0])

    if n_cache:
        pl.when(c < n_cache)(from_cache)
        pl.when(c >= n_cache)(from_new)
    else:
        from_new()


def _mla_prep(seg, b, s, tk, layer, g, wk, wv, cache_ckv, cache_kpe, rope):
    n_new = s // tk
    n_cache = 0 if cache_ckv is None else cache_ckv.shape[2] // tk
    nk = n_cache + n_new
    use_rope = rope is not None
    emit_ckv = n_cache == 0

    def new_row(bi, c):
        return bi * n_new + jnp.maximum(c - n_cache, 0)

    in_specs, args = [], []
    if n_cache:
        in_specs += [pl.BlockSpec((1, 1, tk, KV_LORA), lambda bi, c: (bi, layer, jnp.minimum(c, n_cache - 1), 0)),
                     pl.BlockSpec((1, 1, tk, LANES), lambda bi, c: (bi, layer, jnp.minimum(c, n_cache - 1), 0))]
        args += [cache_ckv, cache_kpe]
    in_specs += [pl.BlockSpec((tk, KV_LORA), lambda bi, c: (new_row(bi, c), 0)),
                 pl.BlockSpec((tk, LANES), lambda bi, c: (new_row(bi, c), 2)),
                 pl.BlockSpec((1, KV_LORA), lambda bi, c: (0, 0)),
                 pl.BlockSpec(wk.shape, lambda bi, c: (0, 0)),
                 pl.BlockSpec(wv.shape, lambda bi, c: (0, 0))]
    args += [seg, seg, g.reshape(1, KV_LORA), wk, wv]
    if use_rope:
        in_specs += [pl.BlockSpec((tk, 128), lambda bi, c: (jnp.maximum(c - n_cache, 0), 0))] * 3
        args += list(rope)
    out_shape = [jax.ShapeDtypeStruct((b, H_MLA, nk, tk, LANES), BF16),
                 jax.ShapeDtypeStruct((b, H_MLA, nk, MLA_V + SUM_ROWS, tk), BF16)]
    out_specs = [pl.BlockSpec((1, H_MLA, 1, tk, LANES), lambda bi, c: (bi, 0, c, 0, 0)),
                 pl.BlockSpec((1, H_MLA, 1, MLA_V + SUM_ROWS, tk), lambda bi, c: (bi, 0, c, 0, 0))]
    if emit_ckv:
        out_shape.append(jax.ShapeDtypeStruct((b * s, KV_LORA), F32))
        out_specs.append(pl.BlockSpec((tk, KV_LORA), lambda bi, c: (new_row(bi, c), 0)))
    outs = pl.pallas_call(
        functools.partial(_mla_prep_body, n_cache=n_cache, use_rope=use_rope, emit_ckv=emit_ckv),
        out_shape=tuple(out_shape),
        grid=(b, nk),
        in_specs=in_specs,
        out_specs=tuple(out_specs),
        compiler_params=_cparams("parallel", "arbitrary"),
        name="mla_prep",
    )(*args)
    return outs if emit_ckv else (outs[0], outs[1], None)


def _flash_keymajor(chains, nk, dv, scale):
    c = scale * LOG2_E
    m, acc, s_next = [], [], []
    for q_t, k_tile, _ in chains:
        n = q_t.shape[1]
        m.append(jnp.full((1, n), NEG_BIG, F32))
        acc.append(jnp.zeros((dv + SUM_ROWS, n), F32))
        s_next.append(_dot(k_tile(0), q_t))
    for i in range(nk):
        for ci, (q_t, k_tile, vt_tile) in enumerate(chains):
            s = s_next[ci]
            if i + 1 < nk:
                s_next[ci] = _dot(k_tile(i + 1), q_t)
            m_new = jnp.maximum(m[ci], jnp.max(s, axis=0, keepdims=True))
            alpha = jnp.exp2((m[ci] - m_new) * c)
            p = jnp.exp2((s - m_new) * c)
            acc[ci] = alpha * acc[ci] + _dot(vt_tile(i), p.astype(BF16))
            m[ci] = m_new
    return [(a[dv:dv + 1], a[:dv]) for a in acc]


def _diff_attn_body(*refs, use_rope, nk, tq, lam_init):
    refs = list(refs)
    q_ref, kt_ref, vt_ref, dl_ref = refs[:4]
    refs = refs[4:]
    if use_rope:
        rope_refs = refs[:3]
        refs = refs[3:]
    (y_ref,) = refs
    cq = min(tq, DIFF_CHAIN_Q)
    lane = lax.broadcasted_iota(jnp.int32, (cq, 2 * DIFF_HD), 1)
    chains = []
    for r in range(0, tq, cq):
        rows = slice(r, r + cq)
        q = q_ref[rows, :]
        if use_rope:
            q = _rope(q, *rope_refs, 96, 32, rows)
        q1 = jnp.where(lane < DIFF_HD, q, 0.0)
        q2 = jnp.where(lane >= DIFF_HD, q, 0.0)
        q_t = jnp.concatenate([q1, q2], axis=0).T.astype(BF16)
        chains.append((q_t, lambda i: kt_ref[0, 0, i], lambda i: vt_ref[0, 0, i]))
    results = _flash_keymajor(chains, nk, 2 * DIFF_HD, 1.0)
    dl = dl_ref[...]
    lam = (jnp.exp(jnp.sum(dl[0:1] * dl[1:2], axis=-1, keepdims=True))
           - jnp.exp(jnp.sum(dl[2:3] * dl[3:4], axis=-1, keepdims=True)) + lam_init)
    for ci, (l, acc) in enumerate(results):
        o = acc * (1.0 / l)
        o = o[:, :cq] - lam * o[:, cq:]
        o = o * lax.rsqrt(jnp.mean(o * o, axis=0, keepdims=True) + EPS) * (1.0 - lam_init)
        y_ref[ci * cq:(ci + 1) * cq, :] = o.T.astype(y_ref.dtype)


def _diff_attention(seg, b, s, kt, vt, dl, lam_init, rope):
    nk, tk = kt.shape[2], kt.shape[3]
    tq = min(DIFF_Q_TILE, s)
    nq = s // tq
    use_rope = rope is not None
    in_specs = [pl.BlockSpec((tq, 128), lambda bi, h, qi: (bi * nq + qi, h)),
                pl.BlockSpec((1, 1, nk, tk, 128), lambda bi, h, qi: (bi, h, 0, 0, 0)),
                pl.BlockSpec((1, 1, nk, 128 + SUM_ROWS, tk), lambda bi, h, qi: (bi, h, 0, 0, 0)),
                pl.BlockSpec((8, 128), lambda bi, h, qi: (0, 0))]
    args = [seg, kt, vt, dl]
    if use_rope:
        in_specs += [pl.BlockSpec((tq, 128), lambda bi, h, qi: (qi, 0))] * 3
        args += list(rope)
    return pl.pallas_call(
        functools.partial(_diff_attn_body, use_rope=use_rope, nk=nk, tq=tq, lam_init=lam_init),
        out_shape=jax.ShapeDtypeStruct((b * s, H_DIFF * 2 * DIFF_HD), BF16),
        grid=(b, H_DIFF, nq),
        in_specs=in_specs,
        out_specs=pl.BlockSpec((tq, 128), lambda bi, h, qi: (bi * nq + qi, h)),
        compiler_params=_cparams("parallel", "parallel", "arbitrary", flags=ATTN_FLAGS),
        name="diff_attn",
    )(*args)


def _mla_attn_body(*refs, use_rope, nk):
    refs = list(refs)
    q_ref, kt_ref, vt_ref = refs[:3]
    refs = refs[3:]
    if use_rope:
        rope_refs = refs[:3]
        refs = refs[3:]
    (y_ref,) = refs
    chains = []
    for j in range(2):
        q = q_ref[:, LANES * j:LANES * (j + 1)]
        if use_rope:
            q = _rope(q, *rope_refs, 112, 16)
        chains.append((q.T.astype(BF16), lambda i, j=j: kt_ref[0, j, i], lambda i, j=j: vt_ref[0, j, i]))
    outs = [acc * (1.0 / l) for l, acc in _flash_keymajor(chains, nk, MLA_V, MLA_SCALE)]
    y_ref[...] = jnp.concatenate(outs, axis=0).T.astype(y_ref.dtype)


def _mla_attention(q_all, b, s, kt, vt, rope):
    nk, tk = kt.shape[2], kt.shape[3]
    tq = min(MLA_Q_TILE, s)
    nq = s // tq
    use_rope = rope is not None
    in_specs = [pl.BlockSpec((tq, 2 * LANES), lambda bi, p, qi: (bi * nq + qi, p)),
                pl.BlockSpec((1, 2, nk, tk, LANES), lambda bi, p, qi: (bi, p, 0, 0, 0)),
                pl.BlockSpec((1, 2, nk, MLA_V + SUM_ROWS, tk), lambda bi, p, qi: (bi, p, 0, 0, 0))]
    args = [q_all, kt, vt]
    if use_rope:
        in_specs += [pl.BlockSpec((tq, 128), lambda bi, p, qi: (qi, 0))] * 3
        args += list(rope)
    return pl.pallas_call(
        functools.partial(_mla_attn_body, use_rope=use_rope, nk=nk),
        out_shape=jax.ShapeDtypeStruct((b * s, H_MLA * MLA_V), BF16),
        grid=(b, H_MLA // 2, nq),
        in_specs=in_specs,
        out_specs=pl.BlockSpec((tq, 2 * MLA_V), lambda bi, p, qi: (bi * nq + qi, p)),
        compiler_params=_cparams("parallel", "parallel", "arbitrary", flags=ATTN_FLAGS),
        name="mla_attn",
    )(*args)


def _merge_body(x_ref, yr_ref, yd_ref, ym_ref, gt_ref, mod_ref, wbr_ref, wo_ref, o_ref):
    merged = None
    for gi, y_ref in enumerate((yr_ref, yd_ref, ym_ref)):
        br = _dot(y_ref[...], wbr_ref[gi])
        term = _sigmoid(gt_ref[:, D_MODEL * gi:D_MODEL * (gi + 1)]) * br
        merged = term if merged is None else merged + term
    out = _dot(merged.astype(BF16), wo_ref[...])
    o_ref[...] = x_ref[...] + mod_ref[0][2:3, :] * out


def _merge(x, yr, yd, ym, gates, mod, mod_row, wbr, wo):
    t = x.shape[0]
    tm = min(TOKEN_TILE, t)
    row = lambda i: (i, 0)
    return pl.pallas_call(
        _merge_body,
        out_shape=jax.ShapeDtypeStruct((t, D_MODEL), F32),
        grid=(t // tm,),
        in_specs=[pl.BlockSpec((tm, D_MODEL), row),
                  pl.BlockSpec((tm, 512), row), pl.BlockSpec((tm, 512), row), pl.BlockSpec((tm, 512), row),
                  pl.BlockSpec((tm, 3 * D_MODEL), row),
                  pl.BlockSpec((1, 8, D_MODEL), lambda i: (mod_row(i), 0, 0)),
                  pl.BlockSpec(wbr.shape, lambda i: (0, 0, 0)),
                  pl.BlockSpec(wo.shape, lambda i: (0, 0))],
        out_specs=pl.BlockSpec((tm, D_MODEL), row),
        compiler_params=_cparams("parallel"),
        name="merge",
    )(x, yr, yd, ym, gates, mod, wbr, wo)


def _ffn_body(*refs, n_chunks, final):
    if final:
        x_ref, g_ref, mod_ref, wa_ref, wb_ref, wo_ref, fg_ref, o_ref, h_ref, acc_ref = refs
    else:
        x_ref, g_ref, mod_ref, wa_ref, wb_ref, wo_ref, o_ref, h_ref, acc_ref = refs
    c = pl.program_id(1)
    m = mod_ref[0]

    @pl.when(c == 0)
    def _():
        h = _rms(x_ref[...]) * g_ref[...]
        h_ref[...] = (h * (1.0 + m[4:5, :]) + m[3:4, :]).astype(BF16)
        acc_ref[...] = jnp.zeros(acc_ref.shape, F32)

    h = h_ref[...]
    u = _silu(_dot(h, wa_ref[...])) * _dot(h, wb_ref[...])
    acc_ref[...] += _dot(u.astype(BF16), wo_ref[...])

    @pl.when(c == n_chunks - 1)
    def _():
        xn = x_ref[...] + m[5:6, :] * acc_ref[...]
        if final:
            xn = _rms(xn) * fg_ref[...]
        o_ref[...] = xn


def _ffn(x, g, mod, mod_row, wa, wb, wo, final_g):
    t = x.shape[0]
    tm = min(TOKEN_TILE, t)
    fc = FFN_CHUNK
    n_chunks = D_FF // fc
    final = final_g is not None
    in_specs = [pl.BlockSpec((tm, D_MODEL), lambda i, c: (i, 0)),
                pl.BlockSpec((1, D_MODEL), lambda i, c: (0, 0)),
                pl.BlockSpec((1, 8, D_MODEL), lambda i, c: (mod_row(i), 0, 0)),
                pl.BlockSpec((D_MODEL, fc), lambda i, c: (0, c)),
                pl.BlockSpec((D_MODEL, fc), lambda i, c: (0, c)),
                pl.BlockSpec((fc, D_MODEL), lambda i, c: (c, 0))]
    args = [x, g.reshape(1, D_MODEL), mod, wa, wb, wo]
    if final:
        in_specs.append(pl.BlockSpec((1, D_MODEL), lambda i, c: (0, 0)))
        args.append(final_g.reshape(1, D_MODEL))
    return pl.pallas_call(
        functools.partial(_ffn_body, n_chunks=n_chunks, final=final),
        out_shape=jax.ShapeDtypeStruct((t, D_MODEL), F32),
        grid=(t // tm, n_chunks),
        in_specs=in_specs,
        out_specs=pl.BlockSpec((tm, D_MODEL), lambda i, c: (i, 0)),
        scratch_shapes=[pltpu.VMEM((tm, D_MODEL), BF16), pltpu.VMEM((tm, D_MODEL), F32)],
        compiler_params=_cparams("parallel", "arbitrary"),
        name="ffn",
    )(*args)


def _axial_angles(n_tokens, rot_dim):
    t = jnp.arange(n_tokens)
    row = (t // GRID_W).astype(F32)
    col = (t % GRID_W).astype(F32)
    nf = rot_dim // 4
    inv = ROPE_BASE ** (-jnp.arange(nf, dtype=F32) / nf)
    ang = jnp.concatenate([row[:, None] * inv, col[:, None] * inv], axis=-1)
    return jnp.cos(ang), jnp.sin(ang)


def _rope_tables_head64(n_tokens):
    cos, sin = _axial_angles(n_tokens, 64)
    zero = jnp.zeros_like(sin)
    c = jnp.tile(jnp.concatenate([cos, cos], axis=-1), (1, 2))
    sa = jnp.tile(jnp.concatenate([-sin, zero], axis=-1), (1, 2))
    sb = jnp.tile(jnp.concatenate([zero, sin], axis=-1), (1, 2))
    return c, sa, sb


def _rope_tables_mla(n_tokens):
    cos, sin = _axial_angles(n_tokens, MLA_ROPE)
    z16 = jnp.zeros_like(sin)
    one64 = jnp.ones((n_tokens, MLA_NOPE), F32)
    z64 = jnp.zeros((n_tokens, MLA_NOPE), F32)
    z32 = jnp.zeros((n_tokens, 32), F32)
    c = jnp.concatenate([one64, cos, cos, z32], axis=-1)
    sa = jnp.concatenate([z64, -sin, z16, z32], axis=-1)
    sb = jnp.concatenate([z64, z16, sin, z32], axis=-1)
    return c, sa, sb


def _layer_weights(l, w_in, w_uq, w_ukv, w_branch, w_out, w_ffn_in, w_ffn_out):
    wi = w_in[l]
    d = wi.shape[0]
    w_ret = wi[:, 0:1536]
    w_diff = wi[:, 1536:3072]
    w_mla = jnp.concatenate([wi[:, 3456:3712], jnp.zeros((d, 64), wi.dtype), wi[:, 3712:3744],
                             jnp.zeros((d, 32), wi.dtype), wi[:, 3072:3456]], axis=1)
    w_gate = wi[:, 3744:]
    hd = MLA_NOPE + MLA_ROPE
    wq = jnp.pad(w_uq[l].reshape(Q_LORA, H_MLA, hd), ((0, 0), (0, 0), (0, LANES - hd))).reshape(Q_LORA, H_MLA * LANES)
    wkv = w_ukv[l].reshape(KV_LORA, H_MLA, MLA_NOPE + MLA_V)
    wk = jnp.pad(wkv[:, :, :MLA_NOPE], ((0, 0), (0, 0), (0, LANES - MLA_NOPE))).reshape(KV_LORA, H_MLA * LANES)
    wv = wkv[:, :, MLA_NOPE:].reshape(KV_LORA, H_MLA * MLA_V)
    wa, wb = w_ffn_in[l][:, :D_FF], w_ffn_in[l][:, D_FF:]
    cast = lambda a: a.astype(BF16)
    return dict(ret=cast(w_ret), diff=cast(w_diff), mla=cast(w_mla), gate=cast(w_gate), wq=cast(wq), wk=cast(wk),
                wv=cast(wv), wbr=cast(w_branch[l]), wo=cast(w_out[l]), wa=cast(wa), wb=cast(wb),
                wf=cast(w_ffn_out[l]))


def _mixer_layer(x, b, s, l, lw, mod, mod_row, norm1_g, norm2_g, tabs, mla_q_norm, mla_kv_norm, dl, lam_init,
                 s0f, s0b, caches, ropes, final_g):
    tk = min(KEY_TILE, s)
    rope64, rope_mla = ropes if ropes is not None else (None, None)
    proj = functools.partial(_norm_matmul, x, 0, D_MODEL, norm1_g, mod=mod, mod_row=mod_row, shift_row=0, scale_row=1)
    seg_ret = proj(lw["ret"], tn=1536, name="proj_ret")
    seg_diff = proj(lw["diff"], tn=1536, name="proj_diff")
    seg_mla = proj(lw["mla"], tn=768, name="proj_mla")
    gates = proj(lw["gate"], tn=1536, name="proj_gate")

    y_ret, fin_f, fin_b = _retention(seg_ret, b, s, s0f, s0b, tabs, rope64)

    cache_k, cache_v, cache_ckv, cache_kpe = caches if caches is not None else (None,) * 4
    kt, vt = _diff_prep(seg_diff, b, s, tk, l, cache_k, cache_v, rope64)
    y_diff = _diff_attention(seg_diff, b, s, kt, vt, dl, lam_init, rope64)

    q_mla = _norm_matmul(seg_mla, 1, Q_LORA, mla_q_norm, lw["wq"], tn=H_MLA * LANES, name="mla_q")
    kt_m, vt_m, ckv_n = _mla_prep(seg_mla, b, s, tk, l, mla_kv_norm, lw["wk"], lw["wv"], cache_ckv, cache_kpe,
                                  rope_mla)
    y_mla = _mla_attention(q_mla, b, s, kt_m, vt_m, rope_mla)

    x = _merge(x, y_ret, y_diff, y_mla, gates, mod, mod_row, lw["wbr"], lw["wo"])
    x = _ffn(x, norm2_g, mod, mod_row, lw["wa"], lw["wb"], lw["wf"], final_g)
    return x, (fin_f, fin_b, seg_diff, seg_mla, ckv_n)


def kernel(x_prompt, x_sample, state_ret_fwd, state_ret_bwd, cache_diff_k, cache_diff_v, cache_mla_ckv,
           cache_mla_kpe, c, c_ctx, norm1_g, norm2_g, w_ada, b_ada, w_in, ret_decay_fwd, ret_decay_bwd,
           diff_lambda, mla_q_norm, mla_kv_norm, w_uq, w_ukv, w_branch, w_out, w_ffn_in, w_ffn_out, final_g):
    bp, sp, d = x_prompt.shape
    bs, ss, _ = x_sample.shape
    depth = w_in.shape[0]
    past = cache_diff_k.shape[2]

    n_rows = -(-(1 + bs) // 8) * 8
    cond = jnp.zeros((n_rows, d), F32).at[0].set(c_ctx).at[1:1 + bs].set(c)
    mod_all = _modulation(cond, w_ada.astype(BF16), b_ada)
    mod_all = jnp.pad(mod_all.reshape(depth, n_rows, 6, d), ((0, 0), (0, 0), (0, 2), (0, 0)))

    tile_p = min(TOKEN_TILE, bp * sp)
    tile_s = min(TOKEN_TILE, ss)
    assert ss % tile_s == 0 and (bp * sp) % tile_p == 0
    row_prompt = lambda i: 0
    row_sample = lambda i: 1 + (i * tile_s) // ss

    ropes = (_rope_tables_head64(ss), _rope_tables_mla(ss))
    cache_k = cache_diff_k.reshape(bs, depth, past, H_DIFF * 2 * DIFF_HD)
    cache_v = cache_diff_v.reshape(bs, depth, past, H_DIFF * 2 * DIFF_HD)
    cache_kpe = jnp.pad(cache_mla_kpe, ((0, 0), (0, 0), (0, 0), (MLA_NOPE, LANES - MLA_NOPE - MLA_ROPE)))
    zero_state = jnp.zeros((bp, H_RET, RET_DK, RET_DV), F32)

    xp = x_prompt.reshape(bp * sp, d)
    xs = x_sample.reshape(bs * ss, d)
    ret_f, ret_b, dks, dvs, ckvs, kpes = [], [], [], [], [], []
    for l in range(depth):
        lw = _layer_weights(l, w_in, w_uq, w_ukv, w_branch, w_out, w_ffn_in, w_ffn_out)
        lam_init = 0.8 - 0.6 * math.exp(-0.3 * l)
        tabs = _retention_tables(ret_decay_fwd[l], ret_decay_bwd[l])
        dl = jnp.pad(diff_lambda[l], ((0, 4), (0, LANES - DIFF_HD)))
        fg = final_g if l == depth - 1 else None
        common = (norm1_g[l], norm2_g[l], tabs, mla_q_norm[l], mla_kv_norm[l], dl, lam_init)

        xp, ctx = _mixer_layer(xp, bp, sp, l, lw, mod_all[l], row_prompt, *common, zero_state, zero_state,
                               None, None, fg)
        fin_f, fin_b, seg_diff, seg_mla, ckv_n = ctx
        ret_f.append(fin_f)
        ret_b.append(fin_b)
        dks.append(seg_diff[:, 512:1024].reshape(bp, sp, H_DIFF, 2 * DIFF_HD))
        dvs.append(seg_diff[:, 1024:1536].reshape(bp, sp, H_DIFF, 2 * DIFF_HD))
        ckvs.append(ckv_n.reshape(bp, sp, KV_LORA))
        kpes.append(seg_mla[:, 320:352].reshape(bp, sp, MLA_ROPE))

        xs, _ = _mixer_layer(xs, bs, ss, l, lw, mod_all[l], row_sample, *common, state_ret_fwd[:, l],
                             state_ret_bwd[:, l], (cache_k, cache_v, cache_mla_ckv, cache_kpe), ropes, fg)

    stack = lambda parts: jnp.stack(parts, axis=1)
    return (xp.reshape(bp, sp, d), xs.reshape(bs, ss, d), stack(ret_f), stack(ret_b), stack(dks), stack(dvs),
            stack(ckvs), stack(kpes))
```

```python
import functools
import math

import jax
import jax.numpy as jnp
from jax import lax
from jax.experimental import pallas as pl
from jax.experimental.pallas import tpu as pltpu

F32 = jnp.float32
BF16 = jnp.bfloat16

D_MODEL = 1024
GRID_W = 64
H_RET, RET_DK, RET_DV, RET_CHUNK = 4, 64, 128, 128
H_DIFF, DIFF_HD = 4, 64
H_MLA, MLA_NOPE, MLA_ROPE, MLA_V = 8, 64, 32, 64
Q_LORA, KV_LORA = 384, 256
D_FF = 2816
ROPE_BASE = 10000.0
EPS = 1e-6
MLA_SCALE = (MLA_NOPE + MLA_ROPE) ** -0.5
LANES = 128
NEG_BIG = -1e30
LOG2_E = math.log2(math.e)

TOKEN_TILE = 512
FFN_CHUNK = 1408
DIFF_Q_TILE = 512
DIFF_CHAIN_Q = 256
MLA_Q_TILE = 512
KEY_TILE = 512
RET_GROUP = 8
SUM_ROWS = 16
PROJ_TILE_N = 2304

N_PROJ = 6912
COL_GATE = 0
COL_RET = 3072
COL_DIFF = 4608
COL_MLA = 6144
VMEM_LIMIT = 56 * 1024 * 1024


def _cparams(*sem):
    return pltpu.CompilerParams(dimension_semantics=sem, vmem_limit_bytes=VMEM_LIMIT)


def _mod_row(modsel, tm):
    base, tokens_per_row = modsel
    assert tokens_per_row % tm == 0
    return lambda i: base + (i * tm) // tokens_per_row


def _dot(a, b):
    return jnp.dot(a, b, preferred_element_type=F32)


def _dot_nt(a, b):
    return lax.dot_general(a, b, (((1,), (1,)), ((), ())), preferred_element_type=F32)


def _rms(x):
    return x * lax.rsqrt(jnp.mean(x * x, axis=-1, keepdims=True) + EPS)


def _silu(x):
    return x * (1.0 / (1.0 + jnp.exp(-x)))


def _sigmoid(x):
    return 1.0 / (1.0 + jnp.exp(-x))


def _rope(x, c_ref, sa_ref, sb_ref, shift_a, shift_b, rows=slice(None)):
    return (x * c_ref[rows, :] + pltpu.roll(x, shift_a, 1) * sa_ref[rows, :]
            + pltpu.roll(x, shift_b, 1) * sb_ref[rows, :])


def _mod_body(c_ref, w_ref, b_ref, o_ref):
    o_ref[0] = _dot(_silu(c_ref[...]).astype(BF16), w_ref[0]) + b_ref[0]


def _modulation(cond, w_ada, b_ada):
    depth, d, n = w_ada.shape
    rows = cond.shape[0]
    tn = 1536
    return pl.pallas_call(
        _mod_body,
        out_shape=jax.ShapeDtypeStruct((depth, rows, n), F32),
        grid=(depth, n // tn),
        in_specs=[pl.BlockSpec((rows, d), lambda l, j: (0, 0)),
                  pl.BlockSpec((1, d, tn), lambda l, j: (l, 0, j)),
                  pl.BlockSpec((1, 1, tn), lambda l, j: (l, 0, j))],
        out_specs=pl.BlockSpec((1, rows, tn), lambda l, j: (l, 0, j)),
        compiler_params=_cparams("parallel", "parallel"),
        name="adaln_mod",
    )(cond, w_ada, b_ada.reshape(depth, 1, n))


def _norm_matmul_body(*refs, has_mod):
    if has_mod:
        x_ref, g_ref, mod_ref, w_ref, o_ref, h_ref = refs
    else:
        x_ref, g_ref, w_ref, o_ref, h_ref = refs

    @pl.when(pl.program_id(1) == 0)
    def _():
        h = _rms(x_ref[...].astype(F32)) * g_ref[...]
        if has_mod:
            m = mod_ref[0]
            h = h * (1.0 + m[1:2, :]) + m[0:1, :]
        h_ref[...] = h.astype(BF16)

    o_ref[...] = _dot(h_ref[...], w_ref[...]).astype(o_ref.dtype)


def _norm_matmul(x, x_col, k, g, w, *, mod=None, mod_row=None, tm, tn, out_dtype, name):
    t = x.shape[0]
    n = w.shape[1]
    assert t % tm == 0 and n % tn == 0
    in_specs = [pl.BlockSpec((tm, k), lambda i, j: (i, x_col)),
                pl.BlockSpec((1, k), lambda i, j: (0, 0))]
    args = [x, g.reshape(1, k)]
    if mod is not None:
        in_specs.append(pl.BlockSpec((1, 8, k), lambda i, j: (mod_row(i), 0, 0)))
        args.append(mod)
    in_specs.append(pl.BlockSpec((k, tn), lambda i, j: (0, j)))
    args.append(w)
    return pl.pallas_call(
        functools.partial(_norm_matmul_body, has_mod=mod is not None),
        out_shape=jax.ShapeDtypeStruct((t, n), out_dtype),
        grid=(t // tm, n // tn),
        in_specs=in_specs,
        out_specs=pl.BlockSpec((tm, tn), lambda i, j: (i, j)),
        scratch_shapes=[pltpu.VMEM((tm, k), BF16)],
        compiler_params=_cparams("parallel", "arbitrary"),
        name=name,
    )(*args)


def _ret_scan_body(*refs, use_rope, n_steps, group):
    if use_rope:
        (gc_ref, kf_ref, vf_ref, kb_ref, vb_ref, zf_ref, zb_ref, s0f_ref, s0b_ref,
         cf_ref, saf_ref, sbf_ref, cb_ref, sab_ref, sbb_ref,
         sf_ref, sb_ref, finf_ref, finb_ref, st_ref) = refs
        rope_f, rope_b = (cf_ref, saf_ref, sbf_ref), (cb_ref, sab_ref, sbb_ref)
    else:
        (gc_ref, kf_ref, vf_ref, kb_ref, vb_ref, zf_ref, zb_ref, s0f_ref, s0b_ref,
         sf_ref, sb_ref, finf_ref, finb_ref, st_ref) = refs
        rope_f = rope_b = None
    p = pl.program_id(1)
    t = pl.program_id(2)
    c = RET_CHUNK

    @pl.when(t == 0)
    def _():
        st_ref[0] = s0f_ref[0]
        st_ref[1] = s0b_ref[0]

    def update(d, gi, k_ref, v_ref, z_ref, rope_refs, out_ref):
        rows = slice(gi * c, (gi + 1) * c)
        out_ref[0, :, gi] = st_ref[d]
        k = k_ref[rows, :].astype(F32)
        if use_rope:
            k = _rope(k, *rope_refs, 96, 32, rows)
        kz_t =(k * (RET_DK ** -0.5) * z_ref[0]).T.astype(BF16)
        v = v_ref[rows, :].astype(BF16)
        for j in range(2):
            loc = _dot(kz_t[RET_DK * j:RET_DK * (j + 1)], v[:, RET_DV * j:RET_DV * (j + 1)])
            st_ref[d, j] = gc_ref[2 * p + j, d] * st_ref[d, j] + loc

    for gi in range(group):
        update(0, gi, kf_ref, vf_ref, zf_ref, rope_f, sf_ref)
    for gi in reversed(range(group)):
        update(1, gi, kb_ref, vb_ref, zb_ref, rope_b, sb_ref)

    @pl.when(t == n_steps - 1)
    def _():
        finf_ref[0] = st_ref[0]
        finb_ref[0] = st_ref[1]


def _ret_out_body(*refs, use_rope, group):
    if use_rope:
        (q_ref, k_ref, v_ref, g_ref, sf_ref, sb_ref, dm_ref, xif_ref, xib_ref,
         c_ref, sa_ref, sbt_ref, y_ref) = refs
    else:
        (q_ref, k_ref, v_ref, g_ref, sf_ref, sb_ref, dm_ref, xif_ref, xib_ref, y_ref) = refs
    c = RET_CHUNK
    lane = lax.broadcasted_iota(jnp.int32, (c, 2 * RET_DK), 1)
    for gi in range(group):
        rows = slice(gi * c, (gi + 1) * c)
        q = q_ref[rows, :].astype(F32)
        k = k_ref[rows, :].astype(F32)
        if use_rope:
            q = _rope(q, c_ref, sa_ref, sbt_ref, 96, 32, rows)
            k = _rope(k, c_ref, sa_ref, sbt_ref, 96, 32, rows)
        kb = (k * (RET_DK ** -0.5)).astype(BF16)
        v = v_ref[rows, :].astype(BF16)
        g = g_ref[rows, :].astype(F32)
        s_f = sf_ref[0, :, gi].reshape(2 * RET_DK, RET_DV).astype(BF16)
        s_b = sb_ref[0, :, gi].reshape(2 * RET_DK, RET_DV).astype(BF16)
        for j in range(2):
            in_head = (lane >= RET_DK * j) & (lane < RET_DK * (j + 1))
            qm = jnp.where(in_head, q, 0.0).astype(BF16)
            sc = _dot_nt(qm, kb) * dm_ref[j]
            o = _dot(sc.astype(BF16), v[:, RET_DV * j:RET_DV * (j + 1)])
            o = o + _dot(qm, s_f) * xif_ref[j] + _dot(qm, s_b) * xib_ref[j]
            mu = jnp.mean(o, axis=-1, keepdims=True)
            oc = o - mu
            on = oc * lax.rsqrt(jnp.mean(oc * oc, axis=-1, keepdims=True) + EPS)
            gj = g[:, RET_DV * j:RET_DV * (j + 1)]
            y_ref[rows, RET_DV * j:RET_DV * (j + 1)] = (_silu(gj) * on).astype(y_ref.dtype)


def _retention(seg, b, s, s0f, s0b, tabs, rope):
    c = RET_CHUNK
    n = s // c
    group = math.gcd(n, RET_GROUP)
    ns = n // group
    rows = group * c
    use_rope = rope is not None
    gc, zf, zb, dmat, xif, xib = tabs
    rq0, rk0 = COL_RET // 128, (COL_RET + 256) // 128
    rv0, rg0 = (COL_RET + 512) // 256, (COL_RET + 1024) // 256
    state_shape = jax.ShapeDtypeStruct((b, H_RET, n, RET_DK, RET_DV), F32)
    fin_shape = jax.ShapeDtypeStruct((b, H_RET, RET_DK, RET_DV), F32)

    def fwd(bi, t):
        return bi * ns + t

    def bwd(bi, t):
        return bi * ns + (ns - 1 - t)

    in_specs = [
        pl.BlockSpec(memory_space=pltpu.SMEM),
        pl.BlockSpec((rows, 128), lambda bi, p, t: (fwd(bi, t), rk0 + p)),
        pl.BlockSpec((rows, 256), lambda bi, p, t: (fwd(bi, t), rv0 + p)),
        pl.BlockSpec((rows, 128), lambda bi, p, t: (bwd(bi, t), rk0 + p)),
        pl.BlockSpec((rows, 256), lambda bi, p, t: (bwd(bi, t), rv0 + p)),
        pl.BlockSpec((1, c, 128), lambda bi, p, t: (p, 0, 0)),
        pl.BlockSpec((1, c, 128), lambda bi, p, t: (p, 0, 0)),
        pl.BlockSpec((1, 2, RET_DK, RET_DV), lambda bi, p, t: (bi, p, 0, 0)),
        pl.BlockSpec((1, 2, RET_DK, RET_DV), lambda bi, p, t: (bi, p, 0, 0)),
    ]
    args = [gc, seg, seg, seg, seg, zf, zb, s0f, s0b]
    if use_rope:
        in_specs += [pl.BlockSpec((rows, 128), lambda bi, p, t: (t, 0))] * 3
        in_specs += [pl.BlockSpec((rows, 128), lambda bi, p, t: (ns - 1 - t, 0))] * 3
        args += list(rope) + list(rope)
    sf, sb, fin_f, fin_b = pl.pallas_call(
        functools.partial(_ret_scan_body, use_rope=use_rope, n_steps=ns, group=group),
        out_shape=(state_shape, state_shape, fin_shape, fin_shape),
        grid=(b, 2, ns),
        in_specs=in_specs,
        out_specs=(pl.BlockSpec((1, 2, group, RET_DK, RET_DV), lambda bi, p, t: (bi, p, t, 0, 0)),
                   pl.BlockSpec((1, 2, group, RET_DK, RET_DV), lambda bi, p, t: (bi, p, ns - 1 - t, 0, 0)),
                   pl.BlockSpec((1, 2, RET_DK, RET_DV), lambda bi, p, t: (bi, p, 0, 0)),
                   pl.BlockSpec((1, 2, RET_DK, RET_DV), lambda bi, p, t: (bi, p, 0, 0))),
        scratch_shapes=[pltpu.VMEM((2, 2, RET_DK, RET_DV), F32)],
        compiler_params=_cparams("parallel", "parallel", "arbitrary"),
        name="ret_scan",
    )(*args)

    in_specs = [
        pl.BlockSpec((rows, 128), lambda bi, p, t: (fwd(bi, t), rq0 + p)),
        pl.BlockSpec((rows, 128), lambda bi, p, t: (fwd(bi, t), rk0 + p)),
        pl.BlockSpec((rows, 256), lambda bi, p, t: (fwd(bi, t), rv0 + p)),
        pl.BlockSpec((rows, 256), lambda bi, p, t: (fwd(bi, t), rg0 + p)),
        pl.BlockSpec((1, 2, group, RET_DK, RET_DV), lambda bi, p, t: (bi, p, t, 0, 0)),
        pl.BlockSpec((1, 2, group, RET_DK, RET_DV), lambda bi, p, t: (bi, p, t, 0, 0)),
        pl.BlockSpec((2, c, c), lambda bi, p, t: (p, 0, 0)),
        pl.BlockSpec((2, c, 128), lambda bi, p, t: (p, 0, 0)),
        pl.BlockSpec((2, c, 128), lambda bi, p, t: (p, 0, 0)),
    ]
    args = [seg, seg, seg, seg, sf, sb, dmat, xif, xib]
    if use_rope:
        in_specs += [pl.BlockSpec((rows, 128), lambda bi, p, t: (t, 0))] * 3
        args += list(rope)
    y = pl.pallas_call(
        functools.partial(_ret_out_body, use_rope=use_rope, group=group),
        out_shape=jax.ShapeDtypeStruct((b * s, H_RET * RET_DV), BF16),
        grid=(b, 2, ns),
        in_specs=in_specs,
        out_specs=pl.BlockSpec((rows, 256), lambda bi, p, t: (fwd(bi, t), p)),
        compiler_params=_cparams("parallel", "parallel", "parallel"),
        name="ret_out",
    )(*args)
    return y, fin_f, fin_b


def _retention_tables(decay_f, decay_b):
    c = RET_CHUNK
    lg_f = jax.nn.log_sigmoid(decay_f.astype(F32))
    lg_b = jax.nn.log_sigmoid(decay_b.astype(F32))
    i = jnp.arange(c, dtype=F32)
    dist = i[:, None] - i[None, :]
    d_f = jnp.where(dist >= 0, jnp.exp(jnp.maximum(dist, 0.0)[None] * lg_f[:, None, None]), 0.0)
    d_b = jnp.where(dist < 0, jnp.exp(jnp.maximum(-dist, 0.0)[None] * lg_b[:, None, None]), 0.0)
    dmat = d_f + d_b
    xi_f = jnp.exp((i + 1.0)[None, :] * lg_f[:, None])
    xi_b = jnp.exp((c - i)[None, :] * lg_b[:, None])
    zeta_f = jnp.exp((c - 1.0 - i)[None, :] * lg_f[:, None])
    zeta_b = jnp.exp(i[None, :] * lg_b[:, None])
    gc = jnp.stack([jnp.exp(c * lg_f), jnp.exp(c * lg_b)], axis=1)

    def lanes(tab, width):
        return jnp.broadcast_to(tab[:, :, None], tab.shape + (width,))

    def pair(tab):
        t64 = lanes(tab, RET_DK).reshape(H_RET // 2, 2, c, RET_DK)
        return jnp.concatenate([t64[:, 0], t64[:, 1]], axis=-1)

    return gc, pair(zeta_f), pair(zeta_b), dmat, lanes(xi_f, RET_DV), lanes(xi_b, RET_DV)


def _diff_prep_body(*refs, n_cache, use_rope):
    refs = list(refs)
    if n_cache:
        ck_ref, cv_ref = refs[:2]
        refs = refs[2:]
    k_ref, v_ref = refs[:2]
    refs = refs[2:]
    if use_rope:
        rope_refs = refs[:3]
        refs = refs[3:]
    kt_ref, vt_ref = refs
    c = pl.program_id(2)
    dv = 2 * DIFF_HD
    ones = jnp.ones((SUM_ROWS, vt_ref.shape[-1]), BF16)

    def from_new():
        k = k_ref[...].astype(F32)
        if use_rope:
            k = _rope(k, *rope_refs, 96, 32)
        kt_ref[0, 0, 0] = (k * (DIFF_HD ** -0.5)).astype(BF16)
        vt_ref[0, 0, 0, :dv, :] = v_ref[...].astype(F32).T.astype(BF16)
        vt_ref[0, 0, 0, dv:, :] = ones

    def from_cache():
        kt_ref[0, 0, 0] = (ck_ref[0, 0] * (DIFF_HD ** -0.5)).astype(BF16)
        vt_ref[0, 0, 0, :dv, :] = cv_ref[0, 0].T.astype(BF16)
        vt_ref[0, 0, 0, dv:, :] = ones

    if n_cache:
        pl.when(c < n_cache)(from_cache)
        pl.when(c >= n_cache)(from_new)
    else:
        from_new()


def _diff_prep(seg, b, s, tk, layer, cache_k, cache_v, rope):
    n_new = s // tk
    n_cache = 0 if cache_k is None else cache_k.shape[2] // tk
    nk = n_cache + n_new
    use_rope = rope is not None

    def new_row(bi, c):
        return bi * n_new + jnp.maximum(c - n_cache, 0)

    in_specs, args = [], []
    if n_cache:
        spec = pl.BlockSpec((1, 1, tk, 128), lambda bi, h, c: (bi, layer, jnp.minimum(c, n_cache - 1), h))
        in_specs += [spec, spec]
        args += [cache_k, cache_v]
    k0, v0 = (COL_DIFF + 512) // 128, (COL_DIFF + 1024) // 128
    in_specs += [pl.BlockSpec((tk, 128), lambda bi, h, c: (new_row(bi, c), k0 + h)),
                 pl.BlockSpec((tk, 128), lambda bi, h, c: (new_row(bi, c), v0 + h))]
    args += [seg, seg]
    if use_rope:
        in_specs += [pl.BlockSpec((tk, 128), lambda bi, h, c: (jnp.maximum(c - n_cache, 0), 0))] * 3
        args += list(rope)
    return pl.pallas_call(
        functools.partial(_diff_prep_body, n_cache=n_cache, use_rope=use_rope),
        out_shape=(jax.ShapeDtypeStruct((b, H_DIFF, nk, tk, 128), BF16),
                   jax.ShapeDtypeStruct((b, H_DIFF, nk, 128 + SUM_ROWS, tk), BF16)),
        grid=(b, H_DIFF, nk),
        in_specs=in_specs,
        out_specs=(pl.BlockSpec((1, 1, 1, tk, 128), lambda bi, h, c: (bi, h, c, 0, 0)),
                   pl.BlockSpec((1, 1, 1, 128 + SUM_ROWS, tk), lambda bi, h, c: (bi, h, c, 0, 0))),
        compiler_params=_cparams("parallel", "parallel", "arbitrary"),
        name="diff_prep",
    )(*args)


def _mla_prep_body(*refs, n_cache, use_rope, emit_ckv):
    refs = list(refs)
    if n_cache:
        cc_ref, cp_ref = refs[:2]
        refs = refs[2:]
    ckv_ref, kpe_ref, g_ref, wk_ref, wv_ref = refs[:5]
    refs = refs[5:]
    if use_rope:
        rope_refs = refs[:3]
        refs = refs[3:]
    kt_ref, vt_ref = refs[:2]
    c = pl.program_id(1)

    def emit(cn, kp):
        cb = cn.astype(BF16)
        kfull = _dot(cb, wk_ref[...])
        vt = _dot(cb, wv_ref[...]).T
        ones = jnp.ones((SUM_ROWS, vt.shape[-1]), BF16)
        for h in range(H_MLA):
            kt_ref[0, h, 0] = (kfull[:, LANES * h:LANES * (h + 1)] + kp).astype(BF16)
            vt_ref[0, h, 0, :MLA_V, :] = vt[MLA_V * h:MLA_V * (h + 1)].astype(BF16)
            vt_ref[0, h, 0, MLA_V:, :] = ones

    def from_new():
        cn = _rms(ckv_ref[...].astype(F32)) * g_ref[...]
        kp = kpe_ref[...].astype(F32)
        if use_rope:
            kp = _rope(kp, *rope_refs, 112, 16)
        if emit_ckv:
            refs[2][...] = cn
        emit(cn, kp)

    def from_cache():
        emit(cc_ref[0, 0], cp_ref[0, 0])

    if n_cache:
        pl.when(c < n_cache)(from_cache)
        pl.when(c >= n_cache)(from_new)
    else:
        from_new()


def _mla_prep(seg, b, s, tk, layer, g, wk, wv, cache_ckv, cache_kpe, rope):
    n_new = s // tk
    n_cache = 0 if cache_ckv is None else cache_ckv.shape[2] // tk
    nk = n_cache + n_new
    use_rope = rope is not None
    emit_ckv = n_cache == 0

    def new_row(bi, c):
        return bi * n_new + jnp.maximum(c - n_cache, 0)

    in_specs, args = [], []
    if n_cache:
        in_specs += [pl.BlockSpec((1, 1, tk, KV_LORA), lambda bi, c: (bi, layer, jnp.minimum(c, n_cache - 1), 0)),
                     pl.BlockSpec((1, 1, tk, LANES), lambda bi, c: (bi, layer, jnp.minimum(c, n_cache - 1), 0))]
        args += [cache_ckv, cache_kpe]
    in_specs += [pl.BlockSpec((tk, KV_LORA), lambda bi, c: (new_row(bi, c), COL_MLA // KV_LORA)),
                 pl.BlockSpec((tk, LANES), lambda bi, c: (new_row(bi, c), (COL_MLA + KV_LORA) // LANES)),
                 pl.BlockSpec((1, KV_LORA), lambda bi, c: (0, 0)),
                 pl.BlockSpec(wk.shape, lambda bi, c: (0, 0)),
                 pl.BlockSpec(wv.shape, lambda bi, c: (0, 0))]
    args += [seg, seg, g.reshape(1, KV_LORA), wk, wv]
    if use_rope:
        in_specs += [pl.BlockSpec((tk, 128), lambda bi, c: (jnp.maximum(c - n_cache, 0), 0))] * 3
        args += list(rope)
    out_shape = [jax.ShapeDtypeStruct((b, H_MLA, nk, tk, LANES), BF16),
                 jax.ShapeDtypeStruct((b, H_MLA, nk, MLA_V + SUM_ROWS, tk), BF16)]
    out_specs = [pl.BlockSpec((1, H_MLA, 1, tk, LANES), lambda bi, c: (bi, 0, c, 0, 0)),
                 pl.BlockSpec((1, H_MLA, 1, MLA_V + SUM_ROWS, tk), lambda bi, c: (bi, 0, c, 0, 0))]
    if emit_ckv:
        out_shape.append(jax.ShapeDtypeStruct((b * s, KV_LORA), F32))
        out_specs.append(pl.BlockSpec((tk, KV_LORA), lambda bi, c: (new_row(bi, c), 0)))
    outs = pl.pallas_call(
        functools.partial(_mla_prep_body, n_cache=n_cache, use_rope=use_rope, emit_ckv=emit_ckv),
        out_shape=tuple(out_shape),
        grid=(b, nk),
        in_specs=in_specs,
        out_specs=tuple(out_specs),
        compiler_params=_cparams("parallel", "arbitrary"),
        name="mla_prep",
    )(*args)
    return outs if emit_ckv else (outs[0], outs[1], None)


def _flash_keymajor(chains, nk, dv):
    m, acc, s_next = [], [], []
    for q_t, k_tile, _ in chains:
        n = q_t.shape[1]
        m.append(jnp.full((1, n), NEG_BIG, F32))
        acc.append(jnp.zeros((dv + SUM_ROWS, n), F32))
        s_next.append(_dot(k_tile(0), q_t))
    for i in range(nk):
        for ci, (q_t, k_tile, vt_tile) in enumerate(chains):
            s = s_next[ci]
            if i + 1 < nk:
                s_next[ci] = _dot(k_tile(i + 1), q_t)
            m_new = jnp.maximum(m[ci], jnp.max(s, axis=0, keepdims=True))
            alpha = jnp.exp2(m[ci] - m_new)
            p = jnp.exp2(s - m_new)
            acc[ci] = alpha * acc[ci] + _dot(vt_tile(i), p.astype(BF16))
            m[ci] = m_new
    return [(a[dv:dv + 1], a[:dv]) for a in acc]


def _diff_attn_body(*refs, use_rope, nk, tq, lam_init):
    refs = list(refs)
    q_ref, kt_ref, vt_ref, dl_ref = refs[:4]
    refs = refs[4:]
    if use_rope:
        rope_refs = refs[:3]
        refs = refs[3:]
    (y_ref,) = refs
    cq = min(tq, DIFF_CHAIN_Q)
    lane = lax.broadcasted_iota(jnp.int32, (cq, 2 * DIFF_HD), 1)
    chains = []
    for r in range(0, tq, cq):
        rows = slice(r, r + cq)
        q = q_ref[rows, :].astype(F32)
        if use_rope:
            q = _rope(q, *rope_refs, 96, 32, rows)
        q = q * LOG2_E
        q1 = jnp.where(lane < DIFF_HD, q, 0.0)
        q2 = jnp.where(lane >= DIFF_HD, q, 0.0)
        q_t = jnp.concatenate([q1, q2], axis=0).T.astype(BF16)
        chains.append((q_t, lambda i: kt_ref[0, 0, i], lambda i: vt_ref[0, 0, i]))
    results = _flash_keymajor(chains, nk, 2 * DIFF_HD)
    dl = dl_ref[...]
    lam = (jnp.exp(jnp.sum(dl[0:1] * dl[1:2], axis=-1, keepdims=True))
           - jnp.exp(jnp.sum(dl[2:3] * dl[3:4], axis=-1, keepdims=True)) + lam_init)
    for ci, (l, acc) in enumerate(results):
        o = acc * (1.0 / l)
        o = o[:, :cq] - lam * o[:, cq:]
        o = o * lax.rsqrt(jnp.mean(o * o, axis=0, keepdims=True) + EPS) * (1.0 - lam_init)
        y_ref[ci * cq:(ci + 1) * cq, :] = o.T.astype(y_ref.dtype)


def _diff_attention(seg, b, s, kt, vt, dl, lam_init, rope):
    nk, tk = kt.shape[2], kt.shape[3]
    tq = min(DIFF_Q_TILE, s)
    nq = s // tq
    use_rope = rope is not None
    q0 = COL_DIFF // 128
    in_specs = [pl.BlockSpec((tq, 128), lambda bi, h, qi: (bi * nq + qi, q0 + h)),
                pl.BlockSpec((1, 1, nk, tk, 128), lambda bi, h, qi: (bi, h, 0, 0, 0)),
                pl.BlockSpec((1, 1, nk, 128 + SUM_ROWS, tk), lambda bi, h, qi: (bi, h, 0, 0, 0)),
                pl.BlockSpec((8, 128), lambda bi, h, qi: (0, 0))]
    args = [seg, kt, vt, dl]
    if use_rope:
        in_specs += [pl.BlockSpec((tq, 128), lambda bi, h, qi: (qi, 0))] * 3
        args += list(rope)
    return pl.pallas_call(
        functools.partial(_diff_attn_body, use_rope=use_rope, nk=nk, tq=tq, lam_init=lam_init),
        out_shape=jax.ShapeDtypeStruct((b * s, H_DIFF * 2 * DIFF_HD), BF16),
        grid=(b, H_DIFF, nq),
        in_specs=in_specs,
        out_specs=pl.BlockSpec((tq, 128), lambda bi, h, qi: (bi * nq + qi, h)),
        compiler_params=_cparams("parallel", "parallel", "arbitrary"),
        name="diff_attn",
    )(*args)


def _mla_attn_body(*refs, use_rope, nk):
    refs = list(refs)
    q_ref, kt_ref, vt_ref = refs[:3]
    refs = refs[3:]
    if use_rope:
        rope_refs = refs[:3]
        refs = refs[3:]
    (y_ref,) = refs
    chains = []
    for j in range(2):
        q = q_ref[:, LANES * j:LANES * (j + 1)].astype(F32)
        if use_rope:
            q = _rope(q, *rope_refs, 112, 16)
        q_t = (q * (MLA_SCALE * LOG2_E)).T.astype(BF16)
        chains.append((q_t, lambda i, j=j: kt_ref[0, j, i], lambda i, j=j: vt_ref[0, j, i]))
    outs = [acc * (1.0 / l) for l, acc in _flash_keymajor(chains, nk, MLA_V)]
    y_ref[...] = jnp.concatenate(outs, axis=0).T.astype(y_ref.dtype)


def _mla_attention(q_all, b, s, kt, vt, rope):
    nk, tk = kt.shape[2], kt.shape[3]
    tq = min(MLA_Q_TILE, s)
    nq = s // tq
    use_rope = rope is not None
    in_specs = [pl.BlockSpec((tq, 2 * LANES), lambda bi, p, qi: (bi * nq + qi, p)),
                pl.BlockSpec((1, 2, nk, tk, LANES), lambda bi, p, qi: (bi, p, 0, 0, 0)),
                pl.BlockSpec((1, 2, nk, MLA_V + SUM_ROWS, tk), lambda bi, p, qi: (bi, p, 0, 0, 0))]
    args = [q_all, kt, vt]
    if use_rope:
        in_specs += [pl.BlockSpec((tq, 128), lambda bi, p, qi: (qi, 0))] * 3
        args += list(rope)
    return pl.pallas_call(
        functools.partial(_mla_attn_body, use_rope=use_rope, nk=nk),
        out_shape=jax.ShapeDtypeStruct((b * s, H_MLA * MLA_V), BF16),
        grid=(b, H_MLA // 2, nq),
        in_specs=in_specs,
        out_specs=pl.BlockSpec((tq, 2 * MLA_V), lambda bi, p, qi: (bi * nq + qi, p)),
        compiler_params=_cparams("parallel", "parallel", "arbitrary"),
        name="mla_attn",
    )(*args)


def _merge_body(x_ref, yr_ref, yd_ref, ym_ref, gt_ref, mod_ref, wbr_ref, wo_ref, o_ref):
    merged = None
    for gi, y_ref in enumerate((yr_ref, yd_ref, ym_ref)):
        br = _dot(y_ref[...], wbr_ref[gi])
        term = _sigmoid(gt_ref[:, D_MODEL * gi:D_MODEL * (gi + 1)].astype(F32)) * br
        merged = term if merged is None else merged + term
    out = _dot(merged.astype(BF16), wo_ref[...])
    o_ref[...] = x_ref[...] + mod_ref[0][2:3, :] * out


def _merge(x, yr, yd, ym, seg, mod, modsel, wbr, wo):
    t = x.shape[0]
    tm = min(TOKEN_TILE, t)
    mod_row = _mod_row(modsel, tm)
    row = lambda i: (i, 0)
    return pl.pallas_call(
        _merge_body,
        out_shape=jax.ShapeDtypeStruct((t, D_MODEL), F32),
        grid=(t // tm,),
        in_specs=[pl.BlockSpec((tm, D_MODEL), row),
                  pl.BlockSpec((tm, 512), row), pl.BlockSpec((tm, 512), row), pl.BlockSpec((tm, 512), row),
                  pl.BlockSpec((tm, 3 * D_MODEL), lambda i: (i, COL_GATE // (3 * D_MODEL))),
                  pl.BlockSpec((1, 8, D_MODEL), lambda i: (mod_row(i), 0, 0)),
                  pl.BlockSpec(wbr.shape, lambda i: (0, 0, 0)),
                  pl.BlockSpec(wo.shape, lambda i: (0, 0))],
        out_specs=pl.BlockSpec((tm, D_MODEL), row),
        compiler_params=_cparams("parallel"),
        name="merge",
    )(x, yr, yd, ym, seg, mod, wbr, wo)


def _ffn_body(*refs, n_chunks, final):
    if final:
        x_ref, g_ref, mod_ref, wa_ref, wb_ref, wo_ref, fg_ref, o_ref, h_ref, acc_ref = refs
    else:
        x_ref, g_ref, mod_ref, wa_ref, wb_ref, wo_ref, o_ref, h_ref, acc_ref = refs
    c = pl.program_id(1)
    m = mod_ref[0]

    @pl.when(c == 0)
    def _():
        h = _rms(x_ref[...]) * g_ref[...]
        h_ref[...] = (h * (1.0 + m[4:5, :]) + m[3:4, :]).astype(BF16)
        acc_ref[...] = jnp.zeros(acc_ref.shape, F32)

    h = h_ref[...]
    u = _silu(_dot(h, wa_ref[...])) * _dot(h, wb_ref[...])
    acc_ref[...] += _dot(u.astype(BF16), wo_ref[...])

    @pl.when(c == n_chunks - 1)
    def _():
        xn = x_ref[...] + m[5:6, :] * acc_ref[...]
        if final:
            xn = _rms(xn) * fg_ref[...]
        o_ref[...] = xn


def _ffn(x, g, mod, modsel, wa, wb, wo, final_g):
    t = x.shape[0]
    tm = min(TOKEN_TILE, t)
    mod_row = _mod_row(modsel, tm)
    fc = FFN_CHUNK
    n_chunks = D_FF // fc
    final = final_g is not None
    in_specs = [pl.BlockSpec((tm, D_MODEL), lambda i, c: (i, 0)),
                pl.BlockSpec((1, D_MODEL), lambda i, c: (0, 0)),
                pl.BlockSpec((1, 8, D_MODEL), lambda i, c: (mod_row(i), 0, 0)),
                pl.BlockSpec((D_MODEL, fc), lambda i, c: (0, c)),
                pl.BlockSpec((D_MODEL, fc), lambda i, c: (0, c)),
                pl.BlockSpec((fc, D_MODEL), lambda i, c: (c, 0))]
    args = [x, g.reshape(1, D_MODEL), mod, wa, wb, wo]
    if final:
        in_specs.append(pl.BlockSpec((1, D_MODEL), lambda i, c: (0, 0)))
        args.append(final_g.reshape(1, D_MODEL))
    return pl.pallas_call(
        functools.partial(_ffn_body, n_chunks=n_chunks, final=final),
        out_shape=jax.ShapeDtypeStruct((t, D_MODEL), F32),
        grid=(t // tm, n_chunks),
        in_specs=in_specs,
        out_specs=pl.BlockSpec((tm, D_MODEL), lambda i, c: (i, 0)),
        scratch_shapes=[pltpu.VMEM((tm, D_MODEL), BF16), pltpu.VMEM((tm, D_MODEL), F32)],
        compiler_params=_cparams("parallel", "arbitrary"),
        name="ffn",
    )(*args)


def _axial_angles(n_tokens, rot_dim):
    t = jnp.arange(n_tokens)
    row = (t // GRID_W).astype(F32)
    col = (t % GRID_W).astype(F32)
    nf = rot_dim // 4
    inv = ROPE_BASE ** (-jnp.arange(nf, dtype=F32) / nf)
    ang = jnp.concatenate([row[:, None] * inv, col[:, None] * inv], axis=-1)
    return jnp.cos(ang), jnp.sin(ang)


def _rope_tables_head64(n_tokens):
    cos, sin = _axial_angles(n_tokens, 64)
    zero = jnp.zeros_like(sin)
    c = jnp.tile(jnp.concatenate([cos, cos], axis=-1), (1, 2))
    sa = jnp.tile(jnp.concatenate([-sin, zero], axis=-1), (1, 2))
    sb = jnp.tile(jnp.concatenate([zero, sin], axis=-1), (1, 2))
    return c, sa, sb


def _rope_tables_mla(n_tokens):
    cos, sin = _axial_angles(n_tokens, MLA_ROPE)
    z16 = jnp.zeros_like(sin)
    one64 = jnp.ones((n_tokens, MLA_NOPE), F32)
    z64 = jnp.zeros((n_tokens, MLA_NOPE), F32)
    z32 = jnp.zeros((n_tokens, 32), F32)
    c = jnp.concatenate([one64, cos, cos, z32], axis=-1)
    sa = jnp.concatenate([z64, -sin, z16, z32], axis=-1)
    sb = jnp.concatenate([z64, z16, sin, z32], axis=-1)
    return c, sa, sb


def _layer_weights(l, w_in, w_uq, w_ukv, w_branch, w_out, w_ffn_in, w_ffn_out):
    wi = w_in[l]
    d = wi.shape[0]
    w_ret = wi[:, 0:1536]
    w_diff = wi[:, 1536:3072]
    w_mla = jnp.concatenate([wi[:, 3456:3712], jnp.zeros((d, 64), wi.dtype), wi[:, 3712:3744],
                             jnp.zeros((d, 32), wi.dtype), wi[:, 3072:3456]], axis=1)
    w_gate = wi[:, 3744:]
    w_proj = jnp.concatenate([w_gate, w_ret, w_diff, w_mla], axis=1)
    assert w_proj.shape[1] == N_PROJ
    hd = MLA_NOPE + MLA_ROPE
    wq = jnp.pad(w_uq[l].reshape(Q_LORA, H_MLA, hd), ((0, 0), (0, 0), (0, LANES - hd))).reshape(Q_LORA, H_MLA * LANES)
    wkv = w_ukv[l].reshape(KV_LORA, H_MLA, MLA_NOPE + MLA_V)
    wk = jnp.pad(wkv[:, :, :MLA_NOPE], ((0, 0), (0, 0), (0, LANES - MLA_NOPE))).reshape(KV_LORA, H_MLA * LANES)
    wv = wkv[:, :, MLA_NOPE:].reshape(KV_LORA, H_MLA * MLA_V)
    wa, wb = w_ffn_in[l][:, :D_FF], w_ffn_in[l][:, D_FF:]
    cast = lambda a: a.astype(BF16)
    return dict(proj=cast(w_proj), wq=cast(wq), wk=cast(wk), wv=cast(wv), wbr=cast(w_branch[l]), wo=cast(w_out[l]),
                wa=cast(wa), wb=cast(wb), wf=cast(w_ffn_out[l]))


def _mixer_layer(x, b, s, l, lw, mod, modsel, seg_dtype, norm1_g, norm2_g, tabs, mla_q_norm, mla_kv_norm, dl,
                 lam_init, s0f, s0b, caches, ropes, final_g):
    t = b * s
    tk = min(KEY_TILE, s)
    rope64, rope_mla = ropes if ropes is not None else (None, None)
    tm = min(TOKEN_TILE if seg_dtype == F32 else 2 * TOKEN_TILE, modsel[1])
    seg = _norm_matmul(x, 0, D_MODEL, norm1_g, lw["proj"], mod=mod, mod_row=_mod_row(modsel, tm), tm=tm,
                       tn=PROJ_TILE_N, out_dtype=seg_dtype, name="proj_in")

    y_ret, fin_f, fin_b = _retention(seg, b, s, s0f, s0b, tabs, rope64)

    cache_k, cache_v, cache_ckv, cache_kpe = caches if caches is not None else (None,) * 4
    kt, vt = _diff_prep(seg, b, s, tk, l, cache_k, cache_v, rope64)
    y_diff = _diff_attention(seg, b, s, kt, vt, dl, lam_init, rope64)

    q_mla = _norm_matmul(seg, (COL_MLA + KV_LORA + LANES) // Q_LORA, Q_LORA, mla_q_norm, lw["wq"],
                         tm=min(TOKEN_TILE, t), tn=H_MLA * LANES, out_dtype=F32, name="mla_q")
    kt_m, vt_m, ckv_n = _mla_prep(seg, b, s, tk, l, mla_kv_norm, lw["wk"], lw["wv"], cache_ckv, cache_kpe, rope_mla)
    y_mla = _mla_attention(q_mla, b, s, kt_m, vt_m, rope_mla)

    x = _merge(x, y_ret, y_diff, y_mla, seg, mod, modsel, lw["wbr"], lw["wo"])
    x = _ffn(x, norm2_g, mod, modsel, lw["wa"], lw["wb"], lw["wf"], final_g)
    return x, (fin_f, fin_b, seg, ckv_n)


def kernel(x_prompt, x_sample, state_ret_fwd, state_ret_bwd, cache_diff_k, cache_diff_v, cache_mla_ckv,
           cache_mla_kpe, c, c_ctx, norm1_g, norm2_g, w_ada, b_ada, w_in, ret_decay_fwd, ret_decay_bwd,
           diff_lambda, mla_q_norm, mla_kv_norm, w_uq, w_ukv, w_branch, w_out, w_ffn_in, w_ffn_out, final_g):
    bp, sp, d = x_prompt.shape
    bs, ss, _ = x_sample.shape
    depth = w_in.shape[0]
    past = cache_diff_k.shape[2]

    n_rows = -(-(1 + bs) // 8) * 8
    cond = jnp.zeros((n_rows, d), F32).at[0].set(c_ctx).at[1:1 + bs].set(c)
    mod_all = _modulation(cond, w_ada.astype(BF16), b_ada)
    mod_all = jnp.pad(mod_all.reshape(depth, n_rows, 6, d), ((0, 0), (0, 0), (0, 2), (0, 0)))

    sel_prompt = (0, bp * sp)
    sel_sample = (1, ss)

    ropes = (_rope_tables_head64(ss), _rope_tables_mla(ss))
    cache_k = cache_diff_k.reshape(bs, depth, past, H_DIFF * 2 * DIFF_HD)
    cache_v = cache_diff_v.reshape(bs, depth, past, H_DIFF * 2 * DIFF_HD)
    cache_kpe = jnp.pad(cache_mla_kpe, ((0, 0), (0, 0), (0, 0), (MLA_NOPE, LANES - MLA_NOPE - MLA_ROPE)))
    zero_state = jnp.zeros((bp, H_RET, RET_DK, RET_DV), F32)

    xp = x_prompt.reshape(bp * sp, d)
    xs = x_sample.reshape(bs * ss, d)
    ret_f, ret_b, dks, dvs, ckvs, kpes = [], [], [], [], [], []
    for l in range(depth):
        lw = _layer_weights(l, w_in, w_uq, w_ukv, w_branch, w_out, w_ffn_in, w_ffn_out)
        lam_init = 0.8 - 0.6 * math.exp(-0.3 * l)
        tabs = _retention_tables(ret_decay_fwd[l], ret_decay_bwd[l])
        dl = jnp.pad(diff_lambda[l], ((0, 4), (0, LANES - DIFF_HD)))
        fg = final_g if l == depth - 1 else None
        common = (norm1_g[l], norm2_g[l], tabs, mla_q_norm[l], mla_kv_norm[l], dl, lam_init)

        xp, ctx = _mixer_layer(xp, bp, sp, l, lw, mod_all[l], sel_prompt, F32, *common, zero_state, zero_state,
                               None, None, fg)
        fin_f, fin_b, seg, ckv_n = ctx
        ret_f.append(fin_f)
        ret_b.append(fin_b)
        dks.append(seg[:, COL_DIFF + 512:COL_DIFF + 1024].reshape(bp, sp, H_DIFF, 2 * DIFF_HD))
        dvs.append(seg[:, COL_DIFF + 1024:COL_DIFF + 1536].reshape(bp, sp, H_DIFF, 2 * DIFF_HD))
        ckvs.append(ckv_n.reshape(bp, sp, KV_LORA))
        kpe0 = COL_MLA + KV_LORA + MLA_NOPE
        kpes.append(seg[:, kpe0:kpe0 + MLA_ROPE].reshape(bp, sp, MLA_ROPE))

        xs, _ = _mixer_layer(xs, bs, ss, l, lw, mod_all[l], sel_sample, BF16, *common, state_ret_fwd[:, l],
                             state_ret_bwd[:, l], (cache_k, cache_v, cache_mla_ckv, cache_kpe), ropes, fg)

    stack = lambda parts: jnp.stack(parts, axis=1)
    return (xp.reshape(bp, sp, d), xs.reshape(bs, ss, d), stack(ret_f), stack(ret_b), stack(dks), stack(dvs),
            stack(ckvs), stack(kpes))
```

```python
import functools
import math

import jax
import jax.numpy as jnp
from jax import lax
from jax.experimental import pallas as pl
from jax.experimental.pallas import tpu as pltpu

F32 = jnp.float32
BF16 = jnp.bfloat16

D_MODEL = 1024
GRID_W = 64
H_RET, RET_DK, RET_DV, RET_CHUNK = 4, 64, 128, 128
H_DIFF, DIFF_HD = 4, 64
H_MLA, MLA_NOPE, MLA_ROPE, MLA_V = 8, 64, 32, 64
Q_LORA, KV_LORA = 384, 256
D_FF = 2816
ROPE_BASE = 10000.0
EPS = 1e-6
MLA_SCALE = (MLA_NOPE + MLA_ROPE) ** -0.5
LANES = 128
NEG_BIG = -1e30
LOG2_E = math.log2(math.e)

TOKEN_TILE = 512
FFN_CHUNK = 1408
DIFF_Q_TILE = 1024
DIFF_CHAIN_Q = 256
MLA_Q_TILE = 1024
MLA_CHAIN_Q = 512
KEY_TILE = 512
RET_GROUP = 8
SUM_ROWS = 16
PROJ_TILE_N = 2304

N_PROJ = 6912
COL_GATE = 0
COL_RET = 3072
COL_DIFF = 4608
COL_MLA = 6144
VMEM_LIMIT = 56 * 1024 * 1024


def _cparams(*sem):
    return pltpu.CompilerParams(dimension_semantics=sem, vmem_limit_bytes=VMEM_LIMIT)


def _mod_row(modsel, tm):
    base, tokens_per_row = modsel
    assert tokens_per_row % tm == 0
    return lambda i: base + (i * tm) // tokens_per_row


def _dot(a, b):
    return jnp.dot(a, b, preferred_element_type=F32)


def _dot_nt(a, b):
    return lax.dot_general(a, b, (((1,), (1,)), ((), ())), preferred_element_type=F32)


def _rms(x):
    return x * lax.rsqrt(jnp.mean(x * x, axis=-1, keepdims=True) + EPS)


def _silu(x):
    return x * (1.0 / (1.0 + jnp.exp(-x)))


def _sigmoid(x):
    return 1.0 / (1.0 + jnp.exp(-x))


def _rope(x, c_ref, sa_ref, sb_ref, shift_a, shift_b, rows=slice(None)):
    return (x * c_ref[rows, :] + pltpu.roll(x, shift_a, 1) * sa_ref[rows, :]
            + pltpu.roll(x, shift_b, 1) * sb_ref[rows, :])


def _mod_body(c_ref, w_ref, b_ref, o_ref):
    o_ref[0] = _dot(_silu(c_ref[...]).astype(BF16), w_ref[0]) + b_ref[0]


def _modulation(cond, w_ada, b_ada):
    depth, d, n = w_ada.shape
    rows = cond.shape[0]
    tn = 1536
    return pl.pallas_call(
        _mod_body,
        out_shape=jax.ShapeDtypeStruct((depth, rows, n), F32),
        grid=(depth, n // tn),
        in_specs=[pl.BlockSpec((rows, d), lambda l, j: (0, 0)),
                  pl.BlockSpec((1, d, tn), lambda l, j: (l, 0, j)),
                  pl.BlockSpec((1, 1, tn), lambda l, j: (l, 0, j))],
        out_specs=pl.BlockSpec((1, rows, tn), lambda l, j: (l, 0, j)),
        compiler_params=_cparams("parallel", "parallel"),
        name="adaln_mod",
    )(cond, w_ada, b_ada.reshape(depth, 1, n))


def _norm_matmul_body(*refs, has_mod):
    if has_mod:
        x_ref, g_ref, mod_ref, w_ref, o_ref, h_ref = refs
    else:
        x_ref, g_ref, w_ref, o_ref, h_ref = refs

    @pl.when(pl.program_id(1) == 0)
    def _():
        h = _rms(x_ref[...].astype(F32)) * g_ref[...]
        if has_mod:
            m = mod_ref[0]
            h = h * (1.0 + m[1:2, :]) + m[0:1, :]
        h_ref[...] = h.astype(BF16)

    o_ref[...] = _dot(h_ref[...], w_ref[...]).astype(o_ref.dtype)


def _norm_matmul(x, x_col, k, g, w, *, mod=None, mod_row=None, tm, tn, out_dtype, name):
    t = x.shape[0]
    n = w.shape[1]
    assert t % tm == 0 and n % tn == 0
    in_specs = [pl.BlockSpec((tm, k), lambda i, j: (i, x_col)),
                pl.BlockSpec((1, k), lambda i, j: (0, 0))]
    args = [x, g.reshape(1, k)]
    if mod is not None:
        in_specs.append(pl.BlockSpec((1, 8, k), lambda i, j: (mod_row(i), 0, 0)))
        args.append(mod)
    in_specs.append(pl.BlockSpec((k, tn), lambda i, j: (0, j)))
    args.append(w)
    return pl.pallas_call(
        functools.partial(_norm_matmul_body, has_mod=mod is not None),
        out_shape=jax.ShapeDtypeStruct((t, n), out_dtype),
        grid=(t // tm, n // tn),
        in_specs=in_specs,
        out_specs=pl.BlockSpec((tm, tn), lambda i, j: (i, j)),
        scratch_shapes=[pltpu.VMEM((tm, k), BF16)],
        compiler_params=_cparams("parallel", "arbitrary"),
        name=name,
    )(*args)


def _ret_scan_body(*refs, use_rope, n_steps, group):
    if use_rope:
        (gc_ref, kf_ref, vf_ref, kb_ref, vb_ref, zf_ref, zb_ref, s0f_ref, s0b_ref,
         cf_ref, saf_ref, sbf_ref, cb_ref, sab_ref, sbb_ref,
         sf_ref, sb_ref, finf_ref, finb_ref, st_ref) = refs
        rope_f, rope_b = (cf_ref, saf_ref, sbf_ref), (cb_ref, sab_ref, sbb_ref)
    else:
        (gc_ref, kf_ref, vf_ref, kb_ref, vb_ref, zf_ref, zb_ref, s0f_ref, s0b_ref,
         sf_ref, sb_ref, finf_ref, finb_ref, st_ref) = refs
        rope_f = rope_b = None
    p = pl.program_id(1)
    t = pl.program_id(2)
    c = RET_CHUNK

    @pl.when(t == 0)
    def _():
        st_ref[0] = s0f_ref[0]
        st_ref[1] = s0b_ref[0]

    def local_terms(gi, k_ref, v_ref, z_ref, rope_refs):
        rows = slice(gi * c, (gi + 1) * c)
        k = k_ref[rows, :].astype(F32)
        if use_rope:
            k = _rope(k, *rope_refs, 96, 32, rows)
        kz_t = (k * (RET_DK ** -0.5) * z_ref[0]).T.astype(BF16)
        v = v_ref[rows, :].astype(BF16)
        return [_dot(kz_t[RET_DK * j:RET_DK * (j + 1)], v[:, RET_DV * j:RET_DV * (j + 1)]) for j in range(2)]

    loc_f = [local_terms(gi, kf_ref, vf_ref, zf_ref, rope_f) for gi in range(group)]
    loc_b = [local_terms(gi, kb_ref, vb_ref, zb_ref, rope_b) for gi in range(group)]
    ascending = list(range(group))
    for d, loc, out_ref, order in ((0, loc_f, sf_ref, ascending), (1, loc_b, sb_ref, ascending[::-1])):
        for j in range(2):
            decay = gc_ref[2 * p + j, d]
            st = st_ref[d, j]
            for gi in order:
                out_ref[0, j, gi] = st
                st = decay * st + loc[gi][j]
            st_ref[d, j] = st

    @pl.when(t == n_steps - 1)
    def _():
        finf_ref[0] = st_ref[0]
        finb_ref[0] = st_ref[1]


def _ret_out_body(*refs, use_rope, group):
    if use_rope:
        (q_ref, k_ref, v_ref, g_ref, sf_ref, sb_ref, dm_ref, xif_ref, xib_ref,
         c_ref, sa_ref, sbt_ref, y_ref) = refs
    else:
        (q_ref, k_ref, v_ref, g_ref, sf_ref, sb_ref, dm_ref, xif_ref, xib_ref, y_ref) = refs
    c = RET_CHUNK
    lane = lax.broadcasted_iota(jnp.int32, (c, 2 * RET_DK), 1)
    stage1 = []
    for gi in range(group):
        rows = slice(gi * c, (gi + 1) * c)
        q = q_ref[rows, :].astype(F32)
        k = k_ref[rows, :].astype(F32)
        if use_rope:
            q = _rope(q, c_ref, sa_ref, sbt_ref, 96, 32, rows)
            k = _rope(k, c_ref, sa_ref, sbt_ref, 96, 32, rows)
        kb = (k * (RET_DK ** -0.5)).astype(BF16)
        s_f = sf_ref[0, :, gi].reshape(2 * RET_DK, RET_DV).astype(BF16)
        s_b = sb_ref[0, :, gi].reshape(2 * RET_DK, RET_DV).astype(BF16)
        for j in range(2):
            in_head = (lane >= RET_DK * j) & (lane < RET_DK * (j + 1))
            qm = jnp.where(in_head, q, 0.0).astype(BF16)
            sc = (_dot_nt(qm, kb) * dm_ref[j]).astype(BF16)
            cross = _dot(qm, s_f) * xif_ref[j] + _dot(qm, s_b) * xib_ref[j]
            stage1.append((rows, j, sc, cross))
    for rows, j, sc, cross in stage1:
        cols = slice(RET_DV * j, RET_DV * (j + 1))
        o = _dot(sc, v_ref[rows, cols].astype(BF16)) + cross
        mu = jnp.mean(o, axis=-1, keepdims=True)
        oc = o - mu
        on = oc * lax.rsqrt(jnp.mean(oc * oc, axis=-1, keepdims=True) + EPS)
        y_ref[rows, cols] = (_silu(g_ref[rows, cols].astype(F32)) * on).astype(y_ref.dtype)


def _retention(seg, b, s, s0f, s0b, tabs, rope):
    c = RET_CHUNK
    n = s // c
    group = math.gcd(n, RET_GROUP)
    ns = n // group
    rows = group * c
    use_rope = rope is not None
    gc, zf, zb, dmat, xif, xib = tabs
    rq0, rk0 = COL_RET // 128, (COL_RET + 256) // 128
    rv0, rg0 = (COL_RET + 512) // 256, (COL_RET + 1024) // 256
    state_shape = jax.ShapeDtypeStruct((b, H_RET, n, RET_DK, RET_DV), F32)
    fin_shape = jax.ShapeDtypeStruct((b, H_RET, RET_DK, RET_DV), F32)

    def fwd(bi, t):
        return bi * ns + t

    def bwd(bi, t):
        return bi * ns + (ns - 1 - t)

    in_specs = [
        pl.BlockSpec(memory_space=pltpu.SMEM),
        pl.BlockSpec((rows, 128), lambda bi, p, t: (fwd(bi, t), rk0 + p)),
        pl.BlockSpec((rows, 256), lambda bi, p, t: (fwd(bi, t), rv0 + p)),
        pl.BlockSpec((rows, 128), lambda bi, p, t: (bwd(bi, t), rk0 + p)),
        pl.BlockSpec((rows, 256), lambda bi, p, t: (bwd(bi, t), rv0 + p)),
        pl.BlockSpec((1, c, 128), lambda bi, p, t: (p, 0, 0)),
        pl.BlockSpec((1, c, 128), lambda bi, p, t: (p, 0, 0)),
        pl.BlockSpec((1, 2, RET_DK, RET_DV), lambda bi, p, t: (bi, p, 0, 0)),
        pl.BlockSpec((1, 2, RET_DK, RET_DV), lambda bi, p, t: (bi, p, 0, 0)),
    ]
    args = [gc, seg, seg, seg, seg, zf, zb, s0f, s0b]
    if use_rope:
        in_specs += [pl.BlockSpec((rows, 128), lambda bi, p, t: (t, 0))] * 3
        in_specs += [pl.BlockSpec((rows, 128), lambda bi, p, t: (ns - 1 - t, 0))] * 3
        args += list(rope) + list(rope)
    sf, sb, fin_f, fin_b = pl.pallas_call(
        functools.partial(_ret_scan_body, use_rope=use_rope, n_steps=ns, group=group),
        out_shape=(state_shape, state_shape, fin_shape, fin_shape),
        grid=(b, 2, ns),
        in_specs=in_specs,
        out_specs=(pl.BlockSpec((1, 2, group, RET_DK, RET_DV), lambda bi, p, t: (bi, p, t, 0, 0)),
                   pl.BlockSpec((1, 2, group, RET_DK, RET_DV), lambda bi, p, t: (bi, p, ns - 1 - t, 0, 0)),
                   pl.BlockSpec((1, 2, RET_DK, RET_DV), lambda bi, p, t: (bi, p, 0, 0)),
                   pl.BlockSpec((1, 2, RET_DK, RET_DV), lambda bi, p, t: (bi, p, 0, 0))),
        scratch_shapes=[pltpu.VMEM((2, 2, RET_DK, RET_DV), F32)],
        compiler_params=_cparams("parallel", "parallel", "arbitrary"),
        name="ret_scan",
    )(*args)

    in_specs = [
        pl.BlockSpec((rows, 128), lambda bi, p, t: (fwd(bi, t), rq0 + p)),
        pl.BlockSpec((rows, 128), lambda bi, p, t: (fwd(bi, t), rk0 + p)),
        pl.BlockSpec((rows, 256), lambda bi, p, t: (fwd(bi, t), rv0 + p)),
        pl.BlockSpec((rows, 256), lambda bi, p, t: (fwd(bi, t), rg0 + p)),
        pl.BlockSpec((1, 2, group, RET_DK, RET_DV), lambda bi, p, t: (bi, p, t, 0, 0)),
        pl.BlockSpec((1, 2, group, RET_DK, RET_DV), lambda bi, p, t: (bi, p, t, 0, 0)),
        pl.BlockSpec((2, c, c), lambda bi, p, t: (p, 0, 0)),
        pl.BlockSpec((2, c, 128), lambda bi, p, t: (p, 0, 0)),
        pl.BlockSpec((2, c, 128), lambda bi, p, t: (p, 0, 0)),
    ]
    args = [seg, seg, seg, seg, sf, sb, dmat, xif, xib]
    if use_rope:
        in_specs += [pl.BlockSpec((rows, 128), lambda bi, p, t: (t, 0))] * 3
        args += list(rope)
    y = pl.pallas_call(
        functools.partial(_ret_out_body, use_rope=use_rope, group=group),
        out_shape=jax.ShapeDtypeStruct((b * s, H_RET * RET_DV), BF16),
        grid=(b, 2, ns),
        in_specs=in_specs,
        out_specs=pl.BlockSpec((rows, 256), lambda bi, p, t: (fwd(bi, t), p)),
        compiler_params=_cparams("parallel", "parallel", "parallel"),
        name="ret_out",
    )(*args)
    return y, fin_f, fin_b


def _retention_tables(decay_f, decay_b):
    c = RET_CHUNK
    lg_f = jax.nn.log_sigmoid(decay_f.astype(F32))
    lg_b = jax.nn.log_sigmoid(decay_b.astype(F32))
    i = jnp.arange(c, dtype=F32)
    dist = i[:, None] - i[None, :]
    d_f = jnp.where(dist >= 0, jnp.exp(jnp.maximum(dist, 0.0)[None] * lg_f[:, None, None]), 0.0)
    d_b = jnp.where(dist < 0, jnp.exp(jnp.maximum(-dist, 0.0)[None] * lg_b[:, None, None]), 0.0)
    dmat = d_f + d_b
    xi_f = jnp.exp((i + 1.0)[None, :] * lg_f[:, None])
    xi_b = jnp.exp((c - i)[None, :] * lg_b[:, None])
    zeta_f = jnp.exp((c - 1.0 - i)[None, :] * lg_f[:, None])
    zeta_b = jnp.exp(i[None, :] * lg_b[:, None])
    gc = jnp.stack([jnp.exp(c * lg_f), jnp.exp(c * lg_b)], axis=1)

    def lanes(tab, width):
        return jnp.broadcast_to(tab[:, :, None], tab.shape + (width,))

    def pair(tab):
        t64 = lanes(tab, RET_DK).reshape(H_RET // 2, 2, c, RET_DK)
        return jnp.concatenate([t64[:, 0], t64[:, 1]], axis=-1)

    return gc, pair(zeta_f), pair(zeta_b), dmat, lanes(xi_f, RET_DV), lanes(xi_b, RET_DV)


def _diff_prep_body(*refs, n_cache, use_rope):
    refs = list(refs)
    if n_cache:
        ck_ref, cv_ref = refs[:2]
        refs = refs[2:]
    k_ref, v_ref = refs[:2]
    refs = refs[2:]
    if use_rope:
        rope_refs = refs[:3]
        refs = refs[3:]
    kt_ref, vt_ref = refs
    c = pl.program_id(1)
    dv = 2 * DIFF_HD
    ones = jnp.ones((SUM_ROWS, vt_ref.shape[-1]), BF16)

    def emit(load_k, load_v, rotate):
        for h in range(H_DIFF):
            cols = slice(dv * h, dv * (h + 1))
            k = load_k(cols).astype(F32)
            if rotate:
                k = _rope(k, *rope_refs, 96, 32)
            kt_ref[0, h, 0] = (k * (DIFF_HD ** -0.5)).astype(BF16)
            vt_ref[0, h, 0, :dv, :] = load_v(cols).astype(F32).T.astype(BF16)
            vt_ref[0, h, 0, dv:, :] = ones

    def from_new():
        emit(lambda cols: k_ref[:, cols], lambda cols: v_ref[:, cols], use_rope)

    def from_cache():
        emit(lambda cols: ck_ref[0, 0, :, cols], lambda cols: cv_ref[0, 0, :, cols], False)

    if n_cache:
        pl.when(c < n_cache)(from_cache)
        pl.when(c >= n_cache)(from_new)
    else:
        from_new()


def _diff_prep(seg, b, s, tk, layer, cache_k, cache_v, rope):
    n_new = s // tk
    n_cache = 0 if cache_k is None else cache_k.shape[2] // tk
    nk = n_cache + n_new
    use_rope = rope is not None

    def new_row(bi, c):
        return bi * n_new + jnp.maximum(c - n_cache, 0)

    in_specs, args = [], []
    width = H_DIFF * 2 * DIFF_HD
    if n_cache:
        spec = pl.BlockSpec((1, 1, tk, width), lambda bi, c: (bi, layer, jnp.minimum(c, n_cache - 1), 0))
        in_specs += [spec, spec]
        args += [cache_k, cache_v]
    k0, v0 = (COL_DIFF + width) // width, (COL_DIFF + 2 * width) // width
    in_specs += [pl.BlockSpec((tk, width), lambda bi, c: (new_row(bi, c), k0)),
                 pl.BlockSpec((tk, width), lambda bi, c: (new_row(bi, c), v0))]
    args += [seg, seg]
    if use_rope:
        in_specs += [pl.BlockSpec((tk, 128), lambda bi, c: (jnp.maximum(c - n_cache, 0), 0))] * 3
        args += list(rope)
    return pl.pallas_call(
        functools.partial(_diff_prep_body, n_cache=n_cache, use_rope=use_rope),
        out_shape=(jax.ShapeDtypeStruct((b, H_DIFF, nk, tk, 128), BF16),
                   jax.ShapeDtypeStruct((b, H_DIFF, nk, 128 + SUM_ROWS, tk), BF16)),
        grid=(b, nk),
        in_specs=in_specs,
        out_specs=(pl.BlockSpec((1, H_DIFF, 1, tk, 128), lambda bi, c: (bi, 0, c, 0, 0)),
                   pl.BlockSpec((1, H_DIFF, 1, 128 + SUM_ROWS, tk), lambda bi, c: (bi, 0, c, 0, 0))),
        compiler_params=_cparams("parallel", "arbitrary"),
        name="diff_prep",
    )(*args)


def _mla_prep_body(*refs, n_cache, use_rope, emit_ckv):
    refs = list(refs)
    if n_cache:
        cc_ref, cp_ref = refs[:2]
        refs = refs[2:]
    ckv_ref, kpe_ref, g_ref, wk_ref, wv_ref = refs[:5]
    refs = refs[5:]
    if use_rope:
        rope_refs = refs[:3]
        refs = refs[3:]
    kt_ref, vt_ref = refs[:2]
    c = pl.program_id(1)

    def emit(cn, kp):
        cb = cn.astype(BF16)
        kfull = _dot(cb, wk_ref[...])
        vt = _dot(cb, wv_ref[...]).T
        ones = jnp.ones((SUM_ROWS, vt.shape[-1]), BF16)
        for h in range(H_MLA):
            kt_ref[0, h, 0] = (kfull[:, LANES * h:LANES * (h + 1)] + kp).astype(BF16)
            vt_ref[0, h, 0, :MLA_V, :] = vt[MLA_V * h:MLA_V * (h + 1)].astype(BF16)
            vt_ref[0, h, 0, MLA_V:, :] = ones

    def from_new():
        cn = _rms(ckv_ref[...].astype(F32)) * g_ref[...]
        kp = kpe_ref[...].astype(F32)
        if use_rope:
            kp = _rope(kp, *rope_refs, 112, 16)
        if emit_ckv:
            refs[2][...] = cn
        emit(cn, kp)

    def from_cache():
        emit(cc_ref[0, 0], cp_ref[0, 0])

    if n_cache:
        pl.when(c < n_cache)(from_cache)
        pl.when(c >= n_cache)(from_new)
    else:
        from_new()


def _mla_prep(seg, b, s, tk, layer, g, wk, wv, cache_ckv, cache_kpe, rope):
    n_new = s // tk
    n_cache = 0 if cache_ckv is None else cache_ckv.shape[2] // tk
    nk = n_cache + n_new
    use_rope = rope is not None
    emit_ckv = n_cache == 0

    def new_row(bi, c):
        return bi * n_new + jnp.maximum(c - n_cache, 0)

    in_specs, args = [], []
    if n_cache:
        in_specs += [pl.BlockSpec((1, 1, tk, KV_LORA), lambda bi, c: (bi, layer, jnp.minimum(c, n_cache - 1), 0)),
                     pl.BlockSpec((1, 1, tk, LANES), lambda bi, c: (bi, layer, jnp.minimum(c, n_cache - 1), 0))]
        args += [cache_ckv, cache_kpe]
    in_specs += [pl.BlockSpec((tk, KV_LORA), lambda bi, c: (new_row(bi, c), COL_MLA // KV_LORA)),
                 pl.BlockSpec((tk, LANES), lambda bi, c: (new_row(bi, c), (COL_MLA + KV_LORA) // LANES)),
                 pl.BlockSpec((1, KV_LORA), lambda bi, c: (0, 0)),
                 pl.BlockSpec(wk.shape, lambda bi, c: (0, 0)),
                 pl.BlockSpec(wv.shape, lambda bi, c: (0, 0))]
    args += [seg, seg, g.reshape(1, KV_LORA), wk, wv]
    if use_rope:
        in_specs += [pl.BlockSpec((tk, 128), lambda bi, c: (jnp.maximum(c - n_cache, 0), 0))] * 3
        args += list(rope)
    out_shape = [jax.ShapeDtypeStruct((b, H_MLA, nk, tk, LANES), BF16),
                 jax.ShapeDtypeStruct((b, H_MLA, nk, MLA_V + SUM_ROWS, tk), BF16)]
    out_specs = [pl.BlockSpec((1, H_MLA, 1, tk, LANES), lambda bi, c: (bi, 0, c, 0, 0)),
                 pl.BlockSpec((1, H_MLA, 1, MLA_V + SUM_ROWS, tk), lambda bi, c: (bi, 0, c, 0, 0))]
    if emit_ckv:
        out_shape.append(jax.ShapeDtypeStruct((b * s, KV_LORA), F32))
        out_specs.append(pl.BlockSpec((tk, KV_LORA), lambda bi, c: (new_row(bi, c), 0)))
    outs = pl.pallas_call(
        functools.partial(_mla_prep_body, n_cache=n_cache, use_rope=use_rope, emit_ckv=emit_ckv),
        out_shape=tuple(out_shape),
        grid=(b, nk),
        in_specs=in_specs,
        out_specs=tuple(out_specs),
        compiler_params=_cparams("parallel", "arbitrary"),
        name="mla_prep",
    )(*args)
    return outs if emit_ckv else (outs[0], outs[1], None)


def _flash_keymajor(chains, nk, dv):
    m, acc, s_next = [], [], []
    for q_t, k_tile, _ in chains:
        n = q_t.shape[1]
        m.append(jnp.full((1, n), NEG_BIG, F32))
        acc.append(jnp.zeros((dv + SUM_ROWS, n), F32))
        s_next.append(_dot(k_tile(0), q_t))
    for i in range(nk):
        for ci, (q_t, k_tile, vt_tile) in enumerate(chains):
            s = s_next[ci]
            if i + 1 < nk:
                s_next[ci] = _dot(k_tile(i + 1), q_t)
            m_new = jnp.maximum(m[ci], jnp.max(s, axis=0, keepdims=True))
            alpha = jnp.exp2(m[ci] - m_new)
            p = jnp.exp2(s - m_new)
            acc[ci] = alpha * acc[ci] + _dot(vt_tile(i), p.astype(BF16))
            m[ci] = m_new
    return [(a[dv:dv + 1], a[:dv]) for a in acc]


def _diff_attn_body(*refs, use_rope, nk, tq, lam_init):
    refs = list(refs)
    q_ref, kt_ref, vt_ref, dl_ref = refs[:4]
    refs = refs[4:]
    if use_rope:
        rope_refs = refs[:3]
        refs = refs[3:]
    (y_ref,) = refs
    cq = min(tq, DIFF_CHAIN_Q)
    lane = lax.broadcasted_iota(jnp.int32, (cq, 2 * DIFF_HD), 1)
    chains = []
    for r in range(0, tq, cq):
        rows = slice(r, r + cq)
        q = q_ref[rows, :].astype(F32)
        if use_rope:
            q = _rope(q, *rope_refs, 96, 32, rows)
        q = q * LOG2_E
        q1 = jnp.where(lane < DIFF_HD, q, 0.0)
        q2 = jnp.where(lane >= DIFF_HD, q, 0.0)
        q_t = jnp.concatenate([q1, q2], axis=0).T.astype(BF16)
        chains.append((q_t, lambda i: kt_ref[0, 0, i], lambda i: vt_ref[0, 0, i]))
    results = _flash_keymajor(chains, nk, 2 * DIFF_HD)
    dl = dl_ref[...]
    lam = (jnp.exp(jnp.sum(dl[0:1] * dl[1:2], axis=-1, keepdims=True))
           - jnp.exp(jnp.sum(dl[2:3] * dl[3:4], axis=-1, keepdims=True)) + lam_init)
    for ci, (l, acc) in enumerate(results):
        o = acc * (1.0 / l)
        o = o[:, :cq] - lam * o[:, cq:]
        o = o * lax.rsqrt(jnp.mean(o * o, axis=0, keepdims=True) + EPS) * (1.0 - lam_init)
        y_ref[ci * cq:(ci + 1) * cq, :] = o.T.astype(y_ref.dtype)


def _diff_attention(seg, b, s, kt, vt, dl, lam_init, rope):
    nk, tk = kt.shape[2], kt.shape[3]
    tq = min(DIFF_Q_TILE, s)
    nq = s // tq
    use_rope = rope is not None
    q0 = COL_DIFF // 128
    in_specs = [pl.BlockSpec((tq, 128), lambda bi, h, qi: (bi * nq + qi, q0 + h)),
                pl.BlockSpec((1, 1, nk, tk, 128), lambda bi, h, qi: (bi, h, 0, 0, 0)),
                pl.BlockSpec((1, 1, nk, 128 + SUM_ROWS, tk), lambda bi, h, qi: (bi, h, 0, 0, 0)),
                pl.BlockSpec((8, 128), lambda bi, h, qi: (0, 0))]
    args = [seg, kt, vt, dl]
    if use_rope:
        in_specs += [pl.BlockSpec((tq, 128), lambda bi, h, qi: (qi, 0))] * 3
        args += list(rope)
    return pl.pallas_call(
        functools.partial(_diff_attn_body, use_rope=use_rope, nk=nk, tq=tq, lam_init=lam_init),
        out_shape=jax.ShapeDtypeStruct((b * s, H_DIFF * 2 * DIFF_HD), BF16),
        grid=(b, H_DIFF, nq),
        in_specs=in_specs,
        out_specs=pl.BlockSpec((tq, 128), lambda bi, h, qi: (bi * nq + qi, h)),
        compiler_params=_cparams("parallel", "parallel", "arbitrary"),
        name="diff_attn",
    )(*args)


def _mla_attn_body(*refs, use_rope, nk):
    refs = list(refs)
    q_ref, kt_ref, vt_ref = refs[:3]
    refs = refs[3:]
    if use_rope:
        rope_refs = refs[:3]
        refs = refs[3:]
    (y_ref,) = refs
    tq = q_ref.shape[0]
    cq = min(tq, MLA_CHAIN_Q)
    chains = []
    for r in range(0, tq, cq):
        rows = slice(r, r + cq)
        for j in range(2):
            q = q_ref[rows, LANES * j:LANES * (j + 1)].astype(F32)
            if use_rope:
                q = _rope(q, *rope_refs, 112, 16, rows)
            q_t = (q * (MLA_SCALE * LOG2_E)).T.astype(BF16)
            chains.append((q_t, lambda i, j=j: kt_ref[0, j, i], lambda i, j=j: vt_ref[0, j, i]))
    outs = [acc * (1.0 / l) for l, acc in _flash_keymajor(chains, nk, MLA_V)]
    for ri, r in enumerate(range(0, tq, cq)):
        pair = jnp.concatenate(outs[2 * ri:2 * ri + 2], axis=0)
        y_ref[r:r + cq, :] = pair.T.astype(y_ref.dtype)


def _mla_attention(q_all, b, s, kt, vt, rope):
    nk, tk = kt.shape[2], kt.shape[3]
    tq = min(MLA_Q_TILE, s)
    nq = s // tq
    use_rope = rope is not None
    in_specs = [pl.BlockSpec((tq, 2 * LANES), lambda bi, p, qi: (bi * nq + qi, p)),
                pl.BlockSpec((1, 2, nk, tk, LANES), lambda bi, p, qi: (bi, p, 0, 0, 0)),
                pl.BlockSpec((1, 2, nk, MLA_V + SUM_ROWS, tk), lambda bi, p, qi: (bi, p, 0, 0, 0))]
    args = [q_all, kt, vt]
    if use_rope:
        in_specs += [pl.BlockSpec((tq, 128), lambda bi, p, qi: (qi, 0))] * 3
        args += list(rope)
    return pl.pallas_call(
        functools.partial(_mla_attn_body, use_rope=use_rope, nk=nk),
        out_shape=jax.ShapeDtypeStruct((b * s, H_MLA * MLA_V), BF16),
        grid=(b, H_MLA // 2, nq),
        in_specs=in_specs,
        out_specs=pl.BlockSpec((tq, 2 * MLA_V), lambda bi, p, qi: (bi * nq + qi, p)),
        compiler_params=_cparams("parallel", "parallel", "arbitrary"),
        name="mla_attn",
    )(*args)


def _merge_body(x_ref, yr_ref, yd_ref, ym_ref, gt_ref, mod_ref, wbr_ref, wo_ref, o_ref):
    merged = None
    for gi, y_ref in enumerate((yr_ref, yd_ref, ym_ref)):
        br = _dot(y_ref[...], wbr_ref[gi])
        term = _sigmoid(gt_ref[:, D_MODEL * gi:D_MODEL * (gi + 1)].astype(F32)) * br
        merged = term if merged is None else merged + term
    out = _dot(merged.astype(BF16), wo_ref[...])
    o_ref[...] = x_ref[...] + mod_ref[0][2:3, :] * out


def _merge(x, yr, yd, ym, seg, mod, modsel, wbr, wo):
    t = x.shape[0]
    tm = min(TOKEN_TILE, t)
    mod_row = _mod_row(modsel, tm)
    row = lambda i: (i, 0)
    return pl.pallas_call(
        _merge_body,
        out_shape=jax.ShapeDtypeStruct((t, D_MODEL), F32),
        grid=(t // tm,),
        in_specs=[pl.BlockSpec((tm, D_MODEL), row),
                  pl.BlockSpec((tm, 512), row), pl.BlockSpec((tm, 512), row), pl.BlockSpec((tm, 512), row),
                  pl.BlockSpec((tm, 3 * D_MODEL), lambda i: (i, COL_GATE // (3 * D_MODEL))),
                  pl.BlockSpec((1, 8, D_MODEL), lambda i: (mod_row(i), 0, 0)),
                  pl.BlockSpec(wbr.shape, lambda i: (0, 0, 0)),
                  pl.BlockSpec(wo.shape, lambda i: (0, 0))],
        out_specs=pl.BlockSpec((tm, D_MODEL), row),
        compiler_params=_cparams("parallel"),
        name="merge",
    )(x, yr, yd, ym, seg, mod, wbr, wo)


def _ffn_body(*refs, n_chunks, final):
    if final:
        x_ref, g_ref, mod_ref, wa_ref, wb_ref, wo_ref, fg_ref, o_ref, h_ref, acc_ref = refs
    else:
        x_ref, g_ref, mod_ref, wa_ref, wb_ref, wo_ref, o_ref, h_ref, acc_ref = refs
    c = pl.program_id(1)
    m = mod_ref[0]

    @pl.when(c == 0)
    def _():
        h = _rms(x_ref[...]) * g_ref[...]
        h_ref[...] = (h * (1.0 + m[4:5, :]) + m[3:4, :]).astype(BF16)
        acc_ref[...] = jnp.zeros(acc_ref.shape, F32)

    h = h_ref[...]
    u = _silu(_dot(h, wa_ref[...])) * _dot(h, wb_ref[...])
    acc_ref[...] += _dot(u.astype(BF16), wo_ref[...])

    @pl.when(c == n_chunks - 1)
    def _():
        xn = x_ref[...] + m[5:6, :] * acc_ref[...]
        if final:
            xn = _rms(xn) * fg_ref[...]
        o_ref[...] = xn


def _ffn(x, g, mod, modsel, wa, wb, wo, final_g):
    t = x.shape[0]
    tm = min(TOKEN_TILE, t)
    mod_row = _mod_row(modsel, tm)
    fc = FFN_CHUNK
    n_chunks = D_FF // fc
    final = final_g is not None
    in_specs = [pl.BlockSpec((tm, D_MODEL), lambda i, c: (i, 0)),
                pl.BlockSpec((1, D_MODEL), lambda i, c: (0, 0)),
                pl.BlockSpec((1, 8, D_MODEL), lambda i, c: (mod_row(i), 0, 0)),
                pl.BlockSpec((D_MODEL, fc), lambda i, c: (0, c)),
                pl.BlockSpec((D_MODEL, fc), lambda i, c: (0, c)),
                pl.BlockSpec((fc, D_MODEL), lambda i, c: (c, 0))]
    args = [x, g.reshape(1, D_MODEL), mod, wa, wb, wo]
    if final:
        in_specs.append(pl.BlockSpec((1, D_MODEL), lambda i, c: (0, 0)))
        args.append(final_g.reshape(1, D_MODEL))
    return pl.pallas_call(
        functools.partial(_ffn_body, n_chunks=n_chunks, final=final),
        out_shape=jax.ShapeDtypeStruct((t, D_MODEL), F32),
        grid=(t // tm, n_chunks),
        in_specs=in_specs,
        out_specs=pl.BlockSpec((tm, D_MODEL), lambda i, c: (i, 0)),
        scratch_shapes=[pltpu.VMEM((tm, D_MODEL), BF16), pltpu.VMEM((tm, D_MODEL), F32)],
        compiler_params=_cparams("parallel", "arbitrary"),
        name="ffn",
    )(*args)


def _axial_angles(n_tokens, rot_dim):
    t = jnp.arange(n_tokens)
    row = (t // GRID_W).astype(F32)
    col = (t % GRID_W).astype(F32)
    nf = rot_dim // 4
    inv = ROPE_BASE ** (-jnp.arange(nf, dtype=F32) / nf)
    ang = jnp.concatenate([row[:, None] * inv, col[:, None] * inv], axis=-1)
    return jnp.cos(ang), jnp.sin(ang)


def _rope_tables_head64(n_tokens):
    cos, sin = _axial_angles(n_tokens, 64)
    zero = jnp.zeros_like(sin)
    c = jnp.tile(jnp.concatenate([cos, cos], axis=-1), (1, 2))
    sa = jnp.tile(jnp.concatenate([-sin, zero], axis=-1), (1, 2))
    sb = jnp.tile(jnp.concatenate([zero, sin], axis=-1), (1, 2))
    return c, sa, sb


def _rope_tables_mla(n_tokens):
    cos, sin = _axial_angles(n_tokens, MLA_ROPE)
    z16 = jnp.zeros_like(sin)
    one64 = jnp.ones((n_tokens, MLA_NOPE), F32)
    z64 = jnp.zeros((n_tokens, MLA_NOPE), F32)
    z32 = jnp.zeros((n_tokens, 32), F32)
    c = jnp.concatenate([one64, cos, cos, z32], axis=-1)
    sa = jnp.concatenate([z64, -sin, z16, z32], axis=-1)
    sb = jnp.concatenate([z64, z16, sin, z32], axis=-1)
    return c, sa, sb


def _layer_weights(l, w_in, w_uq, w_ukv, w_branch, w_out, w_ffn_in, w_ffn_out):
    wi = w_in[l]
    d = wi.shape[0]
    w_ret = wi[:, 0:1536]
    w_diff = wi[:, 1536:3072]
    w_mla = jnp.concatenate([wi[:, 3456:3712], jnp.zeros((d, 64), wi.dtype), wi[:, 3712:3744],
                             jnp.zeros((d, 32), wi.dtype), wi[:, 3072:3456]], axis=1)
    w_gate = wi[:, 3744:]
    w_proj = jnp.concatenate([w_gate, w_ret, w_diff, w_mla], axis=1)
    assert w_proj.shape[1] == N_PROJ
    hd = MLA_NOPE + MLA_ROPE
    wq = jnp.pad(w_uq[l].reshape(Q_LORA, H_MLA, hd), ((0, 0), (0, 0), (0, LANES - hd))).reshape(Q_LORA, H_MLA * LANES)
    wkv = w_ukv[l].reshape(KV_LORA, H_MLA, MLA_NOPE + MLA_V)
    wk = jnp.pad(wkv[:, :, :MLA_NOPE], ((0, 0), (0, 0), (0, LANES - MLA_NOPE))).reshape(KV_LORA, H_MLA * LANES)
    wv = wkv[:, :, MLA_NOPE:].reshape(KV_LORA, H_MLA * MLA_V)
    wa, wb = w_ffn_in[l][:, :D_FF], w_ffn_in[l][:, D_FF:]
    cast = lambda a: a.astype(BF16)
    return dict(proj=cast(w_proj), wq=cast(wq), wk=cast(wk), wv=cast(wv), wbr=cast(w_branch[l]), wo=cast(w_out[l]),
                wa=cast(wa), wb=cast(wb), wf=cast(w_ffn_out[l]))


def _mixer_layer(x, b, s, l, lw, mod, modsel, seg_dtype, norm1_g, norm2_g, tabs, mla_q_norm, mla_kv_norm, dl,
                 lam_init, s0f, s0b, caches, ropes, final_g):
    t = b * s
    tk = min(KEY_TILE, s)
    rope64, rope_mla = ropes if ropes is not None else (None, None)
    tm = min(TOKEN_TILE if seg_dtype == F32 else 2 * TOKEN_TILE, modsel[1])
    seg = _norm_matmul(x, 0, D_MODEL, norm1_g, lw["proj"], mod=mod, mod_row=_mod_row(modsel, tm), tm=tm,
                       tn=PROJ_TILE_N, out_dtype=seg_dtype, name="proj_in")

    y_ret, fin_f, fin_b = _retention(seg, b, s, s0f, s0b, tabs, rope64)

    cache_k, cache_v, cache_ckv, cache_kpe = caches if caches is not None else (None,) * 4
    kt, vt = _diff_prep(seg, b, s, tk, l, cache_k, cache_v, rope64)
    y_diff = _diff_attention(seg, b, s, kt, vt, dl, lam_init, rope64)

    q_mla = _norm_matmul(seg, (COL_MLA + KV_LORA + LANES) // Q_LORA, Q_LORA, mla_q_norm, lw["wq"],
                         tm=min(TOKEN_TILE, t), tn=H_MLA * LANES, out_dtype=F32, name="mla_q")
    kt_m, vt_m, ckv_n = _mla_prep(seg, b, s, tk, l, mla_kv_norm, lw["wk"], lw["wv"], cache_ckv, cache_kpe, rope_mla)
    y_mla = _mla_attention(q_mla, b, s, kt_m, vt_m, rope_mla)

    x = _merge(x, y_ret, y_diff, y_mla, seg, mod, modsel, lw["wbr"], lw["wo"])
    x = _ffn(x, norm2_g, mod, modsel, lw["wa"], lw["wb"], lw["wf"], final_g)
    return x, (fin_f, fin_b, seg, ckv_n)


def kernel(x_prompt, x_sample, state_ret_fwd, state_ret_bwd, cache_diff_k, cache_diff_v, cache_mla_ckv,
           cache_mla_kpe, c, c_ctx, norm1_g, norm2_g, w_ada, b_ada, w_in, ret_decay_fwd, ret_decay_bwd,
           diff_lambda, mla_q_norm, mla_kv_norm, w_uq, w_ukv, w_branch, w_out, w_ffn_in, w_ffn_out, final_g):
    bp, sp, d = x_prompt.shape
    bs, ss, _ = x_sample.shape
    depth = w_in.shape[0]
    past = cache_diff_k.shape[2]

    n_rows = -(-(1 + bs) // 8) * 8
    cond = jnp.zeros((n_rows, d), F32).at[0].set(c_ctx).at[1:1 + bs].set(c)
    mod_all = _modulation(cond, w_ada.astype(BF16), b_ada)
    mod_all = jnp.pad(mod_all.reshape(depth, n_rows, 6, d), ((0, 0), (0, 0), (0, 2), (0, 0)))

    sel_prompt = (0, bp * sp)
    sel_sample = (1, ss)

    ropes = (_rope_tables_head64(ss), _rope_tables_mla(ss))
    cache_k = cache_diff_k.reshape(bs, depth, past, H_DIFF * 2 * DIFF_HD)
    cache_v = cache_diff_v.reshape(bs, depth, past, H_DIFF * 2 * DIFF_HD)
    cache_kpe = jnp.pad(cache_mla_kpe, ((0, 0), (0, 0), (0, 0), (MLA_NOPE, LANES - MLA_NOPE - MLA_ROPE)))
    zero_state = jnp.zeros((bp, H_RET, RET_DK, RET_DV), F32)

    xp = x_prompt.reshape(bp * sp, d)
    xs = x_sample.reshape(bs * ss, d)
    ret_f, ret_b, dks, dvs, ckvs, kpes = [], [], [], [], [], []
    for l in range(depth):
        lw = _layer_weights(l, w_in, w_uq, w_ukv, w_branch, w_out, w_ffn_in, w_ffn_out)
        lam_init = 0.8 - 0.6 * math.exp(-0.3 * l)
        tabs = _retention_tables(ret_decay_fwd[l], ret_decay_bwd[l])
        dl = jnp.pad(diff_lambda[l], ((0, 4), (0, LANES - DIFF_HD)))
        fg = final_g if l == depth - 1 else None
        common = (norm1_g[l], norm2_g[l], tabs, mla_q_norm[l], mla_kv_norm[l], dl, lam_init)

        xp, ctx = _mixer_layer(xp, bp, sp, l, lw, mod_all[l], sel_prompt, F32, *common, zero_state, zero_state,
                               None, None, fg)
        fin_f, fin_b, seg, ckv_n = ctx
        ret_f.append(fin_f)
        ret_b.append(fin_b)
        dks.append(seg[:, COL_DIFF + 512:COL_DIFF + 1024].reshape(bp, sp, H_DIFF, 2 * DIFF_HD))
        dvs.append(seg[:, COL_DIFF + 1024:COL_DIFF + 1536].reshape(bp, sp, H_DIFF, 2 * DIFF_HD))
        ckvs.append(ckv_n.reshape(bp, sp, KV_LORA))
        kpe0 = COL_MLA + KV_LORA + MLA_NOPE
        kpes.append(seg[:, kpe0:kpe0 + MLA_ROPE].reshape(bp, sp, MLA_ROPE))

        xs, _ = _mixer_layer(xs, bs, ss, l, lw, mod_all[l], sel_sample, BF16, *common, state_ret_fwd[:, l],
                             state_ret_bwd[:, l], (cache_k, cache_v, cache_mla_ckv, cache_kpe), ropes, fg)

    stack = lambda parts: jnp.stack(parts, axis=1)
    return (xp.reshape(bp, sp, d), xs.reshape(bs, ss, d), stack(ret_f), stack(ret_b), stack(dks), stack(dvs),
            stack(ckvs), stack(kpes))
```

```python
import functools
import math

import jax
import jax.numpy as jnp
from jax import lax
from jax.experimental import pallas as pl
from jax.experimental.pallas import tpu as pltpu

F32 = jnp.float32
BF16 = jnp.bfloat16

D_MODEL = 1024
GRID_W = 64
H_RET, RET_DK, RET_DV, RET_CHUNK = 4, 64, 128, 128
H_DIFF, DIFF_HD = 4, 64
H_MLA, MLA_NOPE, MLA_ROPE, MLA_V = 8, 64, 32, 64
Q_LORA, KV_LORA = 384, 256
D_FF = 2816
ROPE_BASE = 10000.0
EPS = 1e-6
MLA_SCALE = (MLA_NOPE + MLA_ROPE) ** -0.5
LANES = 128
NEG_BIG = -1e30
LOG2_E = math.log2(math.e)

TOKEN_TILE = 512
FFN_CHUNK = 1024
DIFF_Q_TILE = 1024
DIFF_CHAIN_Q = 256
MLA_Q_TILE = 1024
MLA_CHAIN_Q = 512
KEY_TILE = 512
RET_GROUP = 8
SUM_ROWS = 16
PROJ_TILE_N = 2304

N_PROJ = 6912
COL_GATE = 0
COL_RET = 3072
COL_DIFF = 4608
COL_MLA = 6144
VMEM_LIMIT = 56 * 1024 * 1024


def _cparams(*sem):
    return pltpu.CompilerParams(dimension_semantics=sem, vmem_limit_bytes=VMEM_LIMIT)


def _mod_row(modsel, tm):
    base, tokens_per_row = modsel
    assert tokens_per_row % tm == 0
    return lambda i: base + (i * tm) // tokens_per_row


def _dot(a, b):
    return jnp.dot(a, b, preferred_element_type=F32)


def _dot_nt(a, b):
    return lax.dot_general(a, b, (((1,), (1,)), ((), ())), preferred_element_type=F32)


def _rms(x):
    return x * lax.rsqrt(jnp.mean(x * x, axis=-1, keepdims=True) + EPS)


def _silu(x):
    return x * (1.0 / (1.0 + jnp.exp(-x)))


def _sigmoid(x):
    return 1.0 / (1.0 + jnp.exp(-x))


def _rope(x, c_ref, sa_ref, sb_ref, shift_a, shift_b, rows=slice(None)):
    return (x * c_ref[rows, :] + pltpu.roll(x, shift_a, 1) * sa_ref[rows, :]
            + pltpu.roll(x, shift_b, 1) * sb_ref[rows, :])


def _mod_body(c_ref, w_ref, b_ref, o_ref):
    o_ref[0] = _dot(_silu(c_ref[...]).astype(BF16), w_ref[0]) + b_ref[0]


def _modulation(cond, w_ada, b_ada):
    depth, d, n = w_ada.shape
    rows = cond.shape[0]
    tn = 1536
    return pl.pallas_call(
        _mod_body,
        out_shape=jax.ShapeDtypeStruct((depth, rows, n), F32),
        grid=(depth, n // tn),
        in_specs=[pl.BlockSpec((rows, d), lambda l, j: (0, 0)),
                  pl.BlockSpec((1, d, tn), lambda l, j: (l, 0, j)),
                  pl.BlockSpec((1, 1, tn), lambda l, j: (l, 0, j))],
        out_specs=pl.BlockSpec((1, rows, tn), lambda l, j: (l, 0, j)),
        compiler_params=_cparams("parallel", "parallel"),
        name="adaln_mod",
    )(cond, w_ada, b_ada.reshape(depth, 1, n))


def _norm_matmul_body(*refs, has_mod, tn, rope_cols):
    refs = list(refs)
    x_ref, g_ref = refs[:2]
    refs = refs[2:]
    if has_mod:
        mod_ref = refs.pop(0)
    w_ref = refs.pop(0)
    if rope_cols:
        rope64, rope_mla = refs[:3], refs[3:6]
        refs = refs[6:]
    (o_ref,) = refs
    h = _rms(x_ref[...].astype(F32)) * g_ref[...]
    if has_mod:
        m = mod_ref[0]
        h = h * (1.0 + m[1:2, :]) + m[0:1, :]
    h = h.astype(BF16)
    for c0 in range(0, w_ref.shape[1], tn):
        r = _dot(h, w_ref[:, c0:c0 + tn])
        o_ref[:, c0:c0 + tn] = r.astype(o_ref.dtype)
        for col, kind in rope_cols:
            if c0 <= col < c0 + tn:
                blk = r[:, col - c0:col - c0 + LANES]
                rot = _rope(blk, *rope64, 96, 32) if kind == "head64" else _rope(blk, *rope_mla, 112, 16)
                o_ref[:, col:col + LANES] = rot.astype(o_ref.dtype)


def _norm_matmul(x, x_col, k, g, w, *, mod=None, mod_row=None, tm, tn, out_dtype, name, rope=None, rope_cols=()):
    t = x.shape[0]
    n = w.shape[1]
    assert t % tm == 0 and n % tn == 0
    in_specs = [pl.BlockSpec((tm, k), lambda i: (i, x_col)),
                pl.BlockSpec((1, k), lambda i: (0, 0))]
    args = [x, g.reshape(1, k)]
    if mod is not None:
        in_specs.append(pl.BlockSpec((1, 8, k), lambda i: (mod_row(i), 0, 0)))
        args.append(mod)
    in_specs.append(pl.BlockSpec((k, n), lambda i: (0, 0), pipeline_mode=pl.Buffered(1)))
    args.append(w)
    if rope is not None:
        tabs64, tabs_mla, seq = rope
        assert seq % tm == 0
        in_specs += [pl.BlockSpec((tm, LANES), lambda i: (i % (seq // tm), 0))] * 6
        args += list(tabs64) + list(tabs_mla)
    else:
        rope_cols = ()
    return pl.pallas_call(
        functools.partial(_norm_matmul_body, has_mod=mod is not None, tn=tn, rope_cols=tuple(rope_cols)),
        out_shape=jax.ShapeDtypeStruct((t, n), out_dtype),
        grid=(t // tm,),
        in_specs=in_specs,
        out_specs=pl.BlockSpec((tm, n), lambda i: (i, 0)),
        compiler_params=_cparams("parallel"),
        name=name,
    )(*args)


def _ret_scan_body(gc_ref, kf_ref, vf_ref, kb_ref, vb_ref, zf_ref, zb_ref, s0f_ref, s0b_ref,
                   sf_ref, sb_ref, finf_ref, finb_ref, st_ref, *, n_steps, group):
    p = pl.program_id(1)
    t = pl.program_id(2)
    c = RET_CHUNK

    @pl.when(t == 0)
    def _():
        st_ref[0] = s0f_ref[0]
        st_ref[1] = s0b_ref[0]

    def local_terms(gi, k_ref, v_ref, z_ref):
        rows = slice(gi * c, (gi + 1) * c)
        k = k_ref[rows, :].astype(F32)
        kz_t =(k * (RET_DK ** -0.5) * z_ref[0]).T.astype(BF16)
        v = v_ref[rows, :].astype(BF16)
        return [_dot(kz_t[RET_DK * j:RET_DK * (j + 1)], v[:, RET_DV * j:RET_DV * (j + 1)]) for j in range(2)]

    loc_f = [local_terms(gi, kf_ref, vf_ref, zf_ref) for gi in range(group)]
    loc_b = [local_terms(gi, kb_ref, vb_ref, zb_ref) for gi in range(group)]
    ascending = list(range(group))
    for d, loc, out_ref, order in ((0, loc_f, sf_ref, ascending), (1, loc_b, sb_ref, ascending[::-1])):
        for j in range(2):
            decay = gc_ref[2 * p + j, d]
            st = st_ref[d, j]
            for gi in order:
                out_ref[0, j, gi] = st
                st = decay * st + loc[gi][j]
            st_ref[d, j] = st

    @pl.when(t == n_steps - 1)
    def _():
        finf_ref[0] = st_ref[0]
        finb_ref[0] = st_ref[1]


def _ret_out_body(q_ref, k_ref, v_ref, g_ref, sf_ref, sb_ref, dm_ref, xif_ref, xib_ref, y_ref, *, group):
    c = RET_CHUNK
    lane = lax.broadcasted_iota(jnp.int32, (c, 2 * RET_DK), 1)
    stage1 = []
    for gi in range(group):
        rows = slice(gi * c, (gi + 1) * c)
        q = q_ref[rows, :].astype(F32)
        k = k_ref[rows, :].astype(F32)
        kb =(k * (RET_DK ** -0.5)).astype(BF16)
        s_f = sf_ref[0, :, gi].reshape(2 * RET_DK, RET_DV).astype(BF16)
        s_b = sb_ref[0, :, gi].reshape(2 * RET_DK, RET_DV).astype(BF16)
        for j in range(2):
            in_head = (lane >= RET_DK * j) & (lane < RET_DK * (j + 1))
            qm = jnp.where(in_head, q, 0.0).astype(BF16)
            sc = (_dot_nt(qm, kb) * dm_ref[j]).astype(BF16)
            cross = _dot(qm, s_f) * xif_ref[j] + _dot(qm, s_b) * xib_ref[j]
            stage1.append((rows, j, sc, cross))
    for rows, j, sc, cross in stage1:
        cols = slice(RET_DV * j, RET_DV * (j + 1))
        o = _dot(sc, v_ref[rows, cols].astype(BF16)) + cross
        mu = jnp.mean(o, axis=-1, keepdims=True)
        oc = o - mu
        on = oc * lax.rsqrt(jnp.mean(oc * oc, axis=-1, keepdims=True) + EPS)
        y_ref[rows, cols] = (_silu(g_ref[rows, cols].astype(F32)) * on).astype(y_ref.dtype)


def _retention(seg, b, s, s0f, s0b, tabs):
    c = RET_CHUNK
    n = s // c
    group = math.gcd(n, RET_GROUP)
    ns = n // group
    rows = group * c
    gc, zf, zb, dmat, xif, xib = tabs
    rq0, rk0 = COL_RET // 128, (COL_RET + 256) // 128
    rv0, rg0 = (COL_RET + 512) // 256, (COL_RET + 1024) // 256
    state_shape = jax.ShapeDtypeStruct((b, H_RET, n, RET_DK, RET_DV), F32)
    fin_shape = jax.ShapeDtypeStruct((b, H_RET, RET_DK, RET_DV), F32)

    def fwd(bi, t):
        return bi * ns + t

    def bwd(bi, t):
        return bi * ns + (ns - 1 - t)

    in_specs = [
        pl.BlockSpec(memory_space=pltpu.SMEM),
        pl.BlockSpec((rows, 128), lambda bi, p, t: (fwd(bi, t), rk0 + p)),
        pl.BlockSpec((rows, 256), lambda bi, p, t: (fwd(bi, t), rv0 + p)),
        pl.BlockSpec((rows, 128), lambda bi, p, t: (bwd(bi, t), rk0 + p)),
        pl.BlockSpec((rows, 256), lambda bi, p, t: (bwd(bi, t), rv0 + p)),
        pl.BlockSpec((1, c, 128), lambda bi, p, t: (p, 0, 0)),
        pl.BlockSpec((1, c, 128), lambda bi, p, t: (p, 0, 0)),
        pl.BlockSpec((1, 2, RET_DK, RET_DV), lambda bi, p, t: (bi, p, 0, 0)),
        pl.BlockSpec((1, 2, RET_DK, RET_DV), lambda bi, p, t: (bi, p, 0, 0)),
    ]
    args = [gc, seg, seg, seg, seg, zf, zb, s0f, s0b]
    sf, sb, fin_f, fin_b = pl.pallas_call(
        functools.partial(_ret_scan_body, n_steps=ns, group=group),
        out_shape=(state_shape, state_shape, fin_shape, fin_shape),
        grid=(b, 2, ns),
        in_specs=in_specs,
        out_specs=(pl.BlockSpec((1, 2, group, RET_DK, RET_DV), lambda bi, p, t: (bi, p, t, 0, 0)),
                   pl.BlockSpec((1, 2, group, RET_DK, RET_DV), lambda bi, p, t: (bi, p, ns - 1 - t, 0, 0)),
                   pl.BlockSpec((1, 2, RET_DK, RET_DV), lambda bi, p, t: (bi, p, 0, 0)),
                   pl.BlockSpec((1, 2, RET_DK, RET_DV), lambda bi, p, t: (bi, p, 0, 0))),
        scratch_shapes=[pltpu.VMEM((2, 2, RET_DK, RET_DV), F32)],
        compiler_params=_cparams("parallel", "parallel", "arbitrary"),
        name="ret_scan",
    )(*args)

    in_specs = [
        pl.BlockSpec((rows, 128), lambda bi, p, t: (fwd(bi, t), rq0 + p)),
        pl.BlockSpec((rows, 128), lambda bi, p, t: (fwd(bi, t), rk0 + p)),
        pl.BlockSpec((rows, 256), lambda bi, p, t: (fwd(bi, t), rv0 + p)),
        pl.BlockSpec((rows, 256), lambda bi, p, t: (fwd(bi, t), rg0 + p)),
        pl.BlockSpec((1, 2, group, RET_DK, RET_DV), lambda bi, p, t: (bi, p, t, 0, 0)),
        pl.BlockSpec((1, 2, group, RET_DK, RET_DV), lambda bi, p, t: (bi, p, t, 0, 0)),
        pl.BlockSpec((2, c, c), lambda bi, p, t: (p, 0, 0)),
        pl.BlockSpec((2, c, 128), lambda bi, p, t: (p, 0, 0)),
        pl.BlockSpec((2, c, 128), lambda bi, p, t: (p, 0, 0)),
    ]
    args = [seg, seg, seg, seg, sf, sb, dmat, xif, xib]
    y = pl.pallas_call(
        functools.partial(_ret_out_body, group=group),
        out_shape=jax.ShapeDtypeStruct((b * s, H_RET * RET_DV), BF16),
        grid=(b, 2, ns),
        in_specs=in_specs,
        out_specs=pl.BlockSpec((rows, 256), lambda bi, p, t: (fwd(bi, t), p)),
        compiler_params=_cparams("parallel", "parallel", "parallel"),
        name="ret_out",
    )(*args)
    return y, fin_f, fin_b


def _retention_tables(decay_f, decay_b):
    c = RET_CHUNK
    lg_f = jax.nn.log_sigmoid(decay_f.astype(F32))
    lg_b = jax.nn.log_sigmoid(decay_b.astype(F32))
    i = jnp.arange(c, dtype=F32)
    dist = i[:, None] - i[None, :]
    d_f = jnp.where(dist >= 0, jnp.exp(jnp.maximum(dist, 0.0)[None] * lg_f[:, None, None]), 0.0)
    d_b = jnp.where(dist < 0, jnp.exp(jnp.maximum(-dist, 0.0)[None] * lg_b[:, None, None]), 0.0)
    dmat = d_f + d_b
    xi_f = jnp.exp((i + 1.0)[None, :] * lg_f[:, None])
    xi_b = jnp.exp((c - i)[None, :] * lg_b[:, None])
    zeta_f = jnp.exp((c - 1.0 - i)[None, :] * lg_f[:, None])
    zeta_b = jnp.exp(i[None, :] * lg_b[:, None])
    gc = jnp.stack([jnp.exp(c * lg_f), jnp.exp(c * lg_b)], axis=1)

    def lanes(tab, width):
        return jnp.broadcast_to(tab[:, :, None], tab.shape + (width,))

    def pair(tab):
        t64 = lanes(tab, RET_DK).reshape(H_RET // 2, 2, c, RET_DK)
        return jnp.concatenate([t64[:, 0], t64[:, 1]], axis=-1)

    return gc, pair(zeta_f), pair(zeta_b), dmat, lanes(xi_f, RET_DV), lanes(xi_b, RET_DV)


def _diff_prep_body(*refs, n_cache):
    refs = list(refs)
    if n_cache:
        ck_ref, cv_ref = refs[:2]
        refs = refs[2:]
    k_ref, v_ref, kt_ref, vt_ref = refs
    c = pl.program_id(1)
    dv = 2 * DIFF_HD
    ones = jnp.ones((SUM_ROWS, vt_ref.shape[-1]), BF16)

    def emit(load_k, load_v):
        for h in range(H_DIFF):
            cols = slice(dv * h, dv * (h + 1))
            kt_ref[0, h, 0] = (load_k(cols).astype(F32) * (DIFF_HD ** -0.5)).astype(BF16)
            vt_ref[0, h, 0, :dv, :] = load_v(cols).astype(F32).T.astype(BF16)
            vt_ref[0, h, 0, dv:, :] = ones

    def from_new():
        emit(lambda cols: k_ref[:, cols], lambda cols: v_ref[:, cols])

    def from_cache():
        emit(lambda cols: ck_ref[0, 0, :, cols], lambda cols: cv_ref[0, 0, :, cols])

    if n_cache:
        pl.when(c < n_cache)(from_cache)
        pl.when(c >= n_cache)(from_new)
    else:
        from_new()


def _diff_prep(seg, b, s, tk, layer, cache_k, cache_v):
    n_new = s // tk
    n_cache = 0 if cache_k is None else cache_k.shape[2] // tk
    nk = n_cache + n_new

    def new_row(bi, c):
        return bi * n_new + jnp.maximum(c - n_cache, 0)

    in_specs, args = [], []
    width = H_DIFF * 2 * DIFF_HD
    if n_cache:
        spec = pl.BlockSpec((1, 1, tk, width), lambda bi, c: (bi, layer, jnp.minimum(c, n_cache - 1), 0))
        in_specs += [spec, spec]
        args += [cache_k, cache_v]
    k0, v0 = (COL_DIFF + width) // width, (COL_DIFF + 2 * width) // width
    in_specs += [pl.BlockSpec((tk, width), lambda bi, c: (new_row(bi, c), k0)),
                 pl.BlockSpec((tk, width), lambda bi, c: (new_row(bi, c), v0))]
    args += [seg, seg]
    return pl.pallas_call(
        functools.partial(_diff_prep_body, n_cache=n_cache),
        out_shape=(jax.ShapeDtypeStruct((b, H_DIFF, nk, tk, 128), BF16),
                   jax.ShapeDtypeStruct((b, H_DIFF, nk, 128 + SUM_ROWS, tk), BF16)),
        grid=(b, nk),
        in_specs=in_specs,
        out_specs=(pl.BlockSpec((1, H_DIFF, 1, tk, 128), lambda bi, c: (bi, 0, c, 0, 0)),
                   pl.BlockSpec((1, H_DIFF, 1, 128 + SUM_ROWS, tk), lambda bi, c: (bi, 0, c, 0, 0))),
        compiler_params=_cparams("parallel", "arbitrary"),
        name="diff_prep",
    )(*args)


def _mla_prep_body(*refs, n_cache, emit_ckv):
    refs = list(refs)
    if n_cache:
        cc_ref, cp_ref = refs[:2]
        refs = refs[2:]
    ckv_ref, kpe_ref, g_ref, wk_ref, wv_ref, kt_ref, vt_ref = refs[:7]
    ckvn_ref = refs[7] if emit_ckv else None
    c = pl.program_id(1)

    def emit(cn, kp):
        cb = cn.astype(BF16)
        kfull = _dot(cb, wk_ref[...])
        vt = _dot(cb, wv_ref[...]).T
        ones = jnp.ones((SUM_ROWS, vt.shape[-1]), BF16)
        for h in range(H_MLA):
            kt_ref[0, h, 0] = (kfull[:, LANES * h:LANES * (h + 1)] + kp).astype(BF16)
            vt_ref[0, h, 0, :MLA_V, :] = vt[MLA_V * h:MLA_V * (h + 1)].astype(BF16)
            vt_ref[0, h, 0, MLA_V:, :] = ones

    def from_new():
        cn = _rms(ckv_ref[...].astype(F32)) * g_ref[...]
        if emit_ckv:
            ckvn_ref[...] = cn
        emit(cn, kpe_ref[...].astype(F32))

    def from_cache():
        emit(cc_ref[0, 0], cp_ref[0, 0])

    if n_cache:
        pl.when(c < n_cache)(from_cache)
        pl.when(c >= n_cache)(from_new)
    else:
        from_new()


def _mla_prep(seg, b, s, tk, layer, g, wk, wv, cache_ckv, cache_kpe):
    n_new = s // tk
    n_cache = 0 if cache_ckv is None else cache_ckv.shape[2] // tk
    nk = n_cache + n_new
    emit_ckv = n_cache == 0

    def new_row(bi, c):
        return bi * n_new + jnp.maximum(c - n_cache, 0)

    in_specs, args = [], []
    if n_cache:
        in_specs += [pl.BlockSpec((1, 1, tk, KV_LORA), lambda bi, c: (bi, layer, jnp.minimum(c, n_cache - 1), 0)),
                     pl.BlockSpec((1, 1, tk, LANES), lambda bi, c: (bi, layer, jnp.minimum(c, n_cache - 1), 0))]
        args += [cache_ckv, cache_kpe]
    in_specs += [pl.BlockSpec((tk, KV_LORA), lambda bi, c: (new_row(bi, c), COL_MLA // KV_LORA)),
                 pl.BlockSpec((tk, LANES), lambda bi, c: (new_row(bi, c), (COL_MLA + KV_LORA) // LANES)),
                 pl.BlockSpec((1, KV_LORA), lambda bi, c: (0, 0)),
                 pl.BlockSpec(wk.shape, lambda bi, c: (0, 0)),
                 pl.BlockSpec(wv.shape, lambda bi, c: (0, 0))]
    args += [seg, seg, g.reshape(1, KV_LORA), wk, wv]
    out_shape = [jax.ShapeDtypeStruct((b, H_MLA, nk, tk, LANES), BF16),
                 jax.ShapeDtypeStruct((b, H_MLA, nk, MLA_V + SUM_ROWS, tk), BF16)]
    out_specs = [pl.BlockSpec((1, H_MLA, 1, tk, LANES), lambda bi, c: (bi, 0, c, 0, 0)),
                 pl.BlockSpec((1, H_MLA, 1, MLA_V + SUM_ROWS, tk), lambda bi, c: (bi, 0, c, 0, 0))]
    if emit_ckv:
        out_shape.append(jax.ShapeDtypeStruct((b * s, KV_LORA), F32))
        out_specs.append(pl.BlockSpec((tk, KV_LORA), lambda bi, c: (new_row(bi, c), 0)))
    outs = pl.pallas_call(
        functools.partial(_mla_prep_body, n_cache=n_cache, emit_ckv=emit_ckv),
        out_shape=tuple(out_shape),
        grid=(b, nk),
        in_specs=in_specs,
        out_specs=tuple(out_specs),
        compiler_params=_cparams("parallel", "arbitrary"),
        name="mla_prep",
    )(*args)
    return outs if emit_ckv else (outs[0], outs[1], None)


def _flash_keymajor(chains, nk, dv):
    m, acc, s_next = [], [], []
    for q_t, k_tile, _ in chains:
        n = q_t.shape[1]
        m.append(jnp.full((1, n), NEG_BIG, F32))
        acc.append(jnp.zeros((dv + SUM_ROWS, n), F32))
        s_next.append(_dot(k_tile(0), q_t))
    for i in range(nk):
        for ci, (q_t, k_tile, vt_tile) in enumerate(chains):
            s = s_next[ci]
            if i + 1 < nk:
                s_next[ci] = _dot(k_tile(i + 1), q_t)
            m_new = jnp.maximum(m[ci], jnp.max(s, axis=0, keepdims=True))
            alpha = jnp.exp2(m[ci] - m_new)
            p = jnp.exp2(s - m_new)
            acc[ci] = alpha * acc[ci] + _dot(vt_tile(i), p.astype(BF16))
            m[ci] = m_new
    return [(a[dv:dv + 1], a[:dv]) for a in acc]


def _diff_attn_body(q_ref, kt_ref, vt_ref, dl_ref, y_ref, *, nk, tq, lam_init):
    cq = min(tq, DIFF_CHAIN_Q)
    lane = lax.broadcasted_iota(jnp.int32, (cq, 2 * DIFF_HD), 1)
    chains = []
    for r in range(0, tq, cq):
        q = q_ref[r:r + cq, :].astype(F32) * LOG2_E
        q1 = jnp.where(lane < DIFF_HD, q, 0.0)
        q2 = jnp.where(lane >= DIFF_HD, q, 0.0)
        q_t = jnp.concatenate([q1, q2], axis=0).T.astype(BF16)
        chains.append((q_t, lambda i: kt_ref[0, 0, i], lambda i: vt_ref[0, 0, i]))
    results = _flash_keymajor(chains, nk, 2 * DIFF_HD)
    dl = dl_ref[...]
    lam = (jnp.exp(jnp.sum(dl[0:1] * dl[1:2], axis=-1, keepdims=True))
           - jnp.exp(jnp.sum(dl[2:3] * dl[3:4], axis=-1, keepdims=True)) + lam_init)
    for ci, (l, acc) in enumerate(results):
        o = acc * (1.0 / l)
        o = o[:, :cq] - lam * o[:, cq:]
        o = o * lax.rsqrt(jnp.mean(o * o, axis=0, keepdims=True) + EPS) * (1.0 - lam_init)
        y_ref[ci * cq:(ci + 1) * cq, :] = o.T.astype(y_ref.dtype)


def _diff_attention(seg, b, s, kt, vt, dl, lam_init):
    nk, tk = kt.shape[2], kt.shape[3]
    tq = min(DIFF_Q_TILE, s)
    nq = s // tq
    q0 = COL_DIFF // 128
    in_specs = [pl.BlockSpec((tq, 128), lambda bi, h, qi: (bi * nq + qi, q0 + h)),
                pl.BlockSpec((1, 1, nk, tk, 128), lambda bi, h, qi: (bi, h, 0, 0, 0)),
                pl.BlockSpec((1, 1, nk, 128 + SUM_ROWS, tk), lambda bi, h, qi: (bi, h, 0, 0, 0)),
                pl.BlockSpec((8, 128), lambda bi, h, qi: (0, 0))]
    args = [seg, kt, vt, dl]
    return pl.pallas_call(
        functools.partial(_diff_attn_body, nk=nk, tq=tq, lam_init=lam_init),
        out_shape=jax.ShapeDtypeStruct((b * s, H_DIFF * 2 * DIFF_HD), BF16),
        grid=(b, H_DIFF, nq),
        in_specs=in_specs,
        out_specs=pl.BlockSpec((tq, 128), lambda bi, h, qi: (bi * nq + qi, h)),
        compiler_params=_cparams("parallel", "parallel", "arbitrary"),
        name="diff_attn",
    )(*args)


def _mla_attn_body(*refs, use_rope, nk):
    refs = list(refs)
    q_ref, kt_ref, vt_ref = refs[:3]
    refs = refs[3:]
    if use_rope:
        rope_refs = refs[:3]
        refs = refs[3:]
    (y_ref,) = refs
    tq = q_ref.shape[0]
    cq = min(tq, MLA_CHAIN_Q)
    chains = []
    for r in range(0, tq, cq):
        rows = slice(r, r + cq)
        for j in range(2):
            q = q_ref[rows, LANES * j:LANES * (j + 1)].astype(F32)
            if use_rope:
                q = _rope(q, *rope_refs, 112, 16, rows)
            q_t = (q * (MLA_SCALE * LOG2_E)).T.astype(BF16)
            chains.append((q_t, lambda i, j=j: kt_ref[0, j, i], lambda i, j=j: vt_ref[0, j, i]))
    outs = [acc * (1.0 / l) for l, acc in _flash_keymajor(chains, nk, MLA_V)]
    for ri, r in enumerate(range(0, tq, cq)):
        pair = jnp.concatenate(outs[2 * ri:2 * ri + 2], axis=0)
        y_ref[r:r + cq, :] = pair.T.astype(y_ref.dtype)


def _mla_attention(q_all, b, s, kt, vt, rope):
    nk, tk = kt.shape[2], kt.shape[3]
    tq = min(MLA_Q_TILE, s)
    nq = s // tq
    use_rope = rope is not None
    in_specs = [pl.BlockSpec((tq, 2 * LANES), lambda bi, p, qi: (bi * nq + qi, p)),
                pl.BlockSpec((1, 2, nk, tk, LANES), lambda bi, p, qi: (bi, p, 0, 0, 0)),
                pl.BlockSpec((1, 2, nk, MLA_V + SUM_ROWS, tk), lambda bi, p, qi: (bi, p, 0, 0, 0))]
    args = [q_all, kt, vt]
    if use_rope:
        in_specs += [pl.BlockSpec((tq, 128), lambda bi, p, qi: (qi, 0))] * 3
        args += list(rope)
    return pl.pallas_call(
        functools.partial(_mla_attn_body, use_rope=use_rope, nk=nk),
        out_shape=jax.ShapeDtypeStruct((b * s, H_MLA * MLA_V), BF16),
        grid=(b, H_MLA // 2, nq),
        in_specs=in_specs,
        out_specs=pl.BlockSpec((tq, 2 * MLA_V), lambda bi, p, qi: (bi * nq + qi, p)),
        compiler_params=_cparams("parallel", "parallel", "arbitrary"),
        name="mla_attn",
    )(*args)


def _merge_body(x_ref, yr_ref, yd_ref, ym_ref, gt_ref, mod_ref, wbr_ref, wo_ref, o_ref):
    merged = None
    for gi, y_ref in enumerate((yr_ref, yd_ref, ym_ref)):
        br = _dot(y_ref[...], wbr_ref[gi])
        term = _sigmoid(gt_ref[:, D_MODEL * gi:D_MODEL * (gi + 1)].astype(F32)) * br
        merged = term if merged is None else merged + term
    out = _dot(merged.astype(BF16), wo_ref[...])
    o_ref[...] = x_ref[...] + mod_ref[0][2:3, :] * out


def _merge(x, yr, yd, ym, seg, mod, modsel, wbr, wo):
    t = x.shape[0]
    tm = min(TOKEN_TILE, t)
    mod_row = _mod_row(modsel, tm)
    row = lambda i: (i, 0)
    return pl.pallas_call(
        _merge_body,
        out_shape=jax.ShapeDtypeStruct((t, D_MODEL), F32),
        grid=(t // tm,),
        in_specs=[pl.BlockSpec((tm, D_MODEL), row),
                  pl.BlockSpec((tm, 512), row), pl.BlockSpec((tm, 512), row), pl.BlockSpec((tm, 512), row),
                  pl.BlockSpec((tm, 3 * D_MODEL), lambda i: (i, COL_GATE // (3 * D_MODEL))),
                  pl.BlockSpec((1, 8, D_MODEL), lambda i: (mod_row(i), 0, 0)),
                  pl.BlockSpec(wbr.shape, lambda i: (0, 0, 0)),
                  pl.BlockSpec(wo.shape, lambda i: (0, 0))],
        out_specs=pl.BlockSpec((tm, D_MODEL), row),
        compiler_params=_cparams("parallel"),
        name="merge",
    )(x, yr, yd, ym, seg, mod, wbr, wo)


def _ffn_body(*refs, final):
    if final:
        x_ref, g_ref, mod_ref, wa_ref, wb_ref, wo_ref, fg_ref, o_ref = refs
    else:
        x_ref, g_ref, mod_ref, wa_ref, wb_ref, wo_ref, o_ref = refs
    m = mod_ref[0]
    x = x_ref[...]
    h = ((_rms(x) * g_ref[...]) * (1.0 + m[4:5, :]) + m[3:4, :]).astype(BF16)
    d_ff = wa_ref.shape[1]
    chunks = [(lo, min(lo + FFN_CHUNK, d_ff)) for lo in range(0, d_ff, FFN_CHUNK)]

    def up(k):
        lo, hi = chunks[k]
        return _dot(h, wa_ref[:, lo:hi]), _dot(h, wb_ref[:, lo:hi])

    nxt = up(0)
    y = None
    for k, (lo, hi) in enumerate(chunks):
        a, b = nxt
        if k + 1 < len(chunks):
            nxt = up(k + 1)
        d = _dot((_silu(a) * b).astype(BF16), wo_ref[lo:hi, :])
        y = d if y is None else y + d
    xn = x + m[5:6, :] * y
    if final:
        xn = _rms(xn) * fg_ref[...]
    o_ref[...] = xn


def _ffn(x, g, mod, modsel, wa, wb, wo, final_g):
    t = x.shape[0]
    tm = min(TOKEN_TILE, t)
    mod_row = _mod_row(modsel, tm)
    final = final_g is not None
    resident = lambda a: pl.BlockSpec(a.shape, lambda i: (0, 0), pipeline_mode=pl.Buffered(1))
    in_specs = [pl.BlockSpec((tm, D_MODEL), lambda i: (i, 0)),
                pl.BlockSpec((1, D_MODEL), lambda i: (0, 0)),
                pl.BlockSpec((1, 8, D_MODEL), lambda i: (mod_row(i), 0, 0)),
                resident(wa), resident(wb), resident(wo)]
    args = [x, g.reshape(1, D_MODEL), mod, wa, wb, wo]
    if final:
        in_specs.append(pl.BlockSpec((1, D_MODEL), lambda i: (0, 0)))
        args.append(final_g.reshape(1, D_MODEL))
    return pl.pallas_call(
        functools.partial(_ffn_body, final=final),
        out_shape=jax.ShapeDtypeStruct((t, D_MODEL), F32),
        grid=(t // tm,),
        in_specs=in_specs,
        out_specs=pl.BlockSpec((tm, D_MODEL), lambda i: (i, 0)),
        compiler_params=_cparams("parallel"),
        name="ffn",
    )(*args)


def _axial_angles(n_tokens, rot_dim):
    t = jnp.arange(n_tokens)
    row = (t // GRID_W).astype(F32)
    col = (t % GRID_W).astype(F32)
    nf = rot_dim // 4
    inv = ROPE_BASE ** (-jnp.arange(nf, dtype=F32) / nf)
    ang = jnp.concatenate([row[:, None] * inv, col[:, None] * inv], axis=-1)
    return jnp.cos(ang), jnp.sin(ang)


def _rope_tables_head64(n_tokens):
    cos, sin = _axial_angles(n_tokens, 64)
    zero = jnp.zeros_like(sin)
    c = jnp.tile(jnp.concatenate([cos, cos], axis=-1), (1, 2))
    sa = jnp.tile(jnp.concatenate([-sin, zero], axis=-1), (1, 2))
    sb = jnp.tile(jnp.concatenate([zero, sin], axis=-1), (1, 2))
    return c, sa, sb


def _rope_tables_mla(n_tokens):
    cos, sin = _axial_angles(n_tokens, MLA_ROPE)
    z16 = jnp.zeros_like(sin)
    one64 = jnp.ones((n_tokens, MLA_NOPE), F32)
    z64 = jnp.zeros((n_tokens, MLA_NOPE), F32)
    z32 = jnp.zeros((n_tokens, 32), F32)
    c = jnp.concatenate([one64, cos, cos, z32], axis=-1)
    sa = jnp.concatenate([z64, -sin, z16, z32], axis=-1)
    sb = jnp.concatenate([z64, z16, sin, z32], axis=-1)
    return c, sa, sb


def _layer_weights(l, w_in, w_uq, w_ukv, w_branch, w_out, w_ffn_in, w_ffn_out):
    wi = w_in[l]
    d = wi.shape[0]
    w_ret = wi[:, 0:1536]
    w_diff = wi[:, 1536:3072]
    w_mla = jnp.concatenate([wi[:, 3456:3712], jnp.zeros((d, 64), wi.dtype), wi[:, 3712:3744],
                             jnp.zeros((d, 32), wi.dtype), wi[:, 3072:3456]], axis=1)
    w_gate = wi[:, 3744:]
    w_proj = jnp.concatenate([w_gate, w_ret, w_diff, w_mla], axis=1)
    assert w_proj.shape[1] == N_PROJ
    hd = MLA_NOPE + MLA_ROPE
    wq = jnp.pad(w_uq[l].reshape(Q_LORA, H_MLA, hd), ((0, 0), (0, 0), (0, LANES - hd))).reshape(Q_LORA, H_MLA * LANES)
    wkv = w_ukv[l].reshape(KV_LORA, H_MLA, MLA_NOPE + MLA_V)
    wk = jnp.pad(wkv[:, :, :MLA_NOPE], ((0, 0), (0, 0), (0, LANES - MLA_NOPE))).reshape(KV_LORA, H_MLA * LANES)
    wv = wkv[:, :, MLA_NOPE:].reshape(KV_LORA, H_MLA * MLA_V)
    wa, wb = w_ffn_in[l][:, :D_FF], w_ffn_in[l][:, D_FF:]
    cast = lambda a: a.astype(BF16)
    return dict(proj=cast(w_proj), wq=cast(wq), wk=cast(wk), wv=cast(wv), wbr=cast(w_branch[l]), wo=cast(w_out[l]),
                wa=cast(wa), wb=cast(wb), wf=cast(w_ffn_out[l]))


def _mixer_layer(x, b, s, l, lw, mod, modsel, seg_dtype, norm1_g, norm2_g, tabs, mla_q_norm, mla_kv_norm, dl,
                 lam_init, s0f, s0b, caches, ropes, final_g):
    t = b * s
    tk = min(KEY_TILE, s)
    rope_mla = ropes[1] if ropes is not None else None
    tm = min(TOKEN_TILE // 2 if seg_dtype == F32 else TOKEN_TILE, modsel[1])
    rope_blocks = ([(COL_RET + LANES * i, "head64") for i in range(4)]
                   + [(COL_DIFF + LANES * i, "head64") for i in range(8)] + [(COL_MLA + KV_LORA, "mla")])
    seg = _norm_matmul(x, 0, D_MODEL, norm1_g, lw["proj"], mod=mod, mod_row=_mod_row(modsel, tm), tm=tm,
                       tn=PROJ_TILE_N, out_dtype=seg_dtype, name="proj_in",
                       rope=None if ropes is None else (ropes[0], ropes[1], s), rope_cols=rope_blocks)

    y_ret, fin_f, fin_b = _retention(seg, b, s, s0f, s0b, tabs)

    cache_k, cache_v, cache_ckv, cache_kpe = caches if caches is not None else (None,) * 4
    kt, vt = _diff_prep(seg, b, s, tk, l, cache_k, cache_v)
    y_diff = _diff_attention(seg, b, s, kt, vt, dl, lam_init)

    q_mla = _norm_matmul(seg, (COL_MLA + KV_LORA + LANES) // Q_LORA, Q_LORA, mla_q_norm, lw["wq"],
                         tm=min(TOKEN_TILE, t), tn=H_MLA * LANES, out_dtype=F32, name="mla_q")
    kt_m, vt_m, ckv_n = _mla_prep(seg, b, s, tk, l, mla_kv_norm, lw["wk"], lw["wv"], cache_ckv, cache_kpe)
    y_mla = _mla_attention(q_mla, b, s, kt_m, vt_m, rope_mla)

    x = _merge(x, y_ret, y_diff, y_mla, seg, mod, modsel, lw["wbr"], lw["wo"])
    x = _ffn(x, norm2_g, mod, modsel, lw["wa"], lw["wb"], lw["wf"], final_g)
    return x, (fin_f, fin_b, seg, ckv_n)


def kernel(x_prompt, x_sample, state_ret_fwd, state_ret_bwd, cache_diff_k, cache_diff_v, cache_mla_ckv,
           cache_mla_kpe, c, c_ctx, norm1_g, norm2_g, w_ada, b_ada, w_in, ret_decay_fwd, ret_decay_bwd,
           diff_lambda, mla_q_norm, mla_kv_norm, w_uq, w_ukv, w_branch, w_out, w_ffn_in, w_ffn_out, final_g):
    bp, sp, d = x_prompt.shape
    bs, ss, _ = x_sample.shape
    depth = w_in.shape[0]
    past = cache_diff_k.shape[2]

    n_rows = -(-(1 + bs) // 8) * 8
    cond = jnp.zeros((n_rows, d), F32).at[0].set(c_ctx).at[1:1 + bs].set(c)
    mod_all = _modulation(cond, w_ada.astype(BF16), b_ada)
    mod_all = jnp.pad(mod_all.reshape(depth, n_rows, 6, d), ((0, 0), (0, 0), (0, 2), (0, 0)))

    sel_prompt = (0, bp * sp)
    sel_sample = (1, ss)

    ropes = (_rope_tables_head64(ss), _rope_tables_mla(ss))
    cache_k = cache_diff_k.reshape(bs, depth, past, H_DIFF * 2 * DIFF_HD)
    cache_v = cache_diff_v.reshape(bs, depth, past, H_DIFF * 2 * DIFF_HD)
    cache_kpe = jnp.pad(cache_mla_kpe, ((0, 0), (0, 0), (0, 0), (MLA_NOPE, LANES - MLA_NOPE - MLA_ROPE)))
    zero_state = jnp.zeros((bp, H_RET, RET_DK, RET_DV), F32)

    xp = x_prompt.reshape(bp * sp, d)
    xs = x_sample.reshape(bs * ss, d)
    ret_f, ret_b, dks, dvs, ckvs, kpes = [], [], [], [], [], []
    for l in range(depth):
        lw = _layer_weights(l, w_in, w_uq, w_ukv, w_branch, w_out, w_ffn_in, w_ffn_out)
        lam_init = 0.8 - 0.6 * math.exp(-0.3 * l)
        tabs = _retention_tables(ret_decay_fwd[l], ret_decay_bwd[l])
        dl = jnp.pad(diff_lambda[l], ((0, 4), (0, LANES - DIFF_HD)))
        fg = final_g if l == depth - 1 else None
        common = (norm1_g[l], norm2_g[l], tabs, mla_q_norm[l], mla_kv_norm[l], dl, lam_init)

        xp, ctx = _mixer_layer(xp, bp, sp, l, lw, mod_all[l], sel_prompt, F32, *common, zero_state, zero_state,
                               None, None, fg)
        fin_f, fin_b, seg, ckv_n = ctx
        ret_f.append(fin_f)
        ret_b.append(fin_b)
        dks.append(seg[:, COL_DIFF + 512:COL_DIFF + 1024].reshape(bp, sp, H_DIFF, 2 * DIFF_HD))
        dvs.append(seg[:, COL_DIFF + 1024:COL_DIFF + 1536].reshape(bp, sp, H_DIFF, 2 * DIFF_HD))
        ckvs.append(ckv_n.reshape(bp, sp, KV_LORA))
        kpe0 = COL_MLA + KV_LORA + MLA_NOPE
        kpes.append(seg[:, kpe0:kpe0 + MLA_ROPE].reshape(bp, sp, MLA_ROPE))

        xs, _ = _mixer_layer(xs, bs, ss, l, lw, mod_all[l], sel_sample, BF16, *common, state_ret_fwd[:, l],
                             state_ret_bwd[:, l], (cache_k, cache_v, cache_mla_ckv, cache_kpe), ropes, fg)

    stack = lambda parts: jnp.stack(parts, axis=1)
    return (xp.reshape(bp, sp, d), xs.reshape(bs, ss, d), stack(ret_f), stack(ret_b), stack(dks), stack(dvs),
            stack(ckvs), stack(kpes))
```

```python
import functools
import math

import jax
import jax.numpy as jnp
from jax import lax
from jax.experimental import pallas as pl
from jax.experimental.pallas import tpu as pltpu

F32 = jnp.float32
BF16 = jnp.bfloat16

D_MODEL = 1024
GRID_W = 64
H_RET, RET_DK, RET_DV, RET_CHUNK = 4, 64, 128, 128
H_DIFF, DIFF_HD = 4, 64
H_MLA, MLA_NOPE, MLA_ROPE, MLA_V = 8, 64, 32, 64
Q_LORA, KV_LORA = 384, 256
D_FF = 2816
ROPE_BASE = 10000.0
EPS = 1e-6
MLA_SCALE = (MLA_NOPE + MLA_ROPE) ** -0.5
LANES = 128
NEG_BIG = -1e30
LOG2_E = math.log2(math.e)

TOKEN_TILE = 512
FFN_CHUNK = 1024
DIFF_Q_TILE = 1024
DIFF_CHAIN_Q = 256
MLA_Q_TILE = 1024
MLA_CHAIN_Q = 512
KEY_TILE = 512
RET_GROUP = 8
SUM_ROWS = 16
PROJ_TILE_N = 2304

N_PROJ = 6912
COL_GATE = 0
COL_RET = 3072
COL_DIFF = 4608
COL_MLA = 6144
VMEM_LIMIT = 56 * 1024 * 1024


def _cparams(*sem):
    return pltpu.CompilerParams(dimension_semantics=sem, vmem_limit_bytes=VMEM_LIMIT)


def _mod_row(modsel, tm):
    base, tokens_per_row = modsel
    assert tokens_per_row % tm == 0
    return lambda i: base + (i * tm) // tokens_per_row


def _dot(a, b):
    return jnp.dot(a, b, preferred_element_type=F32)


def _dot_nt(a, b):
    return lax.dot_general(a, b, (((1,), (1,)), ((), ())), preferred_element_type=F32)


def _rms(x):
    return x * lax.rsqrt(jnp.mean(x * x, axis=-1, keepdims=True) + EPS)


def _silu(x):
    return x * (1.0 / (1.0 + jnp.exp(-x)))


def _sigmoid(x):
    return 1.0 / (1.0 + jnp.exp(-x))


def _rope(x, c_ref, sa_ref, sb_ref, shift_a, shift_b, rows=slice(None)):
    return (x * c_ref[rows, :] + pltpu.roll(x, shift_a, 1) * sa_ref[rows, :]
            + pltpu.roll(x, shift_b, 1) * sb_ref[rows, :])


def _mod_body(c_ref, w_ref, b_ref, o_ref):
    o_ref[0] = _dot(_silu(c_ref[...]).astype(BF16), w_ref[0]) + b_ref[0]


def _modulation(cond, w_ada, b_ada):
    depth, d, n = w_ada.shape
    rows = cond.shape[0]
    tn = 1536
    return pl.pallas_call(
        _mod_body,
        out_shape=jax.ShapeDtypeStruct((depth, rows, n), F32),
        grid=(depth, n // tn),
        in_specs=[pl.BlockSpec((rows, d), lambda l, j: (0, 0)),
                  pl.BlockSpec((1, d, tn), lambda l, j: (l, 0, j)),
                  pl.BlockSpec((1, 1, tn), lambda l, j: (l, 0, j))],
        out_specs=pl.BlockSpec((1, rows, tn), lambda l, j: (l, 0, j)),
        compiler_params=_cparams("parallel", "parallel"),
        name="adaln_mod",
    )(cond, w_ada, b_ada.reshape(depth, 1, n))


def _norm_matmul_body(*refs, has_mod, has_rope, tn, col_ops):
    refs = list(refs)
    x_ref, g_ref = refs[:2]
    refs = refs[2:]
    if has_mod:
        mod_ref = refs.pop(0)
    w_ref = refs.pop(0)
    if has_rope:
        rope64, rope_mla = refs[:3], refs[3:6]
        refs = refs[6:]
    (o_ref,) = refs
    h = _rms(x_ref[...].astype(F32)) * g_ref[...]
    if has_mod:
        m = mod_ref[0]
        h = h * (1.0 + m[1:2, :]) + m[0:1, :]
    h = h.astype(BF16)
    for c0 in range(0, w_ref.shape[1], tn):
        r = _dot(h, w_ref[:, c0:c0 + tn])
        o_ref[:, c0:c0 + tn] = r.astype(o_ref.dtype)
        for col, kind, scale in col_ops:
            if c0 <= col < c0 + tn:
                blk = r[:, col - c0:col - c0 + LANES]
                if has_rope and kind == "head64":
                    blk = _rope(blk, *rope64, 96, 32)
                elif has_rope and kind == "mla":
                    blk = _rope(blk, *rope_mla, 112, 16)
                if scale != 1.0:
                    blk = blk * scale
                o_ref[:, col:col + LANES] = blk.astype(o_ref.dtype)


def _norm_matmul(x, x_col, k, g, w, *, mod=None, mod_row=None, tm, tn, out_dtype, name, rope=None, col_ops=()):
    t = x.shape[0]
    n = w.shape[1]
    assert t % tm == 0 and n % tn == 0
    in_specs = [pl.BlockSpec((tm, k), lambda i: (i, x_col)),
                pl.BlockSpec((1, k), lambda i: (0, 0))]
    args = [x, g.reshape(1, k)]
    if mod is not None:
        in_specs.append(pl.BlockSpec((1, 8, k), lambda i: (mod_row(i), 0, 0)))
        args.append(mod)
    in_specs.append(pl.BlockSpec((k, n), lambda i: (0, 0), pipeline_mode=pl.Buffered(1)))
    args.append(w)
    if rope is not None:
        tabs64, tabs_mla, seq = rope
        assert seq % tm == 0
        in_specs += [pl.BlockSpec((tm, LANES), lambda i: (i % (seq // tm), 0))] * 6
        args += list(tabs64) + list(tabs_mla)
    else:
        col_ops = [op for op in col_ops if op[2] != 1.0]
    return pl.pallas_call(
        functools.partial(_norm_matmul_body, has_mod=mod is not None, has_rope=rope is not None, tn=tn,
                          col_ops=tuple(col_ops)),
        out_shape=jax.ShapeDtypeStruct((t, n), out_dtype),
        grid=(t // tm,),
        in_specs=in_specs,
        out_specs=pl.BlockSpec((tm, n), lambda i: (i, 0)),
        compiler_params=_cparams("parallel"),
        name=name,
    )(*args)


def _ret_scan_body(gc_ref, kf_ref, vf_ref, kb_ref, vb_ref, zf_ref, zb_ref, s0f_ref, s0b_ref,
                   sf_ref, sb_ref, finf_ref, finb_ref, st_ref, *, n_steps, group):
    p = pl.program_id(1)
    t = pl.program_id(2)
    c = RET_CHUNK

    @pl.when(t == 0)
    def _():
        st_ref[0] = s0f_ref[0]
        st_ref[1] = s0b_ref[0]

    def local_terms(gi, k_ref, v_ref, z_ref):
        rows = slice(gi * c, (gi + 1) * c)
        k = k_ref[rows, :].astype(F32)
        kz_t =(k * (RET_DK ** -0.5) * z_ref[0]).T.astype(BF16)
        v = v_ref[rows, :].astype(BF16)
        return [_dot(kz_t[RET_DK * j:RET_DK * (j + 1)], v[:, RET_DV * j:RET_DV * (j + 1)]) for j in range(2)]

    loc_f = [local_terms(gi, kf_ref, vf_ref, zf_ref) for gi in range(group)]
    loc_b = [local_terms(gi, kb_ref, vb_ref, zb_ref) for gi in range(group)]
    ascending = list(range(group))
    for d, loc, out_ref, order in ((0, loc_f, sf_ref, ascending), (1, loc_b, sb_ref, ascending[::-1])):
        for j in range(2):
            decay = gc_ref[2 * p + j, d]
            st = st_ref[d, j]
            for gi in order:
                out_ref[0, j, gi] = st
                st = decay * st + loc[gi][j]
            st_ref[d, j] = st

    @pl.when(t == n_steps - 1)
    def _():
        finf_ref[0] = st_ref[0]
        finb_ref[0] = st_ref[1]


def _ret_out_body(q_ref, k_ref, v_ref, g_ref, sf_ref, sb_ref, dm_ref, xif_ref, xib_ref, y_ref, *, group):
    c = RET_CHUNK
    lane = lax.broadcasted_iota(jnp.int32, (c, 2 * RET_DK), 1)
    stage1 = []
    for gi in range(group):
        rows = slice(gi * c, (gi + 1) * c)
        q = q_ref[rows, :].astype(F32)
        k = k_ref[rows, :].astype(F32)
        kb =(k * (RET_DK ** -0.5)).astype(BF16)
        s_f = sf_ref[0, :, gi].reshape(2 * RET_DK, RET_DV).astype(BF16)
        s_b = sb_ref[0, :, gi].reshape(2 * RET_DK, RET_DV).astype(BF16)
        for j in range(2):
            in_head = (lane >= RET_DK * j) & (lane < RET_DK * (j + 1))
            qm = jnp.where(in_head, q, 0.0).astype(BF16)
            sc = (_dot_nt(qm, kb) * dm_ref[j]).astype(BF16)
            cross = _dot(qm, s_f) * xif_ref[j] + _dot(qm, s_b) * xib_ref[j]
            stage1.append((rows, j, sc, cross))
    for rows, j, sc, cross in stage1:
        cols = slice(RET_DV * j, RET_DV * (j + 1))
        o = _dot(sc, v_ref[rows, cols].astype(BF16)) + cross
        mu = jnp.mean(o, axis=-1, keepdims=True)
        oc = o - mu
        on = oc * lax.rsqrt(jnp.mean(oc * oc, axis=-1, keepdims=True) + EPS)
        y_ref[rows, cols] = (_silu(g_ref[rows, cols].astype(F32)) * on).astype(y_ref.dtype)


def _retention(seg, b, s, s0f, s0b, tabs):
    c = RET_CHUNK
    n = s // c
    group = math.gcd(n, RET_GROUP)
    ns = n // group
    rows = group * c
    gc, zf, zb, dmat, xif, xib = tabs
    rq0, rk0 = COL_RET // 128, (COL_RET + 256) // 128
    rv0, rg0 = (COL_RET + 512) // 256, (COL_RET + 1024) // 256
    state_shape = jax.ShapeDtypeStruct((b, H_RET, n, RET_DK, RET_DV), F32)
    fin_shape = jax.ShapeDtypeStruct((b, H_RET, RET_DK, RET_DV), F32)

    def fwd(bi, t):
        return bi * ns + t

    def bwd(bi, t):
        return bi * ns + (ns - 1 - t)

    in_specs = [
        pl.BlockSpec(memory_space=pltpu.SMEM),
        pl.BlockSpec((rows, 128), lambda bi, p, t: (fwd(bi, t), rk0 + p)),
        pl.BlockSpec((rows, 256), lambda bi, p, t: (fwd(bi, t), rv0 + p)),
        pl.BlockSpec((rows, 128), lambda bi, p, t: (bwd(bi, t), rk0 + p)),
        pl.BlockSpec((rows, 256), lambda bi, p, t: (bwd(bi, t), rv0 + p)),
        pl.BlockSpec((1, c, 128), lambda bi, p, t: (p, 0, 0)),
        pl.BlockSpec((1, c, 128), lambda bi, p, t: (p, 0, 0)),
        pl.BlockSpec((1, 2, RET_DK, RET_DV), lambda bi, p, t: (bi, p, 0, 0)),
        pl.BlockSpec((1, 2, RET_DK, RET_DV), lambda bi, p, t: (bi, p, 0, 0)),
    ]
    args = [gc, seg, seg, seg, seg, zf, zb, s0f, s0b]
    sf, sb, fin_f, fin_b = pl.pallas_call(
        functools.partial(_ret_scan_body, n_steps=ns, group=group),
        out_shape=(state_shape, state_shape, fin_shape, fin_shape),
        grid=(b, 2, ns),
        in_specs=in_specs,
        out_specs=(pl.BlockSpec((1, 2, group, RET_DK, RET_DV), lambda bi, p, t: (bi, p, t, 0, 0)),
                   pl.BlockSpec((1, 2, group, RET_DK, RET_DV), lambda bi, p, t: (bi, p, ns - 1 - t, 0, 0)),
                   pl.BlockSpec((1, 2, RET_DK, RET_DV), lambda bi, p, t: (bi, p, 0, 0)),
                   pl.BlockSpec((1, 2, RET_DK, RET_DV), lambda bi, p, t: (bi, p, 0, 0))),
        scratch_shapes=[pltpu.VMEM((2, 2, RET_DK, RET_DV), F32)],
        compiler_params=_cparams("parallel", "parallel", "arbitrary"),
        name="ret_scan",
    )(*args)

    in_specs = [
        pl.BlockSpec((rows, 128), lambda bi, p, t: (fwd(bi, t), rq0 + p)),
        pl.BlockSpec((rows, 128), lambda bi, p, t: (fwd(bi, t), rk0 + p)),
        pl.BlockSpec((rows, 256), lambda bi, p, t: (fwd(bi, t), rv0 + p)),
        pl.BlockSpec((rows, 256), lambda bi, p, t: (fwd(bi, t), rg0 + p)),
        pl.BlockSpec((1, 2, group, RET_DK, RET_DV), lambda bi, p, t: (bi, p, t, 0, 0)),
        pl.BlockSpec((1, 2, group, RET_DK, RET_DV), lambda bi, p, t: (bi, p, t, 0, 0)),
        pl.BlockSpec((2, c, c), lambda bi, p, t: (p, 0, 0)),
        pl.BlockSpec((2, c, 128), lambda bi, p, t: (p, 0, 0)),
        pl.BlockSpec((2, c, 128), lambda bi, p, t: (p, 0, 0)),
    ]
    args = [seg, seg, seg, seg, sf, sb, dmat, xif, xib]
    y = pl.pallas_call(
        functools.partial(_ret_out_body, group=group),
        out_shape=jax.ShapeDtypeStruct((b * s, H_RET * RET_DV), BF16),
        grid=(b, 2, ns),
        in_specs=in_specs,
        out_specs=pl.BlockSpec((rows, 256), lambda bi, p, t: (fwd(bi, t), p)),
        compiler_params=_cparams("parallel", "parallel", "parallel"),
        name="ret_out",
    )(*args)
    return y, fin_f, fin_b


def _retention_tables(decay_f, decay_b):
    c = RET_CHUNK
    lg_f = jax.nn.log_sigmoid(decay_f.astype(F32))
    lg_b = jax.nn.log_sigmoid(decay_b.astype(F32))
    i = jnp.arange(c, dtype=F32)
    dist = i[:, None] - i[None, :]
    d_f = jnp.where(dist >= 0, jnp.exp(jnp.maximum(dist, 0.0)[None] * lg_f[:, None, None]), 0.0)
    d_b = jnp.where(dist < 0, jnp.exp(jnp.maximum(-dist, 0.0)[None] * lg_b[:, None, None]), 0.0)
    dmat = d_f + d_b
    xi_f = jnp.exp((i + 1.0)[None, :] * lg_f[:, None])
    xi_b = jnp.exp((c - i)[None, :] * lg_b[:, None])
    zeta_f = jnp.exp((c - 1.0 - i)[None, :] * lg_f[:, None])
    zeta_b = jnp.exp(i[None, :] * lg_b[:, None])
    gc = jnp.stack([jnp.exp(c * lg_f), jnp.exp(c * lg_b)], axis=1)

    def lanes(tab, width):
        return jnp.broadcast_to(tab[:, :, None], tab.shape + (width,))

    def pair(tab):
        t64 = lanes(tab, RET_DK).reshape(H_RET // 2, 2, c, RET_DK)
        return jnp.concatenate([t64[:, 0], t64[:, 1]], axis=-1)

    return gc, pair(zeta_f), pair(zeta_b), dmat, lanes(xi_f, RET_DV), lanes(xi_b, RET_DV)


def _diff_prep_body(*refs, n_cache):
    refs = list(refs)
    if n_cache:
        ck_ref, cv_ref = refs[:2]
        refs = refs[2:]
    k_ref, v_ref, kt_ref, vt_ref = refs
    c = pl.program_id(1)
    dv = 2 * DIFF_HD
    ones = jnp.ones((SUM_ROWS, vt_ref.shape[-1]), BF16)

    def emit(load_k, load_v):
        for h in range(H_DIFF):
            cols = slice(dv * h, dv * (h + 1))
            kt_ref[0, h, 0] = (load_k(cols).astype(F32) * (DIFF_HD ** -0.5)).astype(BF16)
            vt_ref[0, h, 0, :dv, :] = load_v(cols).astype(F32).T.astype(BF16)
            vt_ref[0, h, 0, dv:, :] = ones

    def from_new():
        emit(lambda cols: k_ref[:, cols], lambda cols: v_ref[:, cols])

    def from_cache():
        emit(lambda cols: ck_ref[0, 0, :, cols], lambda cols: cv_ref[0, 0, :, cols])

    if n_cache:
        pl.when(c < n_cache)(from_cache)
        pl.when(c >= n_cache)(from_new)
    else:
        from_new()


def _diff_prep(seg, b, s, tk, layer, cache_k, cache_v):
    n_new = s // tk
    n_cache = 0 if cache_k is None else cache_k.shape[2] // tk
    nk = n_cache + n_new

    def new_row(bi, c):
        return bi * n_new + jnp.maximum(c - n_cache, 0)

    in_specs, args = [], []
    width = H_DIFF * 2 * DIFF_HD
    if n_cache:
        spec = pl.BlockSpec((1, 1, tk, width), lambda bi, c: (bi, layer, jnp.minimum(c, n_cache - 1), 0))
        in_specs += [spec, spec]
        args += [cache_k, cache_v]
    k0, v0 = (COL_DIFF + width) // width, (COL_DIFF + 2 * width) // width
    in_specs += [pl.BlockSpec((tk, width), lambda bi, c: (new_row(bi, c), k0)),
                 pl.BlockSpec((tk, width), lambda bi, c: (new_row(bi, c), v0))]
    args += [seg, seg]
    return pl.pallas_call(
        functools.partial(_diff_prep_body, n_cache=n_cache),
        out_shape=(jax.ShapeDtypeStruct((b, H_DIFF, nk, tk, 128), BF16),
                   jax.ShapeDtypeStruct((b, H_DIFF, nk, 128 + SUM_ROWS, tk), BF16)),
        grid=(b, nk),
        in_specs=in_specs,
        out_specs=(pl.BlockSpec((1, H_DIFF, 1, tk, 128), lambda bi, c: (bi, 0, c, 0, 0)),
                   pl.BlockSpec((1, H_DIFF, 1, 128 + SUM_ROWS, tk), lambda bi, c: (bi, 0, c, 0, 0))),
        compiler_params=_cparams("parallel", "arbitrary"),
        name="diff_prep",
    )(*args)


def _mla_prep_body(*refs, n_cache, emit_ckv):
    refs = list(refs)
    if n_cache:
        cc_ref, cp_ref = refs[:2]
        refs = refs[2:]
    ckv_ref, kpe_ref, g_ref, wk_ref, wv_ref, kt_ref, vt_ref = refs[:7]
    ckvn_ref = refs[7] if emit_ckv else None
    c = pl.program_id(1)

    def emit(cn, kp):
        cb = cn.astype(BF16)
        kfull = _dot(cb, wk_ref[...])
        vt = _dot(cb, wv_ref[...]).T
        ones = jnp.ones((SUM_ROWS, vt.shape[-1]), BF16)
        for h in range(H_MLA):
            kt_ref[0, h, 0] = (kfull[:, LANES * h:LANES * (h + 1)] + kp).astype(BF16)
            vt_ref[0, h, 0, :MLA_V, :] = vt[MLA_V * h:MLA_V * (h + 1)].astype(BF16)
            vt_ref[0, h, 0, MLA_V:, :] = ones

    def from_new():
        cn = _rms(ckv_ref[...].astype(F32)) * g_ref[...]
        if emit_ckv:
            ckvn_ref[...] = cn
        emit(cn, kpe_ref[...].astype(F32))

    def from_cache():
        emit(cc_ref[0, 0], cp_ref[0, 0])

    if n_cache:
        pl.when(c < n_cache)(from_cache)
        pl.when(c >= n_cache)(from_new)
    else:
        from_new()


def _mla_prep(seg, b, s, tk, layer, g, wk, wv, cache_ckv, cache_kpe):
    n_new = s // tk
    n_cache = 0 if cache_ckv is None else cache_ckv.shape[2] // tk
    nk = n_cache + n_new
    emit_ckv = n_cache == 0

    def new_row(bi, c):
        return bi * n_new + jnp.maximum(c - n_cache, 0)

    in_specs, args = [], []
    if n_cache:
        in_specs += [pl.BlockSpec((1, 1, tk, KV_LORA), lambda bi, c: (bi, layer, jnp.minimum(c, n_cache - 1), 0)),
                     pl.BlockSpec((1, 1, tk, LANES), lambda bi, c: (bi, layer, jnp.minimum(c, n_cache - 1), 0))]
        args += [cache_ckv, cache_kpe]
    in_specs += [pl.BlockSpec((tk, KV_LORA), lambda bi, c: (new_row(bi, c), COL_MLA // KV_LORA)),
                 pl.BlockSpec((tk, LANES), lambda bi, c: (new_row(bi, c), (COL_MLA + KV_LORA) // LANES)),
                 pl.BlockSpec((1, KV_LORA), lambda bi, c: (0, 0)),
                 pl.BlockSpec(wk.shape, lambda bi, c: (0, 0)),
                 pl.BlockSpec(wv.shape, lambda bi, c: (0, 0))]
    args += [seg, seg, g.reshape(1, KV_LORA), wk, wv]
    out_shape = [jax.ShapeDtypeStruct((b, H_MLA, nk, tk, LANES), BF16),
                 jax.ShapeDtypeStruct((b, H_MLA, nk, MLA_V + SUM_ROWS, tk), BF16)]
    out_specs = [pl.BlockSpec((1, H_MLA, 1, tk, LANES), lambda bi, c: (bi, 0, c, 0, 0)),
                 pl.BlockSpec((1, H_MLA, 1, MLA_V + SUM_ROWS, tk), lambda bi, c: (bi, 0, c, 0, 0))]
    if emit_ckv:
        out_shape.append(jax.ShapeDtypeStruct((b * s, KV_LORA), F32))
        out_specs.append(pl.BlockSpec((tk, KV_LORA), lambda bi, c: (new_row(bi, c), 0)))
    outs = pl.pallas_call(
        functools.partial(_mla_prep_body, n_cache=n_cache, emit_ckv=emit_ckv),
        out_shape=tuple(out_shape),
        grid=(b, nk),
        in_specs=in_specs,
        out_specs=tuple(out_specs),
        compiler_params=_cparams("parallel", "arbitrary"),
        name="mla_prep",
    )(*args)
    return outs if emit_ckv else (outs[0], outs[1], None)


def _flash_keymajor(chains, nk, dv):
    m, acc, s_next = [], [], []
    for q_t, k_tile, _ in chains:
        n = q_t.shape[1]
        m.append(jnp.full((1, n), NEG_BIG, F32))
        acc.append(jnp.zeros((dv + SUM_ROWS, n), F32))
        s_next.append(_dot(k_tile(0), q_t))
    for i in range(nk):
        for ci, (q_t, k_tile, vt_tile) in enumerate(chains):
            s = s_next[ci]
            if i + 1 < nk:
                s_next[ci] = _dot(k_tile(i + 1), q_t)
            m_new = jnp.maximum(m[ci], jnp.max(s, axis=0, keepdims=True))
            alpha = jnp.exp2(m[ci] - m_new)
            p = jnp.exp2(s - m_new)
            acc[ci] = alpha * acc[ci] + _dot(vt_tile(i), p.astype(BF16))
            m[ci] = m_new
    return [(a[dv:dv + 1], a[:dv]) for a in acc]


def _diff_attn_body(q_ref, kt_ref, vt_ref, dl_ref, y_ref, *, nk, tq, lam_init):
    cq = min(tq, DIFF_CHAIN_Q)
    lane = lax.broadcasted_iota(jnp.int32, (cq, 2 * DIFF_HD), 1)
    chains = []
    for r in range(0, tq, cq):
        q = q_ref[r:r + cq, :].astype(F32)
        q1 = jnp.where(lane < DIFF_HD, q, 0.0)
        q2 = jnp.where(lane >= DIFF_HD, q, 0.0)
        q_t = jnp.concatenate([q1, q2], axis=0).T.astype(BF16)
        chains.append((q_t, lambda i: kt_ref[0, 0, i], lambda i: vt_ref[0, 0, i]))
    results = _flash_keymajor(chains, nk, 2 * DIFF_HD)
    dl = dl_ref[...]
    lam = (jnp.exp(jnp.sum(dl[0:1] * dl[1:2], axis=-1, keepdims=True))
           - jnp.exp(jnp.sum(dl[2:3] * dl[3:4], axis=-1, keepdims=True)) + lam_init)
    for ci, (l, acc) in enumerate(results):
        o = acc * (1.0 / l)
        o = o[:, :cq] - lam * o[:, cq:]
        o = o * lax.rsqrt(jnp.mean(o * o, axis=0, keepdims=True) + EPS) * (1.0 - lam_init)
        y_ref[ci * cq:(ci + 1) * cq, :] = o.T.astype(y_ref.dtype)


def _diff_attention(seg, b, s, kt, vt, dl, lam_init):
    nk, tk = kt.shape[2], kt.shape[3]
    tq = min(DIFF_Q_TILE, s)
    nq = s // tq
    q0 = COL_DIFF // 128
    in_specs = [pl.BlockSpec((tq, 128), lambda bi, h, qi: (bi * nq + qi, q0 + h)),
                pl.BlockSpec((1, 1, nk, tk, 128), lambda bi, h, qi: (bi, h, 0, 0, 0)),
                pl.BlockSpec((1, 1, nk, 128 + SUM_ROWS, tk), lambda bi, h, qi: (bi, h, 0, 0, 0)),
                pl.BlockSpec((8, 128), lambda bi, h, qi: (0, 0))]
    args = [seg, kt, vt, dl]
    return pl.pallas_call(
        functools.partial(_diff_attn_body, nk=nk, tq=tq, lam_init=lam_init),
        out_shape=jax.ShapeDtypeStruct((b * s, H_DIFF * 2 * DIFF_HD), BF16),
        grid=(b, H_DIFF, nq),
        in_specs=in_specs,
        out_specs=pl.BlockSpec((tq, 128), lambda bi, h, qi: (bi * nq + qi, h)),
        compiler_params=_cparams("parallel", "parallel", "arbitrary"),
        name="diff_attn",
    )(*args)


def _mla_attn_body(q_ref, kt_ref, vt_ref, y_ref, *, nk):
    tq = q_ref.shape[0]
    cq = min(tq, MLA_CHAIN_Q)
    chains = []
    for r in range(0, tq, cq):
        for j in range(2):
            q_t = q_ref[r:r + cq, LANES * j:LANES * (j + 1)].astype(F32).T.astype(BF16)
            chains.append((q_t, lambda i, j=j: kt_ref[0, j, i], lambda i, j=j: vt_ref[0, j, i]))
    outs = [acc * (1.0 / l) for l, acc in _flash_keymajor(chains, nk, MLA_V)]
    for ri, r in enumerate(range(0, tq, cq)):
        pair = jnp.concatenate(outs[2 * ri:2 * ri + 2], axis=0)
        y_ref[r:r + cq, :] = pair.T.astype(y_ref.dtype)


def _mla_attention(q_all, b, s, kt, vt):
    nk, tk = kt.shape[2], kt.shape[3]
    tq = min(MLA_Q_TILE, s)
    nq = s // tq
    in_specs = [pl.BlockSpec((tq, 2 * LANES), lambda bi, p, qi: (bi * nq + qi, p)),
                pl.BlockSpec((1, 2, nk, tk, LANES), lambda bi, p, qi: (bi, p, 0, 0, 0)),
                pl.BlockSpec((1, 2, nk, MLA_V + SUM_ROWS, tk), lambda bi, p, qi: (bi, p, 0, 0, 0))]
    args = [q_all, kt, vt]
    return pl.pallas_call(
        functools.partial(_mla_attn_body, nk=nk),
        out_shape=jax.ShapeDtypeStruct((b * s, H_MLA * MLA_V), BF16),
        grid=(b, H_MLA // 2, nq),
        in_specs=in_specs,
        out_specs=pl.BlockSpec((tq, 2 * MLA_V), lambda bi, p, qi: (bi * nq + qi, p)),
        compiler_params=_cparams("parallel", "parallel", "arbitrary"),
        name="mla_attn",
    )(*args)


def _merge_body(x_ref, yr_ref, yd_ref, ym_ref, gt_ref, mod_ref, wbr_ref, wo_ref, o_ref):
    merged = None
    for gi, y_ref in enumerate((yr_ref, yd_ref, ym_ref)):
        br = _dot(y_ref[...], wbr_ref[gi])
        term = _sigmoid(gt_ref[:, D_MODEL * gi:D_MODEL * (gi + 1)].astype(F32)) * br
        merged = term if merged is None else merged + term
    out = _dot(merged.astype(BF16), wo_ref[...])
    o_ref[...] = x_ref[...] + mod_ref[0][2:3, :] * out


def _merge(x, yr, yd, ym, seg, mod, modsel, wbr, wo):
    t = x.shape[0]
    tm = min(TOKEN_TILE, t)
    mod_row = _mod_row(modsel, tm)
    row = lambda i: (i, 0)
    return pl.pallas_call(
        _merge_body,
        out_shape=jax.ShapeDtypeStruct((t, D_MODEL), F32),
        grid=(t // tm,),
        in_specs=[pl.BlockSpec((tm, D_MODEL), row),
                  pl.BlockSpec((tm, 512), row), pl.BlockSpec((tm, 512), row), pl.BlockSpec((tm, 512), row),
                  pl.BlockSpec((tm, 3 * D_MODEL), lambda i: (i, COL_GATE // (3 * D_MODEL))),
                  pl.BlockSpec((1, 8, D_MODEL), lambda i: (mod_row(i), 0, 0)),
                  pl.BlockSpec(wbr.shape, lambda i: (0, 0, 0)),
                  pl.BlockSpec(wo.shape, lambda i: (0, 0))],
        out_specs=pl.BlockSpec((tm, D_MODEL), row),
        compiler_params=_cparams("parallel"),
        name="merge",
    )(x, yr, yd, ym, seg, mod, wbr, wo)


def _ffn_body(*refs, final):
    if final:
        x_ref, g_ref, mod_ref, wa_ref, wb_ref, wo_ref, fg_ref, o_ref = refs
    else:
        x_ref, g_ref, mod_ref, wa_ref, wb_ref, wo_ref, o_ref = refs
    m = mod_ref[0]
    x = x_ref[...]
    h = ((_rms(x) * g_ref[...]) * (1.0 + m[4:5, :]) + m[3:4, :]).astype(BF16)
    d_ff = wa_ref.shape[1]
    chunks = [(lo, min(lo + FFN_CHUNK, d_ff)) for lo in range(0, d_ff, FFN_CHUNK)]

    def up(k):
        lo, hi = chunks[k]
        return _dot(h, wa_ref[:, lo:hi]), _dot(h, wb_ref[:, lo:hi])

    nxt = up(0)
    y = None
    for k, (lo, hi) in enumerate(chunks):
        a, b = nxt
        if k + 1 < len(chunks):
            nxt = up(k + 1)
        d = _dot((_silu(a) * b).astype(BF16), wo_ref[lo:hi, :])
        y = d if y is None else y + d
    xn = x + m[5:6, :] * y
    if final:
        xn = _rms(xn) * fg_ref[...]
    o_ref[...] = xn


def _ffn(x, g, mod, modsel, wa, wb, wo, final_g):
    t = x.shape[0]
    tm = min(TOKEN_TILE, t)
    mod_row = _mod_row(modsel, tm)
    final = final_g is not None
    resident = lambda a: pl.BlockSpec(a.shape, lambda i: (0, 0), pipeline_mode=pl.Buffered(1))
    in_specs = [pl.BlockSpec((tm, D_MODEL), lambda i: (i, 0)),
                pl.BlockSpec((1, D_MODEL), lambda i: (0, 0)),
                pl.BlockSpec((1, 8, D_MODEL), lambda i: (mod_row(i), 0, 0)),
                resident(wa), resident(wb), resident(wo)]
    args = [x, g.reshape(1, D_MODEL), mod, wa, wb, wo]
    if final:
        in_specs.append(pl.BlockSpec((1, D_MODEL), lambda i: (0, 0)))
        args.append(final_g.reshape(1, D_MODEL))
    return pl.pallas_call(
        functools.partial(_ffn_body, final=final),
        out_shape=jax.ShapeDtypeStruct((t, D_MODEL), F32),
        grid=(t // tm,),
        in_specs=in_specs,
        out_specs=pl.BlockSpec((tm, D_MODEL), lambda i: (i, 0)),
        compiler_params=_cparams("parallel"),
        name="ffn",
    )(*args)


def _axial_angles(n_tokens, rot_dim):
    t = jnp.arange(n_tokens)
    row = (t // GRID_W).astype(F32)
    col = (t % GRID_W).astype(F32)
    nf = rot_dim // 4
    inv = ROPE_BASE ** (-jnp.arange(nf, dtype=F32) / nf)
    ang = jnp.concatenate([row[:, None] * inv, col[:, None] * inv], axis=-1)
    return jnp.cos(ang), jnp.sin(ang)


def _rope_tables_head64(n_tokens):
    cos, sin = _axial_angles(n_tokens, 64)
    zero = jnp.zeros_like(sin)
    c = jnp.tile(jnp.concatenate([cos, cos], axis=-1), (1, 2))
    sa = jnp.tile(jnp.concatenate([-sin, zero], axis=-1), (1, 2))
    sb = jnp.tile(jnp.concatenate([zero, sin], axis=-1), (1, 2))
    return c, sa, sb


def _rope_tables_mla(n_tokens):
    cos, sin = _axial_angles(n_tokens, MLA_ROPE)
    z16 = jnp.zeros_like(sin)
    one64 = jnp.ones((n_tokens, MLA_NOPE), F32)
    z64 = jnp.zeros((n_tokens, MLA_NOPE), F32)
    z32 = jnp.zeros((n_tokens, 32), F32)
    c = jnp.concatenate([one64, cos, cos, z32], axis=-1)
    sa = jnp.concatenate([z64, -sin, z16, z32], axis=-1)
    sb = jnp.concatenate([z64, z16, sin, z32], axis=-1)
    return c, sa, sb


def _layer_weights(l, w_in, w_uq, w_ukv, w_branch, w_out, w_ffn_in, w_ffn_out):
    wi = w_in[l]
    d = wi.shape[0]
    w_ret = wi[:, 0:1536]
    w_diff = wi[:, 1536:3072]
    w_mla = jnp.concatenate([wi[:, 3456:3712], jnp.zeros((d, 64), wi.dtype), wi[:, 3712:3744],
                             jnp.zeros((d, 32), wi.dtype), wi[:, 3072:3456]], axis=1)
    w_gate = wi[:, 3744:]
    w_proj = jnp.concatenate([w_gate, w_ret, w_diff, w_mla], axis=1)
    assert w_proj.shape[1] == N_PROJ
    hd = MLA_NOPE + MLA_ROPE
    wq = jnp.pad(w_uq[l].reshape(Q_LORA, H_MLA, hd), ((0, 0), (0, 0), (0, LANES - hd))).reshape(Q_LORA, H_MLA * LANES)
    wkv = w_ukv[l].reshape(KV_LORA, H_MLA, MLA_NOPE + MLA_V)
    wk = jnp.pad(wkv[:, :, :MLA_NOPE], ((0, 0), (0, 0), (0, LANES - MLA_NOPE))).reshape(KV_LORA, H_MLA * LANES)
    wv = wkv[:, :, MLA_NOPE:].reshape(KV_LORA, H_MLA * MLA_V)
    wa, wb = w_ffn_in[l][:, :D_FF], w_ffn_in[l][:, D_FF:]
    cast = lambda a: a.astype(BF16)
    return dict(proj=cast(w_proj), wq=cast(wq), wk=cast(wk), wv=cast(wv), wbr=cast(w_branch[l]), wo=cast(w_out[l]),
                wa=cast(wa), wb=cast(wb), wf=cast(w_ffn_out[l]))


def _mixer_layer(x, b, s, l, lw, mod, modsel, seg_dtype, norm1_g, norm2_g, tabs, mla_q_norm, mla_kv_norm, dl,
                 lam_init, s0f, s0b, caches, ropes, final_g):
    t = b * s
    tk = min(KEY_TILE, s)
    rope = None if ropes is None else (ropes[0], ropes[1], s)
    tm = min(TOKEN_TILE // 2 if seg_dtype == F32 else TOKEN_TILE, modsel[1])
    col_ops = ([(COL_RET + LANES * i, "head64", 1.0) for i in range(4)]
               + [(COL_DIFF + LANES * i, "head64", LOG2_E) for i in range(4)]
               + [(COL_DIFF + LANES * i, "head64", 1.0) for i in range(4, 8)] + [(COL_MLA + KV_LORA, "mla", 1.0)])
    seg = _norm_matmul(x, 0, D_MODEL, norm1_g, lw["proj"], mod=mod, mod_row=_mod_row(modsel, tm), tm=tm,
                       tn=PROJ_TILE_N, out_dtype=seg_dtype, name="proj_in", rope=rope, col_ops=col_ops)

    y_ret, fin_f, fin_b = _retention(seg, b, s, s0f, s0b, tabs)

    cache_k, cache_v, cache_ckv, cache_kpe = caches if caches is not None else (None,) * 4
    kt, vt = _diff_prep(seg, b, s, tk, l, cache_k, cache_v)
    y_diff = _diff_attention(seg, b, s, kt, vt, dl, lam_init)

    q_mla = _norm_matmul(seg, (COL_MLA + KV_LORA + LANES) // Q_LORA, Q_LORA, mla_q_norm, lw["wq"],
                         tm=min(TOKEN_TILE, s), tn=H_MLA * LANES, out_dtype=BF16, name="mla_q", rope=rope,
                         col_ops=[(LANES * h, "mla", MLA_SCALE * LOG2_E) for h in range(H_MLA)])
    kt_m, vt_m, ckv_n = _mla_prep(seg, b, s, tk, l, mla_kv_norm, lw["wk"], lw["wv"], cache_ckv, cache_kpe)
    y_mla = _mla_attention(q_mla, b, s, kt_m, vt_m)

    x = _merge(x, y_ret, y_diff, y_mla, seg, mod, modsel, lw["wbr"], lw["wo"])
    x = _ffn(x, norm2_g, mod, modsel, lw["wa"], lw["wb"], lw["wf"], final_g)
    return x, (fin_f, fin_b, seg, ckv_n)


def kernel(x_prompt, x_sample, state_ret_fwd, state_ret_bwd, cache_diff_k, cache_diff_v, cache_mla_ckv,
           cache_mla_kpe, c, c_ctx, norm1_g, norm2_g, w_ada, b_ada, w_in, ret_decay_fwd, ret_decay_bwd,
           diff_lambda, mla_q_norm, mla_kv_norm, w_uq, w_ukv, w_branch, w_out, w_ffn_in, w_ffn_out, final_g):
    bp, sp, d = x_prompt.shape
    bs, ss, _ = x_sample.shape
    depth = w_in.shape[0]
    past = cache_diff_k.shape[2]

    n_rows = -(-(1 + bs) // 8) * 8
    cond = jnp.zeros((n_rows, d), F32).at[0].set(c_ctx).at[1:1 + bs].set(c)
    mod_all = _modulation(cond, w_ada.astype(BF16), b_ada)
    mod_all = jnp.pad(mod_all.reshape(depth, n_rows, 6, d), ((0, 0), (0, 0), (0, 2), (0, 0)))

    sel_prompt = (0, bp * sp)
    sel_sample = (1, ss)

    ropes = (_rope_tables_head64(ss), _rope_tables_mla(ss))
    cache_k = cache_diff_k.reshape(bs, depth, past, H_DIFF * 2 * DIFF_HD)
    cache_v = cache_diff_v.reshape(bs, depth, past, H_DIFF * 2 * DIFF_HD)
    cache_kpe = jnp.pad(cache_mla_kpe, ((0, 0), (0, 0), (0, 0), (MLA_NOPE, LANES - MLA_NOPE - MLA_ROPE)))
    zero_state = jnp.zeros((bp, H_RET, RET_DK, RET_DV), F32)

    xp = x_prompt.reshape(bp * sp, d)
    xs = x_sample.reshape(bs * ss, d)
    ret_f, ret_b, dks, dvs, ckvs, kpes = [], [], [], [], [], []
    for l in range(depth):
        lw = _layer_weights(l, w_in, w_uq, w_ukv, w_branch, w_out, w_ffn_in, w_ffn_out)
        lam_init = 0.8 - 0.6 * math.exp(-0.3 * l)
        tabs = _retention_tables(ret_decay_fwd[l], ret_decay_bwd[l])
        dl = jnp.pad(diff_lambda[l], ((0, 4), (0, LANES - DIFF_HD)))
        fg = final_g if l == depth - 1 else None
        common = (norm1_g[l], norm2_g[l], tabs, mla_q_norm[l], mla_kv_norm[l], dl, lam_init)

        xp, ctx = _mixer_layer(xp, bp, sp, l, lw, mod_all[l], sel_prompt, F32, *common, zero_state, zero_state,
                               None, None, fg)
        fin_f, fin_b, seg, ckv_n = ctx
        ret_f.append(fin_f)
        ret_b.append(fin_b)
        dks.append(seg[:, COL_DIFF + 512:COL_DIFF + 1024].reshape(bp, sp, H_DIFF, 2 * DIFF_HD))
        dvs.append(seg[:, COL_DIFF + 1024:COL_DIFF + 1536].reshape(bp, sp, H_DIFF, 2 * DIFF_HD))
        ckvs.append(ckv_n.reshape(bp, sp, KV_LORA))
        kpe0 = COL_MLA + KV_LORA + MLA_NOPE
        kpes.append(seg[:, kpe0:kpe0 + MLA_ROPE].reshape(bp, sp, MLA_ROPE))

        xs, _ = _mixer_layer(xs, bs, ss, l, lw, mod_all[l], sel_sample, BF16, *common, state_ret_fwd[:, l],
                             state_ret_bwd[:, l], (cache_k, cache_v, cache_mla_ckv, cache_kpe), ropes, fg)

    stack = lambda parts: jnp.stack(parts, axis=1)
    return (xp.reshape(bp, sp, d), xs.reshape(bs, ss, d), stack(ret_f), stack(ret_b), stack(dks), stack(dvs),
            stack(ckvs), stack(kpes))
```

```python
import functools
import math

import jax
import jax.numpy as jnp
from jax import lax
from jax.experimental import pallas as pl
from jax.experimental.pallas import tpu as pltpu

F32 = jnp.float32
BF16 = jnp.bfloat16

D_MODEL = 1024
GRID_W = 64
H_RET, RET_DK, RET_DV, RET_CHUNK = 4, 64, 128, 128
H_DIFF, DIFF_HD = 4, 64
H_MLA, MLA_NOPE, MLA_ROPE, MLA_V = 8, 64, 32, 64
Q_LORA, KV_LORA = 384, 256
D_FF = 2816
ROPE_BASE = 10000.0
EPS = 1e-6
MLA_SCALE = (MLA_NOPE + MLA_ROPE) ** -0.5
LANES = 128
NEG_BIG = -1e30
LOG2_E = math.log2(math.e)

TOKEN_TILE = 512
FFN_CHUNK = 1024
DIFF_Q_TILE = 1024
DIFF_CHAIN_Q = 256
MLA_Q_TILE = 1024
MLA_CHAIN_Q = 512
KEY_TILE = 512
RET_GROUP = 8
SUM_ROWS = 16
PROJ_TILE_N = 2304

N_PROJ = 6912
COL_GATE = 0
COL_RET = 3072
COL_DIFF = 4608
COL_MLA = 6144
VMEM_LIMIT = 56 * 1024 * 1024


def _cparams(*sem):
    return pltpu.CompilerParams(dimension_semantics=sem, vmem_limit_bytes=VMEM_LIMIT)


def _mod_row(modsel, tm):
    base, tokens_per_row = modsel
    assert tokens_per_row % tm == 0
    return lambda i: base + (i * tm) // tokens_per_row


def _dot(a, b):
    return jnp.dot(a, b, preferred_element_type=F32)


def _dot_nt(a, b):
    return lax.dot_general(a, b, (((1,), (1,)), ((), ())), preferred_element_type=F32)


def _rms(x):
    return x * lax.rsqrt(jnp.mean(x * x, axis=-1, keepdims=True) + EPS)


def _silu(x):
    return x * (1.0 / (1.0 + jnp.exp(-x)))


def _sigmoid(x):
    return 1.0 / (1.0 + jnp.exp(-x))


def _rope(x, c_ref, sa_ref, sb_ref, shift_a, shift_b, rows=slice(None)):
    return (x * c_ref[rows, :] + pltpu.roll(x, shift_a, 1) * sa_ref[rows, :]
            + pltpu.roll(x, shift_b, 1) * sb_ref[rows, :])


def _mod_body(c_ref, w_ref, b_ref, o_ref):
    o_ref[0] = _dot(_silu(c_ref[...]).astype(BF16), w_ref[0]) + b_ref[0]


def _modulation(cond, w_ada, b_ada):
    depth, d, n = w_ada.shape
    rows = cond.shape[0]
    tn = 1536
    return pl.pallas_call(
        _mod_body,
        out_shape=jax.ShapeDtypeStruct((depth, rows, n), F32),
        grid=(depth, n // tn),
        in_specs=[pl.BlockSpec((rows, d), lambda l, j: (0, 0)),
                  pl.BlockSpec((1, d, tn), lambda l, j: (l, 0, j)),
                  pl.BlockSpec((1, 1, tn), lambda l, j: (l, 0, j))],
        out_specs=pl.BlockSpec((1, rows, tn), lambda l, j: (l, 0, j)),
        compiler_params=_cparams("parallel", "parallel"),
        name="adaln_mod",
    )(cond, w_ada, b_ada.reshape(depth, 1, n))


def _input_proj_body(*refs, has_rope, emit_ckv, tn):
    refs = list(refs)
    x_ref, g_ref, mod_ref, w_ref, qn_ref, kvn_ref, wq_ref, wk_ref, wv_ref = refs[:9]
    refs = refs[9:]
    if has_rope:
        rope64, rope_mla = refs[:3], refs[3:6]
        refs = refs[6:]
    seg_ref, qm_ref, kd_ref, vd_ref, km_ref, vm_ref = refs[:6]
    ckvn_ref = refs[6] if emit_ckv else None

    def rot64(blk):
        return _rope(blk, *rope64, 96, 32) if has_rope else blk

    def rot_mla(blk):
        return _rope(blk, *rope_mla, 112, 16) if has_rope else blk

    m = mod_ref[0]
    h = ((_rms(x_ref[...]) * g_ref[...]) * (1.0 + m[1:2, :]) + m[0:1, :]).astype(BF16)
    tm = h.shape[0]
    dv = 2 * DIFF_HD
    ones = jnp.ones((SUM_ROWS, tm), BF16)

    tiles = [COL_DIFF] + [c0 for c0 in range(0, N_PROJ, tn) if c0 != COL_DIFF]
    r = {c0: _dot(h, w_ref[:, c0:c0 + tn]) for c0 in tiles[:2]}

    def take(col, width=LANES):
        c0 = col // tn * tn
        return r[c0][:, col - c0:col - c0 + width]

    def put(col, blk):
        seg_ref[:, col:col + LANES] = blk.astype(seg_ref.dtype)

    cn = _rms(take(COL_MLA, KV_LORA)) * kvn_ref[...]
    cb = cn.astype(BF16)
    kfull = _dot(cb, wk_ref[...])
    vt = _dot(cb, wv_ref[...]).T
    q = _dot((_rms(take(COL_MLA + KV_LORA + LANES, Q_LORA)) * qn_ref[...]).astype(BF16), wq_ref[...])
    for c0 in tiles[2:]:
        r[c0] = _dot(h, w_ref[:, c0:c0 + tn])

    for c0 in tiles:
        seg_ref[:, c0:c0 + tn] = r[c0].astype(seg_ref.dtype)
    for i in range(4):
        put(COL_RET + LANES * i, rot64(take(COL_RET + LANES * i)))
    for hd in range(H_DIFF):
        put(COL_DIFF + dv * hd, rot64(take(COL_DIFF + dv * hd)) * LOG2_E)
        k = rot64(take(COL_DIFF + 512 + dv * hd))
        put(COL_DIFF + 512 + dv * hd, k)
        kd_ref[0, hd, 0] = (k * (DIFF_HD ** -0.5)).astype(BF16)
        vd_ref[0, hd, 0, :dv, :] = take(COL_DIFF + 1024 + dv * hd).T.astype(BF16)
        vd_ref[0, hd, 0, dv:, :] = ones
    if emit_ckv:
        ckvn_ref[...] = cn
    kp = rot_mla(take(COL_MLA + KV_LORA))
    put(COL_MLA + KV_LORA, kp)
    for hm in range(H_MLA):
        cols = slice(LANES * hm, LANES * (hm + 1))
        km_ref[0, hm, 0] = (kfull[:, cols] + kp).astype(BF16)
        vm_ref[0, hm, 0, :MLA_V, :] = vt[MLA_V * hm:MLA_V * (hm + 1)].astype(BF16)
        vm_ref[0, hm, 0, MLA_V:, :] = ones
        qm_ref[:, cols] = (rot_mla(q[:, cols]) * (MLA_SCALE * LOG2_E)).astype(BF16)


def _input_proj(x, b, s, tk, g, mod, modsel, lw, qn, kvn, seg_dtype, rope):
    t = b * s
    n_new = s // tk
    tn = PROJ_TILE_N
    assert COL_DIFF % tn == 0 and N_PROJ - COL_DIFF == tn and s % tk == 0
    mod_row = _mod_row(modsel, tk)
    emit_ckv = rope is None
    resident = lambda a: pl.BlockSpec(a.shape, lambda i: (0,) * a.ndim, pipeline_mode=pl.Buffered(1))
    qn2, kvn2 = qn.reshape(1, Q_LORA), kvn.reshape(1, KV_LORA)
    in_specs = [pl.BlockSpec((tk, D_MODEL), lambda i: (i, 0)),
                pl.BlockSpec((1, D_MODEL), lambda i: (0, 0)),
                pl.BlockSpec((1, 8, D_MODEL), lambda i: (mod_row(i), 0, 0)),
                resident(lw["proj"]), resident(qn2), resident(kvn2),
                resident(lw["wq"]), resident(lw["wk"]), resident(lw["wv"])]
    args = [x, g.reshape(1, D_MODEL), mod, lw["proj"], qn2, kvn2, lw["wq"], lw["wk"], lw["wv"]]
    if rope is not None:
        in_specs += [pl.BlockSpec((tk, LANES), lambda i: (i % n_new, 0))] * 6
        args += list(rope[0]) + list(rope[1])
    tile = lambda i: (i // n_new, 0, i % n_new, 0, 0)
    out_shape = [jax.ShapeDtypeStruct((t, N_PROJ), seg_dtype),
                 jax.ShapeDtypeStruct((t, H_MLA * LANES), BF16),
                 jax.ShapeDtypeStruct((b, H_DIFF, n_new, tk, LANES), BF16),
                 jax.ShapeDtypeStruct((b, H_DIFF, n_new, 2 * DIFF_HD + SUM_ROWS, tk), BF16),
                 jax.ShapeDtypeStruct((b, H_MLA, n_new, tk, LANES), BF16),
                 jax.ShapeDtypeStruct((b, H_MLA, n_new, MLA_V + SUM_ROWS, tk), BF16)]
    out_specs = [pl.BlockSpec((tk, N_PROJ), lambda i: (i, 0)),
                 pl.BlockSpec((tk, H_MLA * LANES), lambda i: (i, 0)),
                 pl.BlockSpec((1, H_DIFF, 1, tk, LANES), tile),
                 pl.BlockSpec((1, H_DIFF, 1, 2 * DIFF_HD + SUM_ROWS, tk), tile),
                 pl.BlockSpec((1, H_MLA, 1, tk, LANES), tile),
                 pl.BlockSpec((1, H_MLA, 1, MLA_V + SUM_ROWS, tk), tile)]
    if emit_ckv:
        out_shape.append(jax.ShapeDtypeStruct((t, KV_LORA), F32))
        out_specs.append(pl.BlockSpec((tk, KV_LORA), lambda i: (i, 0)))
    outs = pl.pallas_call(
        functools.partial(_input_proj_body, has_rope=rope is not None, emit_ckv=emit_ckv, tn=tn),
        out_shape=tuple(out_shape),
        grid=(t // tk,),
        in_specs=in_specs,
        out_specs=tuple(out_specs),
        compiler_params=_cparams("parallel"),
        name="input_proj",
    )(*args)
    return outs if emit_ckv else tuple(outs) + (None,)


def _ret_scan_body(gc_ref, kf_ref, vf_ref, kb_ref, vb_ref, zf_ref, zb_ref, s0f_ref, s0b_ref,
                   sf_ref, sb_ref, finf_ref, finb_ref, st_ref, *, n_steps, group):
    p = pl.program_id(1)
    t = pl.program_id(2)
    c = RET_CHUNK

    @pl.when(t == 0)
    def _():
        st_ref[0] = s0f_ref[0]
        st_ref[1] = s0b_ref[0]

    def local_terms(gi, k_ref, v_ref, z_ref):
        rows = slice(gi * c, (gi + 1) * c)
        k = k_ref[rows, :].astype(F32)
        kz_t =(k * (RET_DK ** -0.5) * z_ref[0]).T.astype(BF16)
        v = v_ref[rows, :].astype(BF16)
        return [_dot(kz_t[RET_DK * j:RET_DK * (j + 1)], v[:, RET_DV * j:RET_DV * (j + 1)]) for j in range(2)]

    loc_f = [local_terms(gi, kf_ref, vf_ref, zf_ref) for gi in range(group)]
    loc_b = [local_terms(gi, kb_ref, vb_ref, zb_ref) for gi in range(group)]
    ascending = list(range(group))
    for d, loc, out_ref, order in ((0, loc_f, sf_ref, ascending), (1, loc_b, sb_ref, ascending[::-1])):
        for j in range(2):
            decay = gc_ref[2 * p + j, d]
            st = st_ref[d, j]
            for gi in order:
                out_ref[0, j, gi] = st
                st = decay * st + loc[gi][j]
            st_ref[d, j] = st

    @pl.when(t == n_steps - 1)
    def _():
        finf_ref[0] = st_ref[0]
        finb_ref[0] = st_ref[1]


def _ret_out_body(q_ref, k_ref, v_ref, g_ref, sf_ref, sb_ref, dm_ref, xif_ref, xib_ref, y_ref, *, group):
    c = RET_CHUNK
    lane = lax.broadcasted_iota(jnp.int32, (c, 2 * RET_DK), 1)
    stage1 = []
    for gi in range(group):
        rows = slice(gi * c, (gi + 1) * c)
        q = q_ref[rows, :].astype(F32)
        k = k_ref[rows, :].astype(F32)
        kb =(k * (RET_DK ** -0.5)).astype(BF16)
        s_f = sf_ref[0, :, gi].reshape(2 * RET_DK, RET_DV).astype(BF16)
        s_b = sb_ref[0, :, gi].reshape(2 * RET_DK, RET_DV).astype(BF16)
        for j in range(2):
            in_head = (lane >= RET_DK * j) & (lane < RET_DK * (j + 1))
            qm = jnp.where(in_head, q, 0.0).astype(BF16)
            sc = (_dot_nt(qm, kb) * dm_ref[j]).astype(BF16)
            cross = _dot(qm, s_f) * xif_ref[j] + _dot(qm, s_b) * xib_ref[j]
            stage1.append((rows, j, sc, cross))
    for rows, j, sc, cross in stage1:
        cols = slice(RET_DV * j, RET_DV * (j + 1))
        o = _dot(sc, v_ref[rows, cols].astype(BF16)) + cross
        mu = jnp.mean(o, axis=-1, keepdims=True)
        oc = o - mu
        on = oc * lax.rsqrt(jnp.mean(oc * oc, axis=-1, keepdims=True) + EPS)
        y_ref[rows, cols] = (_silu(g_ref[rows, cols].astype(F32)) * on).astype(y_ref.dtype)


def _retention(seg, b, s, s0f, s0b, tabs):
    c = RET_CHUNK
    n = s // c
    group = math.gcd(n, RET_GROUP)
    ns = n // group
    rows = group * c
    gc, zf, zb, dmat, xif, xib = tabs
    rq0, rk0 = COL_RET // 128, (COL_RET + 256) // 128
    rv0, rg0 = (COL_RET + 512) // 256, (COL_RET + 1024) // 256
    state_shape = jax.ShapeDtypeStruct((b, H_RET, n, RET_DK, RET_DV), F32)
    fin_shape = jax.ShapeDtypeStruct((b, H_RET, RET_DK, RET_DV), F32)

    def fwd(bi, t):
        return bi * ns + t

    def bwd(bi, t):
        return bi * ns + (ns - 1 - t)

    in_specs = [
        pl.BlockSpec(memory_space=pltpu.SMEM),
        pl.BlockSpec((rows, 128), lambda bi, p, t: (fwd(bi, t), rk0 + p)),
        pl.BlockSpec((rows, 256), lambda bi, p, t: (fwd(bi, t), rv0 + p)),
        pl.BlockSpec((rows, 128), lambda bi, p, t: (bwd(bi, t), rk0 + p)),
        pl.BlockSpec((rows, 256), lambda bi, p, t: (bwd(bi, t), rv0 + p)),
        pl.BlockSpec((1, c, 128), lambda bi, p, t: (p, 0, 0)),
        pl.BlockSpec((1, c, 128), lambda bi, p, t: (p, 0, 0)),
        pl.BlockSpec((1, 2, RET_DK, RET_DV), lambda bi, p, t: (bi, p, 0, 0)),
        pl.BlockSpec((1, 2, RET_DK, RET_DV), lambda bi, p, t: (bi, p, 0, 0)),
    ]
    args = [gc, seg, seg, seg, seg, zf, zb, s0f, s0b]
    sf, sb, fin_f, fin_b = pl.pallas_call(
        functools.partial(_ret_scan_body, n_steps=ns, group=group),
        out_shape=(state_shape, state_shape, fin_shape, fin_shape),
        grid=(b, 2, ns),
        in_specs=in_specs,
        out_specs=(pl.BlockSpec((1, 2, group, RET_DK, RET_DV), lambda bi, p, t: (bi, p, t, 0, 0)),
                   pl.BlockSpec((1, 2, group, RET_DK, RET_DV), lambda bi, p, t: (bi, p, ns - 1 - t, 0, 0)),
                   pl.BlockSpec((1, 2, RET_DK, RET_DV), lambda bi, p, t: (bi, p, 0, 0)),
                   pl.BlockSpec((1, 2, RET_DK, RET_DV), lambda bi, p, t: (bi, p, 0, 0))),
        scratch_shapes=[pltpu.VMEM((2, 2, RET_DK, RET_DV), F32)],
        compiler_params=_cparams("parallel", "parallel", "arbitrary"),
        name="ret_scan",
    )(*args)

    in_specs = [
        pl.BlockSpec((rows, 128), lambda bi, p, t: (fwd(bi, t), rq0 + p)),
        pl.BlockSpec((rows, 128), lambda bi, p, t: (fwd(bi, t), rk0 + p)),
        pl.BlockSpec((rows, 256), lambda bi, p, t: (fwd(bi, t), rv0 + p)),
        pl.BlockSpec((rows, 256), lambda bi, p, t: (fwd(bi, t), rg0 + p)),
        pl.BlockSpec((1, 2, group, RET_DK, RET_DV), lambda bi, p, t: (bi, p, t, 0, 0)),
        pl.BlockSpec((1, 2, group, RET_DK, RET_DV), lambda bi, p, t: (bi, p, t, 0, 0)),
        pl.BlockSpec((2, c, c), lambda bi, p, t: (p, 0, 0)),
        pl.BlockSpec((2, c, 128), lambda bi, p, t: (p, 0, 0)),
        pl.BlockSpec((2, c, 128), lambda bi, p, t: (p, 0, 0)),
    ]
    args = [seg, seg, seg, seg, sf, sb, dmat, xif, xib]
    y = pl.pallas_call(
        functools.partial(_ret_out_body, group=group),
        out_shape=jax.ShapeDtypeStruct((b * s, H_RET * RET_DV), BF16),
        grid=(b, 2, ns),
        in_specs=in_specs,
        out_specs=pl.BlockSpec((rows, 256), lambda bi, p, t: (fwd(bi, t), p)),
        compiler_params=_cparams("parallel", "parallel", "parallel"),
        name="ret_out",
    )(*args)
    return y, fin_f, fin_b


def _retention_tables(decay_f, decay_b):
    c = RET_CHUNK
    lg_f = jax.nn.log_sigmoid(decay_f.astype(F32))
    lg_b = jax.nn.log_sigmoid(decay_b.astype(F32))
    i = jnp.arange(c, dtype=F32)
    dist = i[:, None] - i[None, :]
    d_f = jnp.where(dist >= 0, jnp.exp(jnp.maximum(dist, 0.0)[None] * lg_f[:, None, None]), 0.0)
    d_b = jnp.where(dist < 0, jnp.exp(jnp.maximum(-dist, 0.0)[None] * lg_b[:, None, None]), 0.0)
    dmat = d_f + d_b
    xi_f = jnp.exp((i + 1.0)[None, :] * lg_f[:, None])
    xi_b = jnp.exp((c - i)[None, :] * lg_b[:, None])
    zeta_f = jnp.exp((c - 1.0 - i)[None, :] * lg_f[:, None])
    zeta_b = jnp.exp(i[None, :] * lg_b[:, None])
    gc = jnp.stack([jnp.exp(c * lg_f), jnp.exp(c * lg_b)], axis=1)

    def lanes(tab, width):
        return jnp.broadcast_to(tab[:, :, None], tab.shape + (width,))

    def pair(tab):
        t64 = lanes(tab, RET_DK).reshape(H_RET // 2, 2, c, RET_DK)
        return jnp.concatenate([t64[:, 0], t64[:, 1]], axis=-1)

    return gc, pair(zeta_f), pair(zeta_b), dmat, lanes(xi_f, RET_DV), lanes(xi_b, RET_DV)


def _diff_cache_body(ck_ref, cv_ref, kt_ref, vt_ref):
    dv = 2 * DIFF_HD
    ones = jnp.ones((SUM_ROWS, vt_ref.shape[-1]), BF16)
    for h in range(H_DIFF):
        cols = slice(dv * h, dv * (h + 1))
        kt_ref[0, h, 0] = (ck_ref[0, 0, :, cols] * (DIFF_HD ** -0.5)).astype(BF16)
        vt_ref[0, h, 0, :dv, :] = cv_ref[0, 0, :, cols].T.astype(BF16)
        vt_ref[0, h, 0, dv:, :] = ones


def _diff_cache_prep(cache_k, cache_v, layer, tk):
    b, _, past, width = cache_k.shape
    n = past // tk
    spec = pl.BlockSpec((1, 1, tk, width), lambda bi, c: (bi, layer, c, 0))
    return pl.pallas_call(
        _diff_cache_body,
        out_shape=(jax.ShapeDtypeStruct((b, H_DIFF, n, tk, LANES), BF16),
                   jax.ShapeDtypeStruct((b, H_DIFF, n, 2 * DIFF_HD + SUM_ROWS, tk), BF16)),
        grid=(b, n),
        in_specs=[spec, spec],
        out_specs=(pl.BlockSpec((1, H_DIFF, 1, tk, LANES), lambda bi, c: (bi, 0, c, 0, 0)),
                   pl.BlockSpec((1, H_DIFF, 1, 2 * DIFF_HD + SUM_ROWS, tk), lambda bi, c: (bi, 0, c, 0, 0))),
        compiler_params=_cparams("parallel", "parallel"),
        name="diff_cache_prep",
    )(cache_k, cache_v)


def _mla_cache_body(cc_ref, cp_ref, wk_ref, wv_ref, kt_ref, vt_ref):
    cb = cc_ref[0, 0].astype(BF16)
    kfull = _dot(cb, wk_ref[...])
    vt = _dot(cb, wv_ref[...]).T
    kp = cp_ref[0, 0]
    ones = jnp.ones((SUM_ROWS, vt.shape[-1]), BF16)
    for h in range(H_MLA):
        kt_ref[0, h, 0] = (kfull[:, LANES * h:LANES * (h + 1)] + kp).astype(BF16)
        vt_ref[0, h, 0, :MLA_V, :] = vt[MLA_V * h:MLA_V * (h + 1)].astype(BF16)
        vt_ref[0, h, 0, MLA_V:, :] = ones


def _mla_cache_prep(cache_ckv, cache_kpe, layer, wk, wv, tk):
    b, _, past, _ = cache_ckv.shape
    n = past // tk
    return pl.pallas_call(
        _mla_cache_body,
        out_shape=(jax.ShapeDtypeStruct((b, H_MLA, n, tk, LANES), BF16),
                   jax.ShapeDtypeStruct((b, H_MLA, n, MLA_V + SUM_ROWS, tk), BF16)),
        grid=(b, n),
        in_specs=[pl.BlockSpec((1, 1, tk, KV_LORA), lambda bi, c: (bi, layer, c, 0)),
                  pl.BlockSpec((1, 1, tk, LANES), lambda bi, c: (bi, layer, c, 0)),
                  pl.BlockSpec(wk.shape, lambda bi, c: (0, 0)),
                  pl.BlockSpec(wv.shape, lambda bi, c: (0, 0))],
        out_specs=(pl.BlockSpec((1, H_MLA, 1, tk, LANES), lambda bi, c: (bi, 0, c, 0, 0)),
                   pl.BlockSpec((1, H_MLA, 1, MLA_V + SUM_ROWS, tk), lambda bi, c: (bi, 0, c, 0, 0))),
        compiler_params=_cparams("parallel", "parallel"),
        name="mla_cache_prep",
    )(cache_ckv, cache_kpe, wk, wv)


def _flash_keymajor(chains, nk, dv):
    m, acc, s_next = [], [], []
    for q_t, k_tile, _ in chains:
        n = q_t.shape[1]
        m.append(jnp.full((1, n), NEG_BIG, F32))
        acc.append(jnp.zeros((dv + SUM_ROWS, n), F32))
        s_next.append(_dot(k_tile(0), q_t))
    for i in range(nk):
        for ci, (q_t, k_tile, vt_tile) in enumerate(chains):
            s = s_next[ci]
            if i + 1 < nk:
                s_next[ci] = _dot(k_tile(i + 1), q_t)
            m_new = jnp.maximum(m[ci], jnp.max(s, axis=0, keepdims=True))
            alpha = jnp.exp2(m[ci] - m_new)
            p = jnp.exp2(s - m_new)
            acc[ci] = alpha * acc[ci] + _dot(vt_tile(i), p.astype(BF16))
            m[ci] = m_new
    return [(a[dv:dv + 1], a[:dv]) for a in acc]


def _key_tiles(kv_refs, counts, head):
    index = [(part, i) for part, n in enumerate(counts) for i in range(n)]
    k_tile = lambda i: kv_refs[2 * index[i][0]][0, head, index[i][1]]
    vt_tile = lambda i: kv_refs[2 * index[i][0] + 1][0, head, index[i][1]]
    return k_tile, vt_tile


def _kv_specs(parts, heads, dv, head_block):
    specs, args = [], []
    for kt, vt in parts:
        n, tk = kt.shape[2], kt.shape[3]
        specs += [pl.BlockSpec((1, heads, n, tk, LANES), lambda bi, h, qi: (bi, head_block(h), 0, 0, 0)),
                  pl.BlockSpec((1, heads, n, dv + SUM_ROWS, tk), lambda bi, h, qi: (bi, head_block(h), 0, 0, 0))]
        args += [kt, vt]
    return specs, args


def _diff_attn_body(q_ref, dl_ref, *refs, counts, tq, lam_init):
    kv_refs, y_ref = refs[:-1], refs[-1]
    k_tile, vt_tile = _key_tiles(kv_refs, counts, 0)
    cq = min(tq, DIFF_CHAIN_Q)
    lane = lax.broadcasted_iota(jnp.int32, (cq, 2 * DIFF_HD), 1)
    chains = []
    for r in range(0, tq, cq):
        q = q_ref[r:r + cq, :].astype(F32)
        q1 = jnp.where(lane < DIFF_HD, q, 0.0)
        q2 = jnp.where(lane >= DIFF_HD, q, 0.0)
        q_t = jnp.concatenate([q1, q2], axis=0).T.astype(BF16)
        chains.append((q_t, k_tile, vt_tile))
    results = _flash_keymajor(chains, sum(counts), 2 * DIFF_HD)
    dl = dl_ref[...]
    lam = (jnp.exp(jnp.sum(dl[0:1] * dl[1:2], axis=-1, keepdims=True))
           - jnp.exp(jnp.sum(dl[2:3] * dl[3:4], axis=-1, keepdims=True)) + lam_init)
    for ci, (l, acc) in enumerate(results):
        o = acc * (1.0 / l)
        o = o[:, :cq] - lam * o[:, cq:]
        o = o * lax.rsqrt(jnp.mean(o * o, axis=0, keepdims=True) + EPS) * (1.0 - lam_init)
        y_ref[ci * cq:(ci + 1) * cq, :] = o.T.astype(y_ref.dtype)


def _diff_attention(seg, b, s, kv_parts, dl, lam_init):
    tq = min(DIFF_Q_TILE, s)
    nq = s // tq
    q0 = COL_DIFF // 128
    kv_specs, kv_args = _kv_specs(kv_parts, 1, 2 * DIFF_HD, lambda h: h)
    in_specs = [pl.BlockSpec((tq, 128), lambda bi, h, qi: (bi * nq + qi, q0 + h)),
                pl.BlockSpec((8, 128), lambda bi, h, qi: (0, 0))] + kv_specs
    args = [seg, dl] + kv_args
    return pl.pallas_call(
        functools.partial(_diff_attn_body, counts=tuple(kt.shape[2] for kt, _ in kv_parts), tq=tq,
                          lam_init=lam_init),
        out_shape=jax.ShapeDtypeStruct((b * s, H_DIFF * 2 * DIFF_HD), BF16),
        grid=(b, H_DIFF, nq),
        in_specs=in_specs,
        out_specs=pl.BlockSpec((tq, 128), lambda bi, h, qi: (bi * nq + qi, h)),
        compiler_params=_cparams("parallel", "parallel", "arbitrary"),
        name="diff_attn",
    )(*args)


def _mla_attn_body(q_ref, *refs, counts):
    kv_refs, y_ref = refs[:-1], refs[-1]
    tq = q_ref.shape[0]
    cq = min(tq, MLA_CHAIN_Q)
    chains = []
    for r in range(0, tq, cq):
        for j in range(2):
            q_t = q_ref[r:r + cq, LANES * j:LANES * (j + 1)].astype(F32).T.astype(BF16)
            chains.append((q_t,) + _key_tiles(kv_refs, counts, j))
    outs = [acc * (1.0 / l) for l, acc in _flash_keymajor(chains, sum(counts), MLA_V)]
    for ri, r in enumerate(range(0, tq, cq)):
        pair = jnp.concatenate(outs[2 * ri:2 * ri + 2], axis=0)
        y_ref[r:r + cq, :] = pair.T.astype(y_ref.dtype)


def _mla_attention(q_all, b, s, kv_parts):
    tq = min(MLA_Q_TILE, s)
    nq = s // tq
    kv_specs, kv_args = _kv_specs(kv_parts, 2, MLA_V, lambda p: p)
    in_specs = [pl.BlockSpec((tq, 2 * LANES), lambda bi, p, qi: (bi * nq + qi, p))] + kv_specs
    args = [q_all] + kv_args
    return pl.pallas_call(
        functools.partial(_mla_attn_body, counts=tuple(kt.shape[2] for kt, _ in kv_parts)),
        out_shape=jax.ShapeDtypeStruct((b * s, H_MLA * MLA_V), BF16),
        grid=(b, H_MLA // 2, nq),
        in_specs=in_specs,
        out_specs=pl.BlockSpec((tq, 2 * MLA_V), lambda bi, p, qi: (bi * nq + qi, p)),
        compiler_params=_cparams("parallel", "parallel", "arbitrary"),
        name="mla_attn",
    )(*args)


def _merge_body(x_ref, yr_ref, yd_ref, ym_ref, gt_ref, mod_ref, wbr_ref, wo_ref, o_ref):
    merged = None
    for gi, y_ref in enumerate((yr_ref, yd_ref, ym_ref)):
        br = _dot(y_ref[...], wbr_ref[gi])
        term = _sigmoid(gt_ref[:, D_MODEL * gi:D_MODEL * (gi + 1)].astype(F32)) * br
        merged = term if merged is None else merged + term
    out = _dot(merged.astype(BF16), wo_ref[...])
    o_ref[...] = x_ref[...] + mod_ref[0][2:3, :] * out


def _merge(x, yr, yd, ym, seg, mod, modsel, wbr, wo):
    t = x.shape[0]
    tm = min(TOKEN_TILE, t)
    mod_row = _mod_row(modsel, tm)
    row = lambda i: (i, 0)
    return pl.pallas_call(
        _merge_body,
        out_shape=jax.ShapeDtypeStruct((t, D_MODEL), F32),
        grid=(t // tm,),
        in_specs=[pl.BlockSpec((tm, D_MODEL), row),
                  pl.BlockSpec((tm, 512), row), pl.BlockSpec((tm, 512), row), pl.BlockSpec((tm, 512), row),
                  pl.BlockSpec((tm, 3 * D_MODEL), lambda i: (i, COL_GATE // (3 * D_MODEL))),
                  pl.BlockSpec((1, 8, D_MODEL), lambda i: (mod_row(i), 0, 0)),
                  pl.BlockSpec(wbr.shape, lambda i: (0, 0, 0)),
                  pl.BlockSpec(wo.shape, lambda i: (0, 0))],
        out_specs=pl.BlockSpec((tm, D_MODEL), row),
        compiler_params=_cparams("parallel"),
        name="merge",
    )(x, yr, yd, ym, seg, mod, wbr, wo)


def _ffn_body(*refs, final):
    if final:
        x_ref, g_ref, mod_ref, wa_ref, wb_ref, wo_ref, fg_ref, o_ref = refs
    else:
        x_ref, g_ref, mod_ref, wa_ref, wb_ref, wo_ref, o_ref = refs
    m = mod_ref[0]
    x = x_ref[...]
    h = ((_rms(x) * g_ref[...]) * (1.0 + m[4:5, :]) + m[3:4, :]).astype(BF16)
    d_ff = wa_ref.shape[1]
    chunks = [(lo, min(lo + FFN_CHUNK, d_ff)) for lo in range(0, d_ff, FFN_CHUNK)]

    def up(k):
        lo, hi = chunks[k]
        return _dot(h, wa_ref[:, lo:hi]), _dot(h, wb_ref[:, lo:hi])

    nxt = up(0)
    y = None
    for k, (lo, hi) in enumerate(chunks):
        a, b = nxt
        if k + 1 < len(chunks):
            nxt = up(k + 1)
        d = _dot((_silu(a) * b).astype(BF16), wo_ref[lo:hi, :])
        y = d if y is None else y + d
    xn = x + m[5:6, :] * y
    if final:
        xn = _rms(xn) * fg_ref[...]
    o_ref[...] = xn


def _ffn(x, g, mod, modsel, wa, wb, wo, final_g):
    t = x.shape[0]
    tm = min(TOKEN_TILE, t)
    mod_row = _mod_row(modsel, tm)
    final = final_g is not None
    resident = lambda a: pl.BlockSpec(a.shape, lambda i: (0, 0), pipeline_mode=pl.Buffered(1))
    in_specs = [pl.BlockSpec((tm, D_MODEL), lambda i: (i, 0)),
                pl.BlockSpec((1, D_MODEL), lambda i: (0, 0)),
                pl.BlockSpec((1, 8, D_MODEL), lambda i: (mod_row(i), 0, 0)),
                resident(wa), resident(wb), resident(wo)]
    args = [x, g.reshape(1, D_MODEL), mod, wa, wb, wo]
    if final:
        in_specs.append(pl.BlockSpec((1, D_MODEL), lambda i: (0, 0)))
        args.append(final_g.reshape(1, D_MODEL))
    return pl.pallas_call(
        functools.partial(_ffn_body, final=final),
        out_shape=jax.ShapeDtypeStruct((t, D_MODEL), F32),
        grid=(t // tm,),
        in_specs=in_specs,
        out_specs=pl.BlockSpec((tm, D_MODEL), lambda i: (i, 0)),
        compiler_params=_cparams("parallel"),
        name="ffn",
    )(*args)


def _axial_angles(n_tokens, rot_dim):
    t = jnp.arange(n_tokens)
    row = (t // GRID_W).astype(F32)
    col = (t % GRID_W).astype(F32)
    nf = rot_dim // 4
    inv = ROPE_BASE ** (-jnp.arange(nf, dtype=F32) / nf)
    ang = jnp.concatenate([row[:, None] * inv, col[:, None] * inv], axis=-1)
    return jnp.cos(ang), jnp.sin(ang)


def _rope_tables_head64(n_tokens):
    cos, sin = _axial_angles(n_tokens, 64)
    zero = jnp.zeros_like(sin)
    c = jnp.tile(jnp.concatenate([cos, cos], axis=-1), (1, 2))
    sa = jnp.tile(jnp.concatenate([-sin, zero], axis=-1), (1, 2))
    sb = jnp.tile(jnp.concatenate([zero, sin], axis=-1), (1, 2))
    return c, sa, sb


def _rope_tables_mla(n_tokens):
    cos, sin = _axial_angles(n_tokens, MLA_ROPE)
    z16 = jnp.zeros_like(sin)
    one64 = jnp.ones((n_tokens, MLA_NOPE), F32)
    z64 = jnp.zeros((n_tokens, MLA_NOPE), F32)
    z32 = jnp.zeros((n_tokens, 32), F32)
    c = jnp.concatenate([one64, cos, cos, z32], axis=-1)
    sa = jnp.concatenate([z64, -sin, z16, z32], axis=-1)
    sb = jnp.concatenate([z64, z16, sin, z32], axis=-1)
    return c, sa, sb


def _layer_weights(l, w_in, w_uq, w_ukv, w_branch, w_out, w_ffn_in, w_ffn_out):
    wi = w_in[l]
    d = wi.shape[0]
    w_ret = wi[:, 0:1536]
    w_diff = wi[:, 1536:3072]
    w_mla = jnp.concatenate([wi[:, 3456:3712], jnp.zeros((d, 64), wi.dtype), wi[:, 3712:3744],
                             jnp.zeros((d, 32), wi.dtype), wi[:, 3072:3456]], axis=1)
    w_gate = wi[:, 3744:]
    w_proj = jnp.concatenate([w_gate, w_ret, w_diff, w_mla], axis=1)
    assert w_proj.shape[1] == N_PROJ
    hd = MLA_NOPE + MLA_ROPE
    wq = jnp.pad(w_uq[l].reshape(Q_LORA, H_MLA, hd), ((0, 0), (0, 0), (0, LANES - hd))).reshape(Q_LORA, H_MLA * LANES)
    wkv = w_ukv[l].reshape(KV_LORA, H_MLA, MLA_NOPE + MLA_V)
    wk = jnp.pad(wkv[:, :, :MLA_NOPE], ((0, 0), (0, 0), (0, LANES - MLA_NOPE))).reshape(KV_LORA, H_MLA * LANES)
    wv = wkv[:, :, MLA_NOPE:].reshape(KV_LORA, H_MLA * MLA_V)
    wa, wb = w_ffn_in[l][:, :D_FF], w_ffn_in[l][:, D_FF:]
    cast = lambda a: a.astype(BF16)
    return dict(proj=cast(w_proj), wq=cast(wq), wk=cast(wk), wv=cast(wv), wbr=cast(w_branch[l]), wo=cast(w_out[l]),
                wa=cast(wa), wb=cast(wb), wf=cast(w_ffn_out[l]))


def _mixer_layer(x, b, s, l, lw, mod, modsel, seg_dtype, norm1_g, norm2_g, tabs, mla_q_norm, mla_kv_norm, dl,
                 lam_init, s0f, s0b, caches, ropes, final_g):
    tk = min(KEY_TILE, s)
    seg, q_mla, kd, vd, km, vm, ckv_n = _input_proj(x, b, s, tk, norm1_g, mod, modsel, lw, mla_q_norm, mla_kv_norm,
                                                    seg_dtype, ropes)

    y_ret, fin_f, fin_b = _retention(seg, b, s, s0f, s0b, tabs)

    diff_parts, mla_parts = [(kd, vd)], [(km, vm)]
    if caches is not None:
        cache_k, cache_v, cache_ckv, cache_kpe = caches
        diff_parts.insert(0, _diff_cache_prep(cache_k, cache_v, l, tk))
        mla_parts.insert(0, _mla_cache_prep(cache_ckv, cache_kpe, l, lw["wk"], lw["wv"], tk))
    y_diff = _diff_attention(seg, b, s, diff_parts, dl, lam_init)
    y_mla = _mla_attention(q_mla, b, s, mla_parts)

    x = _merge(x, y_ret, y_diff, y_mla, seg, mod, modsel, lw["wbr"], lw["wo"])
    x = _ffn(x, norm2_g, mod, modsel, lw["wa"], lw["wb"], lw["wf"], final_g)
    return x, (fin_f, fin_b, seg, ckv_n)


def kernel(x_prompt, x_sample, state_ret_fwd, state_ret_bwd, cache_diff_k, cache_diff_v, cache_mla_ckv,
           cache_mla_kpe, c, c_ctx, norm1_g, norm2_g, w_ada, b_ada, w_in, ret_decay_fwd, ret_decay_bwd,
           diff_lambda, mla_q_norm, mla_kv_norm, w_uq, w_ukv, w_branch, w_out, w_ffn_in, w_ffn_out, final_g):
    bp, sp, d = x_prompt.shape
    bs, ss, _ = x_sample.shape
    depth = w_in.shape[0]
    past = cache_diff_k.shape[2]

    n_rows = -(-(1 + bs) // 8) * 8
    cond = jnp.zeros((n_rows, d), F32).at[0].set(c_ctx).at[1:1 + bs].set(c)
    mod_all = _modulation(cond, w_ada.astype(BF16), b_ada)
    mod_all = jnp.pad(mod_all.reshape(depth, n_rows, 6, d), ((0, 0), (0, 0), (0, 2), (0, 0)))

    sel_prompt = (0, bp * sp)
    sel_sample = (1, ss)

    ropes = (_rope_tables_head64(ss), _rope_tables_mla(ss))
    cache_k = cache_diff_k.reshape(bs, depth, past, H_DIFF * 2 * DIFF_HD)
    cache_v = cache_diff_v.reshape(bs, depth, past, H_DIFF * 2 * DIFF_HD)
    cache_kpe = jnp.pad(cache_mla_kpe, ((0, 0), (0, 0), (0, 0), (MLA_NOPE, LANES - MLA_NOPE - MLA_ROPE)))
    zero_state = jnp.zeros((bp, H_RET, RET_DK, RET_DV), F32)

    xp = x_prompt.reshape(bp * sp, d)
    xs = x_sample.reshape(bs * ss, d)
    ret_f, ret_b, dks, dvs, ckvs, kpes = [], [], [], [], [], []
    for l in range(depth):
        lw = _layer_weights(l, w_in, w_uq, w_ukv, w_branch, w_out, w_ffn_in, w_ffn_out)
        lam_init = 0.8 - 0.6 * math.exp(-0.3 * l)
        tabs = _retention_tables(ret_decay_fwd[l], ret_decay_bwd[l])
        dl = jnp.pad(diff_lambda[l], ((0, 4), (0, LANES - DIFF_HD)))
        fg = final_g if l == depth - 1 else None
        common = (norm1_g[l], norm2_g[l], tabs, mla_q_norm[l], mla_kv_norm[l], dl, lam_init)

        xp, ctx = _mixer_layer(xp, bp, sp, l, lw, mod_all[l], sel_prompt, F32, *common, zero_state, zero_state,
                               None, None, fg)
        fin_f, fin_b, seg, ckv_n = ctx
        ret_f.append(fin_f)
        ret_b.append(fin_b)
        dks.append(seg[:, COL_DIFF + 512:COL_DIFF + 1024].reshape(bp, sp, H_DIFF, 2 * DIFF_HD))
        dvs.append(seg[:, COL_DIFF + 1024:COL_DIFF + 1536].reshape(bp, sp, H_DIFF, 2 * DIFF_HD))
        ckvs.append(ckv_n.reshape(bp, sp, KV_LORA))
        kpe0 = COL_MLA + KV_LORA + MLA_NOPE
        kpes.append(seg[:, kpe0:kpe0 + MLA_ROPE].reshape(bp, sp, MLA_ROPE))

        xs, _ = _mixer_layer(xs, bs, ss, l, lw, mod_all[l], sel_sample, BF16, *common, state_ret_fwd[:, l],
                             state_ret_bwd[:, l], (cache_k, cache_v, cache_mla_ckv, cache_kpe), ropes, fg)

    stack = lambda parts: jnp.stack(parts, axis=1)
    return (xp.reshape(bp, sp, d), xs.reshape(bs, ss, d), stack(ret_f), stack(ret_b), stack(dks), stack(dvs),
            stack(ckvs), stack(kpes))
```

```python
import functools
import math

import jax
import jax.numpy as jnp
from jax import lax
from jax.experimental import pallas as pl
from jax.experimental.pallas import tpu as pltpu

F32 = jnp.float32
BF16 = jnp.bfloat16

D_MODEL = 1024
GRID_W = 64
H_RET, RET_DK, RET_DV, RET_CHUNK = 4, 64, 128, 128
H_DIFF, DIFF_HD = 4, 64
H_MLA, MLA_NOPE, MLA_ROPE, MLA_V = 8, 64, 32, 64
Q_LORA, KV_LORA = 384, 256
D_FF = 2816
ROPE_BASE = 10000.0
EPS = 1e-6
MLA_SCALE = (MLA_NOPE + MLA_ROPE) ** -0.5
LANES = 128
NEG_BIG = -1e30
LOG2_E = math.log2(math.e)

TOKEN_TILE = 512
FFN_CHUNK = 1024
DIFF_Q_TILE = 2048
DIFF_CHAIN_Q = 256
MLA_Q_TILE = 2048
MLA_CHAIN_Q = 512
KEY_TILE = 512
RET_GROUP = 8
SUM_ROWS = 16
PROJ_TILE_N = 2304

N_PROJ = 6912
COL_GATE = 0
COL_RET = 3072
COL_DIFF = 4608
COL_MLA = 6144
VMEM_LIMIT = 56 * 1024 * 1024


def _cparams(*sem):
    return pltpu.CompilerParams(dimension_semantics=sem, vmem_limit_bytes=VMEM_LIMIT)


def _mod_row(modsel, tm):
    base, tokens_per_row = modsel
    assert tokens_per_row % tm == 0
    return lambda i: base + (i * tm) // tokens_per_row


def _dot(a, b):
    return jnp.dot(a, b, preferred_element_type=F32)


def _dot_nt(a, b):
    return lax.dot_general(a, b, (((1,), (1,)), ((), ())), preferred_element_type=F32)


def _rms(x):
    return x * lax.rsqrt(jnp.mean(x * x, axis=-1, keepdims=True) + EPS)


def _silu(x):
    return x * (1.0 / (1.0 + jnp.exp(-x)))


def _sigmoid(x):
    return 1.0 / (1.0 + jnp.exp(-x))


def _rope(x, c_ref, sa_ref, sb_ref, shift_a, shift_b, rows=slice(None)):
    return (x * c_ref[rows, :] + pltpu.roll(x, shift_a, 1) * sa_ref[rows, :]
            + pltpu.roll(x, shift_b, 1) * sb_ref[rows, :])


def _mod_body(c_ref, w_ref, b_ref, o_ref):
    o_ref[0] = _dot(_silu(c_ref[...]).astype(BF16), w_ref[0].astype(BF16)) + b_ref[0]


def _modulation(cond, w_ada, b_ada):
    depth, d, n = w_ada.shape
    rows = cond.shape[0]
    tn = 1536
    return pl.pallas_call(
        _mod_body,
        out_shape=jax.ShapeDtypeStruct((depth, rows, n), F32),
        grid=(depth, n // tn),
        in_specs=[pl.BlockSpec((rows, d), lambda l, j: (0, 0)),
                  pl.BlockSpec((1, d, tn), lambda l, j: (l, 0, j)),
                  pl.BlockSpec((1, 1, tn), lambda l, j: (l, 0, j))],
        out_specs=pl.BlockSpec((1, rows, tn), lambda l, j: (l, 0, j)),
        compiler_params=_cparams("parallel", "parallel"),
        name="adaln_mod",
    )(cond, w_ada, b_ada.reshape(depth, 1, n))


def _input_proj_body(*refs, has_rope, n_ctx_in, tn):
    refs = list(refs)
    x_ref, g_ref, mod_ref, w_ref, qn_ref, kvn_ref, wq_ref, wk_ref, wv_ref = refs[:9]
    refs = refs[9:]
    if has_rope:
        rope64, rope_mla = refs[:3], refs[3:6]
        refs = refs[6:]
    refs = refs[n_ctx_in:]
    seg_ref, qm_ref, kd_ref, vd_ref, km_ref, vm_ref = refs[:6]
    ctx_refs = refs[6:]

    def rot64(blk):
        return _rope(blk, *rope64, 96, 32) if has_rope else blk

    def rot_mla(blk):
        return _rope(blk, *rope_mla, 112, 16) if has_rope else blk

    m = mod_ref[0]
    h = ((_rms(x_ref[...]) * g_ref[...]) * (1.0 + m[1:2, :]) + m[0:1, :]).astype(BF16)
    tm = h.shape[0]
    dv = 2 * DIFF_HD
    ones = jnp.ones((SUM_ROWS, tm), BF16)

    tiles = [COL_DIFF] + [c0 for c0 in range(0, N_PROJ, tn) if c0 != COL_DIFF]
    r = {c0: _dot(h, w_ref[:, c0:c0 + tn]) for c0 in tiles[:2]}

    def take(col, width=LANES):
        c0 = col // tn * tn
        return r[c0][:, col - c0:col - c0 + width]

    def put(col, blk):
        seg_ref[:, col:col + LANES] = blk.astype(seg_ref.dtype)

    cn = _rms(take(COL_MLA, KV_LORA)) * kvn_ref[...]
    cb = cn.astype(BF16)
    kfull = _dot(cb, wk_ref[...])
    vt = _dot(cb, wv_ref[...]).T
    q = _dot((_rms(take(COL_MLA + KV_LORA + LANES, Q_LORA)) * qn_ref[...]).astype(BF16), wq_ref[...])
    for c0 in tiles[2:]:
        r[c0] = _dot(h, w_ref[:, c0:c0 + tn])

    for c0 in tiles:
        seg_ref[:, c0:c0 + tn] = r[c0].astype(seg_ref.dtype)
    for i in range(4):
        put(COL_RET + LANES * i, rot64(take(COL_RET + LANES * i)))
    for hd in range(H_DIFF):
        put(COL_DIFF + dv * hd, rot64(take(COL_DIFF + dv * hd)) * LOG2_E)
        k = rot64(take(COL_DIFF + 512 + dv * hd))
        put(COL_DIFF + 512 + dv * hd, k)
        kd_ref[0, hd, 0] = (k * (DIFF_HD ** -0.5)).astype(BF16)
        vd_ref[0, hd, 0, :dv, :] = take(COL_DIFF + 1024 + dv * hd).T.astype(BF16)
        vd_ref[0, hd, 0, dv:, :] = ones
    kp = rot_mla(take(COL_MLA + KV_LORA))
    put(COL_MLA + KV_LORA, kp)
    if ctx_refs:
        dk_ref, dv_ref, ckv_ref, kpe_ref = ctx_refs
        dk_ref[0, 0] = take(COL_DIFF + 512, 512)
        dv_ref[0, 0] = take(COL_DIFF + 1024, 512)
        ckv_ref[0, 0] = cn
        kpe_ref[0, 0] = kp[:, MLA_NOPE:MLA_NOPE + MLA_ROPE]
    for hm in range(H_MLA):
        cols = slice(LANES * hm, LANES * (hm + 1))
        km_ref[0, hm, 0] = (kfull[:, cols] + kp).astype(BF16)
        vm_ref[0, hm, 0, :MLA_V, :] = vt[MLA_V * hm:MLA_V * (hm + 1)].astype(BF16)
        vm_ref[0, hm, 0, MLA_V:, :] = ones
        qm_ref[:, cols] = (rot_mla(q[:, cols]) * (MLA_SCALE * LOG2_E)).astype(BF16)


def _input_proj(x, b, s, tk, g, mod, modsel, lw, qn, kvn, rope, ctx):
    t = b * s
    n_new = s // tk
    tn = PROJ_TILE_N
    assert COL_DIFF % tn == 0 and N_PROJ - COL_DIFF == tn and s % tk == 0
    mod_row = _mod_row(modsel, tk)
    resident = lambda a: pl.BlockSpec(a.shape, lambda i: (0,) * a.ndim, pipeline_mode=pl.Buffered(1))
    qn2, kvn2 = qn.reshape(1, Q_LORA), kvn.reshape(1, KV_LORA)
    in_specs = [pl.BlockSpec((tk, D_MODEL), lambda i: (i, 0)),
                pl.BlockSpec((1, D_MODEL), lambda i: (0, 0)),
                pl.BlockSpec((1, 8, D_MODEL), lambda i: (mod_row(i), 0, 0)),
                resident(lw["proj"]), resident(qn2), resident(kvn2),
                resident(lw["wq"]), resident(lw["wk"]), resident(lw["wv"])]
    args = [x, g.reshape(1, D_MODEL), mod, lw["proj"], qn2, kvn2, lw["wq"], lw["wk"], lw["wv"]]
    if rope is not None:
        in_specs += [pl.BlockSpec((tk, LANES), lambda i: (i % n_new, 0))] * 6
        args += list(rope[0]) + list(rope[1])
    aliases = {}
    n_ctx_in = 0
    if ctx is not None and ctx[2] is not None:
        n_ctx_in = len(ctx[2])
        aliases = {len(args) + k: 6 + k for k in range(n_ctx_in)}
        in_specs += [pl.BlockSpec(memory_space=pl.ANY)] * n_ctx_in
        args += list(ctx[2])
    tile = lambda i: (i // n_new, 0, i % n_new, 0, 0)
    out_shape = [jax.ShapeDtypeStruct((t, N_PROJ), BF16),
                 jax.ShapeDtypeStruct((t, H_MLA * LANES), BF16),
                 jax.ShapeDtypeStruct((b, H_DIFF, n_new, tk, LANES), BF16),
                 jax.ShapeDtypeStruct((b, H_DIFF, n_new, 2 * DIFF_HD + SUM_ROWS, tk), BF16),
                 jax.ShapeDtypeStruct((b, H_MLA, n_new, tk, LANES), BF16),
                 jax.ShapeDtypeStruct((b, H_MLA, n_new, MLA_V + SUM_ROWS, tk), BF16)]
    out_specs = [pl.BlockSpec((tk, N_PROJ), lambda i: (i, 0)),
                 pl.BlockSpec((tk, H_MLA * LANES), lambda i: (i, 0)),
                 pl.BlockSpec((1, H_DIFF, 1, tk, LANES), tile),
                 pl.BlockSpec((1, H_DIFF, 1, 2 * DIFF_HD + SUM_ROWS, tk), tile),
                 pl.BlockSpec((1, H_MLA, 1, tk, LANES), tile),
                 pl.BlockSpec((1, H_MLA, 1, MLA_V + SUM_ROWS, tk), tile)]
    if ctx is not None:
        layer, depth, _ = ctx
        for width in (H_DIFF * 2 * DIFF_HD, H_DIFF * 2 * DIFF_HD, KV_LORA, MLA_ROPE):
            out_shape.append(jax.ShapeDtypeStruct((b, depth, s, width), F32))
            out_specs.append(pl.BlockSpec((1, 1, tk, width), lambda i: (i // n_new, layer, i % n_new, 0)))
    outs = pl.pallas_call(
        functools.partial(_input_proj_body, has_rope=rope is not None, n_ctx_in=n_ctx_in, tn=tn),
        out_shape=tuple(out_shape),
        grid=(t // tk,),
        in_specs=in_specs,
        out_specs=tuple(out_specs),
        input_output_aliases=aliases,
        compiler_params=_cparams("parallel"),
        name="input_proj",
    )(*args)
    return tuple(outs[:6]), tuple(outs[6:])


def _ret_scan_body(gc_ref, kf_ref, vf_ref, kb_ref, vb_ref, zf_ref, zb_ref, s0f_ref, s0b_ref,
                   sf_ref, sb_ref, finf_ref, finb_ref, st_ref, *, n_steps, group):
    p = pl.program_id(1)
    t = pl.program_id(2)
    c = RET_CHUNK

    @pl.when(t == 0)
    def _():
        st_ref[0] = s0f_ref[0]
        st_ref[1] = s0b_ref[0]

    def local_terms(gi, k_ref, v_ref, z_ref):
        rows = slice(gi * c, (gi + 1) * c)
        k = k_ref[rows, :].astype(F32)
        kz_t =(k * (RET_DK ** -0.5) * z_ref[0]).T.astype(BF16)
        v = v_ref[rows, :].astype(BF16)
        return [_dot(kz_t[RET_DK * j:RET_DK * (j + 1)], v[:, RET_DV * j:RET_DV * (j + 1)]) for j in range(2)]

    loc_f = [local_terms(gi, kf_ref, vf_ref, zf_ref) for gi in range(group)]
    loc_b = [local_terms(gi, kb_ref, vb_ref, zb_ref) for gi in range(group)]
    ascending = list(range(group))
    for d, loc, out_ref, order in ((0, loc_f, sf_ref, ascending), (1, loc_b, sb_ref, ascending[::-1])):
        for j in range(2):
            decay = gc_ref[2 * p + j, d]
            st = st_ref[d, j]
            for gi in order:
                out_ref[0, j, gi] = st
                st = decay * st + loc[gi][j]
            st_ref[d, j] = st

    @pl.when(t == n_steps - 1)
    def _():
        finf_ref[0] = st_ref[0]
        finb_ref[0] = st_ref[1]


def _ret_out_body(q_ref, k_ref, v_ref, g_ref, sf_ref, sb_ref, dm_ref, xif_ref, xib_ref, y_ref, *, group):
    c = RET_CHUNK
    lane = lax.broadcasted_iota(jnp.int32, (c, 2 * RET_DK), 1)
    stage1 = []
    for gi in range(group):
        rows = slice(gi * c, (gi + 1) * c)
        q = q_ref[rows, :].astype(F32)
        k = k_ref[rows, :].astype(F32)
        kb =(k * (RET_DK ** -0.5)).astype(BF16)
        s_f = sf_ref[0, :, gi].reshape(2 * RET_DK, RET_DV).astype(BF16)
        s_b = sb_ref[0, :, gi].reshape(2 * RET_DK, RET_DV).astype(BF16)
        for j in range(2):
            in_head = (lane >= RET_DK * j) & (lane < RET_DK * (j + 1))
            qm = jnp.where(in_head, q, 0.0).astype(BF16)
            sc = (_dot_nt(qm, kb) * dm_ref[j]).astype(BF16)
            cross = _dot(qm, s_f) * xif_ref[j] + _dot(qm, s_b) * xib_ref[j]
            stage1.append((rows, j, sc, cross))
    for rows, j, sc, cross in stage1:
        cols = slice(RET_DV * j, RET_DV * (j + 1))
        o = _dot(sc, v_ref[rows, cols].astype(BF16)) + cross
        mu = jnp.mean(o, axis=-1, keepdims=True)
        oc = o - mu
        on = oc * lax.rsqrt(jnp.mean(oc * oc, axis=-1, keepdims=True) + EPS)
        y_ref[rows, cols] = (_silu(g_ref[rows, cols].astype(F32)) * on).astype(y_ref.dtype)


def _retention(seg, b, s, s0f, s0b, tabs):
    c = RET_CHUNK
    n = s // c
    group = math.gcd(n, RET_GROUP)
    ns = n // group
    rows = group * c
    gc, zf, zb, dmat, xif, xib = tabs
    rq0, rk0 = COL_RET // 128, (COL_RET + 256) // 128
    rv0, rg0 = (COL_RET + 512) // 256, (COL_RET + 1024) // 256
    state_shape = jax.ShapeDtypeStruct((b, H_RET, n, RET_DK, RET_DV), F32)
    fin_shape = jax.ShapeDtypeStruct((b, H_RET, RET_DK, RET_DV), F32)

    def fwd(bi, t):
        return bi * ns + t

    def bwd(bi, t):
        return bi * ns + (ns - 1 - t)

    in_specs = [
        pl.BlockSpec(memory_space=pltpu.SMEM),
        pl.BlockSpec((rows, 128), lambda bi, p, t: (fwd(bi, t), rk0 + p)),
        pl.BlockSpec((rows, 256), lambda bi, p, t: (fwd(bi, t), rv0 + p)),
        pl.BlockSpec((rows, 128), lambda bi, p, t: (bwd(bi, t), rk0 + p)),
        pl.BlockSpec((rows, 256), lambda bi, p, t: (bwd(bi, t), rv0 + p)),
        pl.BlockSpec((1, c, 128), lambda bi, p, t: (p, 0, 0)),
        pl.BlockSpec((1, c, 128), lambda bi, p, t: (p, 0, 0)),
        pl.BlockSpec((1, 2, RET_DK, RET_DV), lambda bi, p, t: (bi, p, 0, 0)),
        pl.BlockSpec((1, 2, RET_DK, RET_DV), lambda bi, p, t: (bi, p, 0, 0)),
    ]
    args = [gc, seg, seg, seg, seg, zf, zb, s0f, s0b]
    sf, sb, fin_f, fin_b = pl.pallas_call(
        functools.partial(_ret_scan_body, n_steps=ns, group=group),
        out_shape=(state_shape, state_shape, fin_shape, fin_shape),
        grid=(b, 2, ns),
        in_specs=in_specs,
        out_specs=(pl.BlockSpec((1, 2, group, RET_DK, RET_DV), lambda bi, p, t: (bi, p, t, 0, 0)),
                   pl.BlockSpec((1, 2, group, RET_DK, RET_DV), lambda bi, p, t: (bi, p, ns - 1 - t, 0, 0)),
                   pl.BlockSpec((1, 2, RET_DK, RET_DV), lambda bi, p, t: (bi, p, 0, 0)),
                   pl.BlockSpec((1, 2, RET_DK, RET_DV), lambda bi, p, t: (bi, p, 0, 0))),
        scratch_shapes=[pltpu.VMEM((2, 2, RET_DK, RET_DV), F32)],
        compiler_params=_cparams("parallel", "parallel", "arbitrary"),
        name="ret_scan",
    )(*args)

    in_specs = [
        pl.BlockSpec((rows, 128), lambda bi, p, t: (fwd(bi, t), rq0 + p)),
        pl.BlockSpec((rows, 128), lambda bi, p, t: (fwd(bi, t), rk0 + p)),
        pl.BlockSpec((rows, 256), lambda bi, p, t: (fwd(bi, t), rv0 + p)),
        pl.BlockSpec((rows, 256), lambda bi, p, t: (fwd(bi, t), rg0 + p)),
        pl.BlockSpec((1, 2, group, RET_DK, RET_DV), lambda bi, p, t: (bi, p, t, 0, 0)),
        pl.BlockSpec((1, 2, group, RET_DK, RET_DV), lambda bi, p, t: (bi, p, t, 0, 0)),
        pl.BlockSpec((2, c, c), lambda bi, p, t: (p, 0, 0)),
        pl.BlockSpec((2, c, 128), lambda bi, p, t: (p, 0, 0)),
        pl.BlockSpec((2, c, 128), lambda bi, p, t: (p, 0, 0)),
    ]
    args = [seg, seg, seg, seg, sf, sb, dmat, xif, xib]
    y = pl.pallas_call(
        functools.partial(_ret_out_body, group=group),
        out_shape=jax.ShapeDtypeStruct((b * s, H_RET * RET_DV), BF16),
        grid=(b, 2, ns),
        in_specs=in_specs,
        out_specs=pl.BlockSpec((rows, 256), lambda bi, p, t: (fwd(bi, t), p)),
        compiler_params=_cparams("parallel", "parallel", "parallel"),
        name="ret_out",
    )(*args)
    return y, fin_f, fin_b


def _retention_tables(decay_f, decay_b):
    c = RET_CHUNK
    lg_f = jax.nn.log_sigmoid(decay_f.astype(F32))
    lg_b = jax.nn.log_sigmoid(decay_b.astype(F32))
    i = jnp.arange(c, dtype=F32)
    dist = i[:, None] - i[None, :]
    d_f = jnp.where(dist >= 0, jnp.exp(jnp.maximum(dist, 0.0)[None] * lg_f[:, None, None]), 0.0)
    d_b = jnp.where(dist < 0, jnp.exp(jnp.maximum(-dist, 0.0)[None] * lg_b[:, None, None]), 0.0)
    dmat = d_f + d_b
    xi_f = jnp.exp((i + 1.0)[None, :] * lg_f[:, None])
    xi_b = jnp.exp((c - i)[None, :] * lg_b[:, None])
    zeta_f = jnp.exp((c - 1.0 - i)[None, :] * lg_f[:, None])
    zeta_b = jnp.exp(i[None, :] * lg_b[:, None])
    gc = jnp.stack([jnp.exp(c * lg_f), jnp.exp(c * lg_b)], axis=1)

    def lanes(tab, width):
        return jnp.broadcast_to(tab[:, :, None], tab.shape + (width,))

    def pair(tab):
        t64 = lanes(tab, RET_DK).reshape(H_RET // 2, 2, c, RET_DK)
        return jnp.concatenate([t64[:, 0], t64[:, 1]], axis=-1)

    return gc, pair(zeta_f), pair(zeta_b), dmat, lanes(xi_f, RET_DV), lanes(xi_b, RET_DV)


def _diff_cache_body(ck_ref, cv_ref, kt_ref, vt_ref):
    dv = 2 * DIFF_HD
    ones = jnp.ones((SUM_ROWS, vt_ref.shape[-1]), BF16)
    for h in range(H_DIFF):
        cols = slice(dv * h, dv * (h + 1))
        kt_ref[0, h, 0] = (ck_ref[0, 0, :, cols] * (DIFF_HD ** -0.5)).astype(BF16)
        vt_ref[0, h, 0, :dv, :] = cv_ref[0, 0, :, cols].T.astype(BF16)
        vt_ref[0, h, 0, dv:, :] = ones


def _diff_cache_prep(cache_k, cache_v, layer, tk):
    b, _, past, width = cache_k.shape
    n = past // tk
    spec = pl.BlockSpec((1, 1, tk, width), lambda bi, c: (bi, layer, c, 0))
    return pl.pallas_call(
        _diff_cache_body,
        out_shape=(jax.ShapeDtypeStruct((b, H_DIFF, n, tk, LANES), BF16),
                   jax.ShapeDtypeStruct((b, H_DIFF, n, 2 * DIFF_HD + SUM_ROWS, tk), BF16)),
        grid=(b, n),
        in_specs=[spec, spec],
        out_specs=(pl.BlockSpec((1, H_DIFF, 1, tk, LANES), lambda bi, c: (bi, 0, c, 0, 0)),
                   pl.BlockSpec((1, H_DIFF, 1, 2 * DIFF_HD + SUM_ROWS, tk), lambda bi, c: (bi, 0, c, 0, 0))),
        compiler_params=_cparams("parallel", "parallel"),
        name="diff_cache_prep",
    )(cache_k, cache_v)


def _mla_cache_body(cc_ref, cp_ref, wk_ref, wv_ref, kt_ref, vt_ref):
    cb = cc_ref[0, 0].astype(BF16)
    kfull = _dot(cb, wk_ref[...])
    vt = _dot(cb, wv_ref[...]).T
    kp = cp_ref[0, 0]
    ones = jnp.ones((SUM_ROWS, vt.shape[-1]), BF16)
    for h in range(H_MLA):
        kt_ref[0, h, 0] = (kfull[:, LANES * h:LANES * (h + 1)] + kp).astype(BF16)
        vt_ref[0, h, 0, :MLA_V, :] = vt[MLA_V * h:MLA_V * (h + 1)].astype(BF16)
        vt_ref[0, h, 0, MLA_V:, :] = ones


def _mla_cache_prep(cache_ckv, cache_kpe, layer, wk, wv, tk):
    b, _, past, _ = cache_ckv.shape
    n = past // tk
    return pl.pallas_call(
        _mla_cache_body,
        out_shape=(jax.ShapeDtypeStruct((b, H_MLA, n, tk, LANES), BF16),
                   jax.ShapeDtypeStruct((b, H_MLA, n, MLA_V + SUM_ROWS, tk), BF16)),
        grid=(b, n),
        in_specs=[pl.BlockSpec((1, 1, tk, KV_LORA), lambda bi, c: (bi, layer, c, 0)),
                  pl.BlockSpec((1, 1, tk, LANES), lambda bi, c: (bi, layer, c, 0)),
                  pl.BlockSpec(wk.shape, lambda bi, c: (0, 0)),
                  pl.BlockSpec(wv.shape, lambda bi, c: (0, 0))],
        out_specs=(pl.BlockSpec((1, H_MLA, 1, tk, LANES), lambda bi, c: (bi, 0, c, 0, 0)),
                   pl.BlockSpec((1, H_MLA, 1, MLA_V + SUM_ROWS, tk), lambda bi, c: (bi, 0, c, 0, 0))),
        compiler_params=_cparams("parallel", "parallel"),
        name="mla_cache_prep",
    )(cache_ckv, cache_kpe, wk, wv)


def _flash_keymajor(chains, nk, dv):
    m, acc, s_next = [], [], []
    for q_t, k_tile, _ in chains:
        n = q_t.shape[1]
        m.append(jnp.full((1, n), NEG_BIG, F32))
        acc.append(jnp.zeros((dv + SUM_ROWS, n), F32))
        s_next.append(_dot(k_tile(0), q_t))
    for i in range(nk):
        for ci, (q_t, k_tile, vt_tile) in enumerate(chains):
            s = s_next[ci]
            if i + 1 < nk:
                s_next[ci] = _dot(k_tile(i + 1), q_t)
            m_new = jnp.maximum(m[ci], jnp.max(s, axis=0, keepdims=True))
            alpha = jnp.exp2(m[ci] - m_new)
            p = jnp.exp2(s - m_new)
            acc[ci] = alpha * acc[ci] + _dot(vt_tile(i), p.astype(BF16))
            m[ci] = m_new
    return [(a[dv:dv + 1], a[:dv]) for a in acc]


def _key_tiles(kv_refs, counts, head):
    index = [(part, i) for part, n in enumerate(counts) for i in range(n)]
    k_tile = lambda i: kv_refs[2 * index[i][0]][0, head, index[i][1]]
    vt_tile = lambda i: kv_refs[2 * index[i][0] + 1][0, head, index[i][1]]
    return k_tile, vt_tile


def _kv_specs(parts, heads, dv, head_block):
    specs, args = [], []
    for kt, vt in parts:
        n, tk = kt.shape[2], kt.shape[3]
        specs += [pl.BlockSpec((1, heads, n, tk, LANES), lambda bi, h, qi: (bi, head_block(h), 0, 0, 0)),
                  pl.BlockSpec((1, heads, n, dv + SUM_ROWS, tk), lambda bi, h, qi: (bi, head_block(h), 0, 0, 0))]
        args += [kt, vt]
    return specs, args


def _diff_attn_body(q_ref, dl_ref, *refs, counts, tq, lam_init):
    kv_refs, y_ref = refs[:-1], refs[-1]
    k_tile, vt_tile = _key_tiles(kv_refs, counts, 0)
    cq = min(tq, DIFF_CHAIN_Q)
    lane = lax.broadcasted_iota(jnp.int32, (cq, 2 * DIFF_HD), 1)
    chains = []
    for r in range(0, tq, cq):
        q = q_ref[r:r + cq, :].astype(F32)
        q1 = jnp.where(lane < DIFF_HD, q, 0.0)
        q2 = jnp.where(lane >= DIFF_HD, q, 0.0)
        q_t = jnp.concatenate([q1, q2], axis=0).T.astype(BF16)
        chains.append((q_t, k_tile, vt_tile))
    results = _flash_keymajor(chains, sum(counts), 2 * DIFF_HD)
    dl = dl_ref[...]
    lam = (jnp.exp(jnp.sum(dl[0:1] * dl[1:2], axis=-1, keepdims=True))
           - jnp.exp(jnp.sum(dl[2:3] * dl[3:4], axis=-1, keepdims=True)) + lam_init)
    for ci, (l, acc) in enumerate(results):
        o = acc * (1.0 / l)
        o = o[:, :cq] - lam * o[:, cq:]
        o = o * lax.rsqrt(jnp.mean(o * o, axis=0, keepdims=True) + EPS) * (1.0 - lam_init)
        y_ref[ci * cq:(ci + 1) * cq, :] = o.T.astype(y_ref.dtype)


def _diff_attention(seg, b, s, kv_parts, dl, lam_init):
    tq = min(DIFF_Q_TILE, s)
    nq = s // tq
    q0 = COL_DIFF // 128
    kv_specs, kv_args = _kv_specs(kv_parts, 1, 2 * DIFF_HD, lambda h: h)
    in_specs = [pl.BlockSpec((tq, 128), lambda bi, h, qi: (bi * nq + qi, q0 + h)),
                pl.BlockSpec((8, 128), lambda bi, h, qi: (0, 0))] + kv_specs
    args = [seg, dl] + kv_args
    return pl.pallas_call(
        functools.partial(_diff_attn_body, counts=tuple(kt.shape[2] for kt, _ in kv_parts), tq=tq,
                          lam_init=lam_init),
        out_shape=jax.ShapeDtypeStruct((b * s, H_DIFF * 2 * DIFF_HD), BF16),
        grid=(b, H_DIFF, nq),
        in_specs=in_specs,
        out_specs=pl.BlockSpec((tq, 128), lambda bi, h, qi: (bi * nq + qi, h)),
        compiler_params=_cparams("parallel", "parallel", "arbitrary"),
        name="diff_attn",
    )(*args)


def _mla_attn_body(q_ref, *refs, counts):
    kv_refs, y_ref = refs[:-1], refs[-1]
    tq = q_ref.shape[0]
    cq = min(tq, MLA_CHAIN_Q)
    chains = []
    for r in range(0, tq, cq):
        for j in range(2):
            q_t = q_ref[r:r + cq, LANES * j:LANES * (j + 1)].astype(F32).T.astype(BF16)
            chains.append((q_t,) + _key_tiles(kv_refs, counts, j))
    outs = [acc * (1.0 / l) for l, acc in _flash_keymajor(chains, sum(counts), MLA_V)]
    for ri, r in enumerate(range(0, tq, cq)):
        pair = jnp.concatenate(outs[2 * ri:2 * ri + 2], axis=0)
        y_ref[r:r + cq, :] = pair.T.astype(y_ref.dtype)


def _mla_attention(q_all, b, s, kv_parts):
    tq = min(MLA_Q_TILE, s)
    nq = s // tq
    kv_specs, kv_args = _kv_specs(kv_parts, 2, MLA_V, lambda p: p)
    in_specs = [pl.BlockSpec((tq, 2 * LANES), lambda bi, p, qi: (bi * nq + qi, p))] + kv_specs
    args = [q_all] + kv_args
    return pl.pallas_call(
        functools.partial(_mla_attn_body, counts=tuple(kt.shape[2] for kt, _ in kv_parts)),
        out_shape=jax.ShapeDtypeStruct((b * s, H_MLA * MLA_V), BF16),
        grid=(b, H_MLA // 2, nq),
        in_specs=in_specs,
        out_specs=pl.BlockSpec((tq, 2 * MLA_V), lambda bi, p, qi: (bi * nq + qi, p)),
        compiler_params=_cparams("parallel", "parallel", "arbitrary"),
        name="mla_attn",
    )(*args)


def _merge_body(x_ref, yr_ref, yd_ref, ym_ref, gt_ref, mod_ref, wbr_ref, wo_ref, o_ref):
    merged = None
    for gi, y_ref in enumerate((yr_ref, yd_ref, ym_ref)):
        br = _dot(y_ref[...], wbr_ref[gi])
        term = _sigmoid(gt_ref[:, D_MODEL * gi:D_MODEL * (gi + 1)].astype(F32)) * br
        merged = term if merged is None else merged + term
    out = _dot(merged.astype(BF16), wo_ref[...])
    o_ref[...] = x_ref[...] + mod_ref[0][2:3, :] * out


def _merge(x, yr, yd, ym, seg, mod, modsel, wbr, wo):
    t = x.shape[0]
    tm = min(TOKEN_TILE, t)
    mod_row = _mod_row(modsel, tm)
    row = lambda i: (i, 0)
    return pl.pallas_call(
        _merge_body,
        out_shape=jax.ShapeDtypeStruct((t, D_MODEL), F32),
        grid=(t // tm,),
        in_specs=[pl.BlockSpec((tm, D_MODEL), row),
                  pl.BlockSpec((tm, 512), row), pl.BlockSpec((tm, 512), row), pl.BlockSpec((tm, 512), row),
                  pl.BlockSpec((tm, 3 * D_MODEL), lambda i: (i, COL_GATE // (3 * D_MODEL))),
                  pl.BlockSpec((1, 8, D_MODEL), lambda i: (mod_row(i), 0, 0)),
                  pl.BlockSpec(wbr.shape, lambda i: (0, 0, 0)),
                  pl.BlockSpec(wo.shape, lambda i: (0, 0))],
        out_specs=pl.BlockSpec((tm, D_MODEL), row),
        compiler_params=_cparams("parallel"),
        name="merge",
    )(x, yr, yd, ym, seg, mod, wbr, wo)


def _ffn_body(*refs, final):
    if final:
        x_ref, g_ref, mod_ref, wa_ref, wb_ref, wo_ref, fg_ref, o_ref = refs
    else:
        x_ref, g_ref, mod_ref, wa_ref, wb_ref, wo_ref, o_ref = refs
    m = mod_ref[0]
    x = x_ref[...]
    h = ((_rms(x) * g_ref[...]) * (1.0 + m[4:5, :]) + m[3:4, :]).astype(BF16)
    d_ff = wa_ref.shape[1]
    chunks = [(lo, min(lo + FFN_CHUNK, d_ff)) for lo in range(0, d_ff, FFN_CHUNK)]

    def up(k):
        lo, hi = chunks[k]
        return _dot(h, wa_ref[:, lo:hi]), _dot(h, wb_ref[:, lo:hi])

    nxt = up(0)
    y = None
    for k, (lo, hi) in enumerate(chunks):
        a, b = nxt
        if k + 1 < len(chunks):
            nxt = up(k + 1)
        d = _dot((_silu(a) * b).astype(BF16), wo_ref[lo:hi, :])
        y = d if y is None else y + d
    xn = x + m[5:6, :] * y
    if final:
        xn = _rms(xn) * fg_ref[...]
    o_ref[...] = xn


def _ffn(x, g, mod, modsel, wa, wb, wo, final_g):
    t = x.shape[0]
    tm = min(TOKEN_TILE, t)
    mod_row = _mod_row(modsel, tm)
    final = final_g is not None
    resident = lambda a: pl.BlockSpec(a.shape, lambda i: (0, 0), pipeline_mode=pl.Buffered(1))
    in_specs = [pl.BlockSpec((tm, D_MODEL), lambda i: (i, 0)),
                pl.BlockSpec((1, D_MODEL), lambda i: (0, 0)),
                pl.BlockSpec((1, 8, D_MODEL), lambda i: (mod_row(i), 0, 0)),
                resident(wa), resident(wb), resident(wo)]
    args = [x, g.reshape(1, D_MODEL), mod, wa, wb, wo]
    if final:
        in_specs.append(pl.BlockSpec((1, D_MODEL), lambda i: (0, 0)))
        args.append(final_g.reshape(1, D_MODEL))
    return pl.pallas_call(
        functools.partial(_ffn_body, final=final),
        out_shape=jax.ShapeDtypeStruct((t, D_MODEL), F32),
        grid=(t // tm,),
        in_specs=in_specs,
        out_specs=pl.BlockSpec((tm, D_MODEL), lambda i: (i, 0)),
        compiler_params=_cparams("parallel"),
        name="ffn",
    )(*args)


def _axial_angles(n_tokens, rot_dim):
    t = jnp.arange(n_tokens)
    row = (t // GRID_W).astype(F32)
    col = (t % GRID_W).astype(F32)
    nf = rot_dim // 4
    inv = ROPE_BASE ** (-jnp.arange(nf, dtype=F32) / nf)
    ang = jnp.concatenate([row[:, None] * inv, col[:, None] * inv], axis=-1)
    return jnp.cos(ang), jnp.sin(ang)


def _rope_tables_head64(n_tokens):
    cos, sin = _axial_angles(n_tokens, 64)
    zero = jnp.zeros_like(sin)
    c = jnp.tile(jnp.concatenate([cos, cos], axis=-1), (1, 2))
    sa = jnp.tile(jnp.concatenate([-sin, zero], axis=-1), (1, 2))
    sb = jnp.tile(jnp.concatenate([zero, sin], axis=-1), (1, 2))
    return c, sa, sb


def _rope_tables_mla(n_tokens):
    cos, sin = _axial_angles(n_tokens, MLA_ROPE)
    z16 = jnp.zeros_like(sin)
    one64 = jnp.ones((n_tokens, MLA_NOPE), F32)
    z64 = jnp.zeros((n_tokens, MLA_NOPE), F32)
    z32 = jnp.zeros((n_tokens, 32), F32)
    c = jnp.concatenate([one64, cos, cos, z32], axis=-1)
    sa = jnp.concatenate([z64, -sin, z16, z32], axis=-1)
    sb = jnp.concatenate([z64, z16, sin, z32], axis=-1)
    return c, sa, sb


def _layer_weights(l, w_in, w_uq, w_ukv, w_branch, w_out, w_ffn_in, w_ffn_out):
    wi = w_in[l]
    d = wi.shape[0]
    w_ret = wi[:, 0:1536]
    w_diff = wi[:, 1536:3072]
    w_mla = jnp.concatenate([wi[:, 3456:3712], jnp.zeros((d, 64), wi.dtype), wi[:, 3712:3744],
                             jnp.zeros((d, 32), wi.dtype), wi[:, 3072:3456]], axis=1)
    w_gate = wi[:, 3744:]
    w_proj = jnp.concatenate([w_gate, w_ret, w_diff, w_mla], axis=1)
    assert w_proj.shape[1] == N_PROJ
    hd = MLA_NOPE + MLA_ROPE
    wq = jnp.pad(w_uq[l].reshape(Q_LORA, H_MLA, hd), ((0, 0), (0, 0), (0, LANES - hd))).reshape(Q_LORA, H_MLA * LANES)
    wkv = w_ukv[l].reshape(KV_LORA, H_MLA, MLA_NOPE + MLA_V)
    wk = jnp.pad(wkv[:, :, :MLA_NOPE], ((0, 0), (0, 0), (0, LANES - MLA_NOPE))).reshape(KV_LORA, H_MLA * LANES)
    wv = wkv[:, :, MLA_NOPE:].reshape(KV_LORA, H_MLA * MLA_V)
    wa, wb = w_ffn_in[l][:, :D_FF], w_ffn_in[l][:, D_FF:]
    cast = lambda a: a.astype(BF16)
    return dict(proj=cast(w_proj), wq=cast(wq), wk=cast(wk), wv=cast(wv), wbr=cast(w_branch[l]), wo=cast(w_out[l]),
                wa=cast(wa), wb=cast(wb), wf=cast(w_ffn_out[l]))


def _mixer_layer(x, b, s, l, lw, mod, modsel, norm1_g, norm2_g, tabs, mla_q_norm, mla_kv_norm, dl, lam_init,
                 s0f, s0b, caches, ropes, ctx, final_g):
    tk = min(KEY_TILE, s)
    (seg, q_mla, kd, vd, km, vm), ctx_out = _input_proj(x, b, s, tk, norm1_g, mod, modsel, lw, mla_q_norm,
                                                       mla_kv_norm, ropes, ctx)

    y_ret, fin_f, fin_b = _retention(seg, b, s, s0f, s0b, tabs)

    diff_parts, mla_parts = [(kd, vd)], [(km, vm)]
    if caches is not None:
        cache_k, cache_v, cache_ckv, cache_kpe = caches
        diff_parts.insert(0, _diff_cache_prep(cache_k, cache_v, l, tk))
        mla_parts.insert(0, _mla_cache_prep(cache_ckv, cache_kpe, l, lw["wk"], lw["wv"], tk))
    y_diff = _diff_attention(seg, b, s, diff_parts, dl, lam_init)
    y_mla = _mla_attention(q_mla, b, s, mla_parts)

    x = _merge(x, y_ret, y_diff, y_mla, seg, mod, modsel, lw["wbr"], lw["wo"])
    x = _ffn(x, norm2_g, mod, modsel, lw["wa"], lw["wb"], lw["wf"], final_g)
    return x, fin_f, fin_b, ctx_out


def kernel(x_prompt, x_sample, state_ret_fwd, state_ret_bwd, cache_diff_k, cache_diff_v, cache_mla_ckv,
           cache_mla_kpe, c, c_ctx, norm1_g, norm2_g, w_ada, b_ada, w_in, ret_decay_fwd, ret_decay_bwd,
           diff_lambda, mla_q_norm, mla_kv_norm, w_uq, w_ukv, w_branch, w_out, w_ffn_in, w_ffn_out, final_g):
    bp, sp, d = x_prompt.shape
    bs, ss, _ = x_sample.shape
    depth = w_in.shape[0]
    past = cache_diff_k.shape[2]

    n_rows = -(-(1 + bs) // 8) * 8
    cond = jnp.zeros((n_rows, d), F32).at[0].set(c_ctx).at[1:1 + bs].set(c)
    mod_all = _modulation(cond, w_ada, b_ada)
    mod_all = jnp.pad(mod_all.reshape(depth, n_rows, 6, d), ((0, 0), (0, 0), (0, 2), (0, 0)))

    sel_prompt = (0, bp * sp)
    sel_sample = (1, ss)

    ropes = (_rope_tables_head64(ss), _rope_tables_mla(ss))
    cache_k = cache_diff_k.reshape(bs, depth, past, H_DIFF * 2 * DIFF_HD)
    cache_v = cache_diff_v.reshape(bs, depth, past, H_DIFF * 2 * DIFF_HD)
    cache_kpe = jnp.pad(cache_mla_kpe, ((0, 0), (0, 0), (0, 0), (MLA_NOPE, LANES - MLA_NOPE - MLA_ROPE)))
    zero_state = jnp.zeros((bp, H_RET, RET_DK, RET_DV), F32)

    xp = x_prompt.reshape(bp * sp, d)
    xs = x_sample.reshape(bs * ss, d)
    ret_f, ret_b, ctx_arrays = [], [], None
    for l in range(depth):
        lw = _layer_weights(l, w_in, w_uq, w_ukv, w_branch, w_out, w_ffn_in, w_ffn_out)
        lam_init = 0.8 - 0.6 * math.exp(-0.3 * l)
        tabs = _retention_tables(ret_decay_fwd[l], ret_decay_bwd[l])
        dl = jnp.pad(diff_lambda[l], ((0, 4), (0, LANES - DIFF_HD)))
        fg = final_g if l == depth - 1 else None
        common = (norm1_g[l], norm2_g[l], tabs, mla_q_norm[l], mla_kv_norm[l], dl, lam_init)

        xp, fin_f, fin_b, ctx_arrays = _mixer_layer(xp, bp, sp, l, lw, mod_all[l], sel_prompt, *common, zero_state,
                                                    zero_state, None, None, (l, depth, ctx_arrays), fg)
        ret_f.append(fin_f)
        ret_b.append(fin_b)

        xs, _, _, _ = _mixer_layer(xs, bs, ss, l, lw, mod_all[l], sel_sample, *common, state_ret_fwd[:, l],
                                   state_ret_bwd[:, l], (cache_k, cache_v, cache_mla_ckv, cache_kpe), ropes, None, fg)

    dk_all, dv_all, ckv_all, kpe_all = ctx_arrays
    head_shape = (bp, depth, sp, H_DIFF, 2 * DIFF_HD)
    return (xp.reshape(bp, sp, d), xs.reshape(bs, ss, d), jnp.stack(ret_f, axis=1), jnp.stack(ret_b, axis=1),
            dk_all.reshape(head_shape), dv_all.reshape(head_shape), ckv_all, kpe_all)
```

```python
import functools
import math

import jax
import jax.numpy as jnp
from jax import lax
from jax.experimental import pallas as pl
from jax.experimental.pallas import tpu as pltpu

F32 = jnp.float32
BF16 = jnp.bfloat16

D_MODEL = 1024
GRID_W = 64
H_RET, RET_DK, RET_DV, RET_CHUNK = 4, 64, 128, 128
H_DIFF, DIFF_HD = 4, 64
H_MLA, MLA_NOPE, MLA_ROPE, MLA_V = 8, 64, 32, 64
Q_LORA, KV_LORA = 384, 256
D_FF = 2816
ROPE_BASE = 10000.0
EPS = 1e-6
MLA_SCALE = (MLA_NOPE + MLA_ROPE) ** -0.5
LANES = 128
NEG_BIG = -1e30
LOG2_E = math.log2(math.e)

TOKEN_TILE = 512
FFN_CHUNK = 1024
DIFF_Q_TILE = 2048
DIFF_CHAIN_Q = 256
MLA_Q_TILE = 2048
MLA_CHAIN_Q = 512
ATTN_CHAINS = 8
KEY_TILE = 512
RET_GROUP = 8
SUM_ROWS = 16
PROJ_TILE_N = 2304

N_PROJ = 6912
COL_GATE = 0
COL_RET = 3072
COL_DIFF = 4608
COL_MLA = 6144
VMEM_LIMIT = 56 * 1024 * 1024


def _cparams(*sem):
    return pltpu.CompilerParams(dimension_semantics=sem, vmem_limit_bytes=VMEM_LIMIT)


def _mod_row(modsel, tm):
    base, tokens_per_row = modsel
    assert tokens_per_row % tm == 0
    return lambda i: base + (i * tm) // tokens_per_row


def _dot(a, b):
    return jnp.dot(a, b, preferred_element_type=F32)


def _dot_nt(a, b):
    return lax.dot_general(a, b, (((1,), (1,)), ((), ())), preferred_element_type=F32)


def _rms(x):
    return x * lax.rsqrt(jnp.mean(x * x, axis=-1, keepdims=True) + EPS)


def _silu(x):
    return x * (1.0 / (1.0 + jnp.exp(-x)))


def _sigmoid(x):
    return 1.0 / (1.0 + jnp.exp(-x))


def _rope(x, c_ref, sa_ref, sb_ref, shift_a, shift_b, rows=slice(None)):
    return (x * c_ref[rows, :] + pltpu.roll(x, shift_a, 1) * sa_ref[rows, :]
            + pltpu.roll(x, shift_b, 1) * sb_ref[rows, :])


def _mod_body(c_ref, w_ref, b_ref, o_ref):
    o_ref[0] = _dot(_silu(c_ref[...]).astype(BF16), w_ref[0].astype(BF16)) + b_ref[0]


def _modulation(cond, w_ada, b_ada):
    depth, d, n = w_ada.shape
    rows = cond.shape[0]
    tn = 1536
    return pl.pallas_call(
        _mod_body,
        out_shape=jax.ShapeDtypeStruct((depth, rows, n), F32),
        grid=(depth, n // tn),
        in_specs=[pl.BlockSpec((rows, d), lambda l, j: (0, 0)),
                  pl.BlockSpec((1, d, tn), lambda l, j: (l, 0, j)),
                  pl.BlockSpec((1, 1, tn), lambda l, j: (l, 0, j))],
        out_specs=pl.BlockSpec((1, rows, tn), lambda l, j: (l, 0, j)),
        compiler_params=_cparams("parallel", "parallel"),
        name="adaln_mod",
    )(cond, w_ada, b_ada.reshape(depth, 1, n))


def _input_proj_body(*refs, has_rope, n_ctx_in, tn):
    refs = list(refs)
    x_ref, g_ref, mod_ref, w_ref, qn_ref, kvn_ref, wq_ref, wk_ref, wv_ref = refs[:9]
    refs = refs[9:]
    if has_rope:
        rope64, rope_mla = refs[:3], refs[3:6]
        refs = refs[6:]
    refs = refs[n_ctx_in:]
    seg_ref, qm_ref, kd_ref, vd_ref, km_ref, vm_ref = refs[:6]
    ctx_refs = refs[6:]

    def rot64(blk):
        return _rope(blk, *rope64, 96, 32) if has_rope else blk

    def rot_mla(blk):
        return _rope(blk, *rope_mla, 112, 16) if has_rope else blk

    m = mod_ref[0]
    h = ((_rms(x_ref[...]) * g_ref[...]) * (1.0 + m[1:2, :]) + m[0:1, :]).astype(BF16)
    tm = h.shape[0]
    dv = 2 * DIFF_HD
    ones = jnp.ones((SUM_ROWS, tm), BF16)

    tiles = [COL_DIFF] + [c0 for c0 in range(0, N_PROJ, tn) if c0 != COL_DIFF]
    r = {c0: _dot(h, w_ref[:, c0:c0 + tn]) for c0 in tiles[:2]}

    def take(col, width=LANES):
        c0 = col // tn * tn
        return r[c0][:, col - c0:col - c0 + width]

    def put(col, blk):
        seg_ref[:, col:col + LANES] = blk.astype(seg_ref.dtype)

    cn = _rms(take(COL_MLA, KV_LORA)) * kvn_ref[...]
    cb = cn.astype(BF16)
    kfull = _dot(cb, wk_ref[...])
    vt = _dot(cb, wv_ref[...]).T
    q = _dot((_rms(take(COL_MLA + KV_LORA + LANES, Q_LORA)) * qn_ref[...]).astype(BF16), wq_ref[...])
    for c0 in tiles[2:]:
        r[c0] = _dot(h, w_ref[:, c0:c0 + tn])

    for c0 in tiles:
        seg_ref[:, c0:c0 + tn] = r[c0].astype(seg_ref.dtype)
    for i in range(4):
        put(COL_RET + LANES * i, rot64(take(COL_RET + LANES * i)))
    for hd in range(H_DIFF):
        put(COL_DIFF + dv * hd, rot64(take(COL_DIFF + dv * hd)) * LOG2_E)
        k = rot64(take(COL_DIFF + 512 + dv * hd))
        put(COL_DIFF + 512 + dv * hd, k)
        kd_ref[0, hd, 0] = (k * (DIFF_HD ** -0.5)).astype(BF16)
        vd_ref[0, hd, 0, :dv, :] = take(COL_DIFF + 1024 + dv * hd).T.astype(BF16)
        vd_ref[0, hd, 0, dv:, :] = ones
    kp = rot_mla(take(COL_MLA + KV_LORA))
    put(COL_MLA + KV_LORA, kp)
    if ctx_refs:
        dk_ref, dv_ref, ckv_ref, kpe_ref = ctx_refs
        dk_ref[0, 0] = take(COL_DIFF + 512, 512)
        dv_ref[0, 0] = take(COL_DIFF + 1024, 512)
        ckv_ref[0, 0] = cn
        kpe_ref[0, 0] = kp[:, MLA_NOPE:MLA_NOPE + MLA_ROPE]
    for hm in range(H_MLA):
        cols = slice(LANES * hm, LANES * (hm + 1))
        km_ref[0, hm, 0] = (kfull[:, cols] + kp).astype(BF16)
        vm_ref[0, hm, 0, :MLA_V, :] = vt[MLA_V * hm:MLA_V * (hm + 1)].astype(BF16)
        vm_ref[0, hm, 0, MLA_V:, :] = ones
        qm_ref[:, cols] = (rot_mla(q[:, cols]) * (MLA_SCALE * LOG2_E)).astype(BF16)


def _input_proj(x, b, s, tk, g, mod, modsel, lw, qn, kvn, rope, ctx):
    t = b * s
    n_new = s // tk
    tn = PROJ_TILE_N
    assert COL_DIFF % tn == 0 and N_PROJ - COL_DIFF == tn and s % tk == 0
    mod_row = _mod_row(modsel, tk)
    resident = lambda a: pl.BlockSpec(a.shape, lambda i: (0,) * a.ndim, pipeline_mode=pl.Buffered(1))
    qn2, kvn2 = qn.reshape(1, Q_LORA), kvn.reshape(1, KV_LORA)
    in_specs = [pl.BlockSpec((tk, D_MODEL), lambda i: (i, 0)),
                pl.BlockSpec((1, D_MODEL), lambda i: (0, 0)),
                pl.BlockSpec((1, 8, D_MODEL), lambda i: (mod_row(i), 0, 0)),
                resident(lw["proj"]), resident(qn2), resident(kvn2),
                resident(lw["wq"]), resident(lw["wk"]), resident(lw["wv"])]
    args = [x, g.reshape(1, D_MODEL), mod, lw["proj"], qn2, kvn2, lw["wq"], lw["wk"], lw["wv"]]
    if rope is not None:
        in_specs += [pl.BlockSpec((tk, LANES), lambda i: (i % n_new, 0))] * 6
        args += list(rope[0]) + list(rope[1])
    aliases = {}
    n_ctx_in = 0
    if ctx is not None and ctx[2] is not None:
        n_ctx_in = len(ctx[2])
        aliases = {len(args) + k: 6 + k for k in range(n_ctx_in)}
        in_specs += [pl.BlockSpec(memory_space=pl.ANY)] * n_ctx_in
        args += list(ctx[2])
    tile = lambda i: (i // n_new, 0, i % n_new, 0, 0)
    out_shape = [jax.ShapeDtypeStruct((t, N_PROJ), BF16),
                 jax.ShapeDtypeStruct((t, H_MLA * LANES), BF16),
                 jax.ShapeDtypeStruct((b, H_DIFF, n_new, tk, LANES), BF16),
                 jax.ShapeDtypeStruct((b, H_DIFF, n_new, 2 * DIFF_HD + SUM_ROWS, tk), BF16),
                 jax.ShapeDtypeStruct((b, H_MLA, n_new, tk, LANES), BF16),
                 jax.ShapeDtypeStruct((b, H_MLA, n_new, MLA_V + SUM_ROWS, tk), BF16)]
    out_specs = [pl.BlockSpec((tk, N_PROJ), lambda i: (i, 0)),
                 pl.BlockSpec((tk, H_MLA * LANES), lambda i: (i, 0)),
                 pl.BlockSpec((1, H_DIFF, 1, tk, LANES), tile),
                 pl.BlockSpec((1, H_DIFF, 1, 2 * DIFF_HD + SUM_ROWS, tk), tile),
                 pl.BlockSpec((1, H_MLA, 1, tk, LANES), tile),
                 pl.BlockSpec((1, H_MLA, 1, MLA_V + SUM_ROWS, tk), tile)]
    if ctx is not None:
        layer, depth, _ = ctx
        for width in (H_DIFF * 2 * DIFF_HD, H_DIFF * 2 * DIFF_HD, KV_LORA, MLA_ROPE):
            out_shape.append(jax.ShapeDtypeStruct((b, depth, s, width), F32))
            out_specs.append(pl.BlockSpec((1, 1, tk, width), lambda i: (i // n_new, layer, i % n_new, 0)))
    outs = pl.pallas_call(
        functools.partial(_input_proj_body, has_rope=rope is not None, n_ctx_in=n_ctx_in, tn=tn),
        out_shape=tuple(out_shape),
        grid=(t // tk,),
        in_specs=in_specs,
        out_specs=tuple(out_specs),
        input_output_aliases=aliases,
        compiler_params=_cparams("parallel"),
        name="input_proj",
    )(*args)
    return tuple(outs[:6]), tuple(outs[6:])


def _ret_scan_body(gc_ref, kf_ref, vf_ref, kb_ref, vb_ref, zf_ref, zb_ref, s0f_ref, s0b_ref,
                   sf_ref, sb_ref, finf_ref, finb_ref, st_ref, *, n_steps, group):
    p = pl.program_id(1)
    t = pl.program_id(2)
    c = RET_CHUNK

    @pl.when(t == 0)
    def _():
        st_ref[0] = s0f_ref[0]
        st_ref[1] = s0b_ref[0]

    def local_terms(gi, k_ref, v_ref, z_ref):
        rows = slice(gi * c, (gi + 1) * c)
        k = k_ref[rows, :].astype(F32)
        kz_t =(k * (RET_DK ** -0.5) * z_ref[0]).T.astype(BF16)
        v = v_ref[rows, :].astype(BF16)
        return [_dot(kz_t[RET_DK * j:RET_DK * (j + 1)], v[:, RET_DV * j:RET_DV * (j + 1)]) for j in range(2)]

    loc_f = [local_terms(gi, kf_ref, vf_ref, zf_ref) for gi in range(group)]
    loc_b = [local_terms(gi, kb_ref, vb_ref, zb_ref) for gi in range(group)]
    ascending = list(range(group))
    for d, loc, out_ref, order in ((0, loc_f, sf_ref, ascending), (1, loc_b, sb_ref, ascending[::-1])):
        for j in range(2):
            decay = gc_ref[2 * p + j, d]
            st = st_ref[d, j]
            for gi in order:
                out_ref[0, j, gi] = st
                st = decay * st + loc[gi][j]
            st_ref[d, j] = st

    @pl.when(t == n_steps - 1)
    def _():
        finf_ref[0] = st_ref[0]
        finb_ref[0] = st_ref[1]


def _ret_out_body(q_ref, k_ref, v_ref, g_ref, sf_ref, sb_ref, dm_ref, xif_ref, xib_ref, y_ref, *, group):
    c = RET_CHUNK
    lane = lax.broadcasted_iota(jnp.int32, (c, 2 * RET_DK), 1)
    stage1 = []
    for gi in range(group):
        rows = slice(gi * c, (gi + 1) * c)
        q = q_ref[rows, :].astype(F32)
        k = k_ref[rows, :].astype(F32)
        kb =(k * (RET_DK ** -0.5)).astype(BF16)
        s_f = sf_ref[0, :, gi].reshape(2 * RET_DK, RET_DV).astype(BF16)
        s_b = sb_ref[0, :, gi].reshape(2 * RET_DK, RET_DV).astype(BF16)
        for j in range(2):
            in_head = (lane >= RET_DK * j) & (lane < RET_DK * (j + 1))
            qm = jnp.where(in_head, q, 0.0).astype(BF16)
            sc = (_dot_nt(qm, kb) * dm_ref[j]).astype(BF16)
            cross = _dot(qm, s_f) * xif_ref[j] + _dot(qm, s_b) * xib_ref[j]
            stage1.append((rows, j, sc, cross))
    for rows, j, sc, cross in stage1:
        cols = slice(RET_DV * j, RET_DV * (j + 1))
        o = _dot(sc, v_ref[rows, cols].astype(BF16)) + cross
        mu = jnp.mean(o, axis=-1, keepdims=True)
        oc = o - mu
        on = oc * lax.rsqrt(jnp.mean(oc * oc, axis=-1, keepdims=True) + EPS)
        y_ref[rows, cols] = (_silu(g_ref[rows, cols].astype(F32)) * on).astype(y_ref.dtype)


def _retention(seg, b, s, s0f, s0b, tabs):
    c = RET_CHUNK
    n = s // c
    group = math.gcd(n, RET_GROUP)
    ns = n // group
    rows = group * c
    gc, zf, zb, dmat, xif, xib = tabs
    rq0, rk0 = COL_RET // 128, (COL_RET + 256) // 128
    rv0, rg0 = (COL_RET + 512) // 256, (COL_RET + 1024) // 256
    state_shape = jax.ShapeDtypeStruct((b, H_RET, n, RET_DK, RET_DV), F32)
    fin_shape = jax.ShapeDtypeStruct((b, H_RET, RET_DK, RET_DV), F32)

    def fwd(bi, t):
        return bi * ns + t

    def bwd(bi, t):
        return bi * ns + (ns - 1 - t)

    in_specs = [
        pl.BlockSpec(memory_space=pltpu.SMEM),
        pl.BlockSpec((rows, 128), lambda bi, p, t: (fwd(bi, t), rk0 + p)),
        pl.BlockSpec((rows, 256), lambda bi, p, t: (fwd(bi, t), rv0 + p)),
        pl.BlockSpec((rows, 128), lambda bi, p, t: (bwd(bi, t), rk0 + p)),
        pl.BlockSpec((rows, 256), lambda bi, p, t: (bwd(bi, t), rv0 + p)),
        pl.BlockSpec((1, c, 128), lambda bi, p, t: (p, 0, 0)),
        pl.BlockSpec((1, c, 128), lambda bi, p, t: (p, 0, 0)),
        pl.BlockSpec((1, 2, RET_DK, RET_DV), lambda bi, p, t: (bi, p, 0, 0)),
        pl.BlockSpec((1, 2, RET_DK, RET_DV), lambda bi, p, t: (bi, p, 0, 0)),
    ]
    args = [gc, seg, seg, seg, seg, zf, zb, s0f, s0b]
    sf, sb, fin_f, fin_b = pl.pallas_call(
        functools.partial(_ret_scan_body, n_steps=ns, group=group),
        out_shape=(state_shape, state_shape, fin_shape, fin_shape),
        grid=(b, 2, ns),
        in_specs=in_specs,
        out_specs=(pl.BlockSpec((1, 2, group, RET_DK, RET_DV), lambda bi, p, t: (bi, p, t, 0, 0)),
                   pl.BlockSpec((1, 2, group, RET_DK, RET_DV), lambda bi, p, t: (bi, p, ns - 1 - t, 0, 0)),
                   pl.BlockSpec((1, 2, RET_DK, RET_DV), lambda bi, p, t: (bi, p, 0, 0)),
                   pl.BlockSpec((1, 2, RET_DK, RET_DV), lambda bi, p, t: (bi, p, 0, 0))),
        scratch_shapes=[pltpu.VMEM((2, 2, RET_DK, RET_DV), F32)],
        compiler_params=_cparams("parallel", "parallel", "arbitrary"),
        name="ret_scan",
    )(*args)

    in_specs = [
        pl.BlockSpec((rows, 128), lambda bi, p, t: (fwd(bi, t), rq0 + p)),
        pl.BlockSpec((rows, 128), lambda bi, p, t: (fwd(bi, t), rk0 + p)),
        pl.BlockSpec((rows, 256), lambda bi, p, t: (fwd(bi, t), rv0 + p)),
        pl.BlockSpec((rows, 256), lambda bi, p, t: (fwd(bi, t), rg0 + p)),
        pl.BlockSpec((1, 2, group, RET_DK, RET_DV), lambda bi, p, t: (bi, p, t, 0, 0)),
        pl.BlockSpec((1, 2, group, RET_DK, RET_DV), lambda bi, p, t: (bi, p, t, 0, 0)),
        pl.BlockSpec((2, c, c), lambda bi, p, t: (p, 0, 0)),
        pl.BlockSpec((2, c, 128), lambda bi, p, t: (p, 0, 0)),
        pl.BlockSpec((2, c, 128), lambda bi, p, t: (p, 0, 0)),
    ]
    args = [seg, seg, seg, seg, sf, sb, dmat, xif, xib]
    y = pl.pallas_call(
        functools.partial(_ret_out_body, group=group),
        out_shape=jax.ShapeDtypeStruct((b * s, H_RET * RET_DV), BF16),
        grid=(b, 2, ns),
        in_specs=in_specs,
        out_specs=pl.BlockSpec((rows, 256), lambda bi, p, t: (fwd(bi, t), p)),
        compiler_params=_cparams("parallel", "parallel", "parallel"),
        name="ret_out",
    )(*args)
    return y, fin_f, fin_b


def _retention_tables(decay_f, decay_b):
    c = RET_CHUNK
    lg_f = jax.nn.log_sigmoid(decay_f.astype(F32))
    lg_b = jax.nn.log_sigmoid(decay_b.astype(F32))
    i = jnp.arange(c, dtype=F32)
    dist = i[:, None] - i[None, :]
    d_f = jnp.where(dist >= 0, jnp.exp(jnp.maximum(dist, 0.0)[None] * lg_f[:, None, None]), 0.0)
    d_b = jnp.where(dist < 0, jnp.exp(jnp.maximum(-dist, 0.0)[None] * lg_b[:, None, None]), 0.0)
    dmat = d_f + d_b
    xi_f = jnp.exp((i + 1.0)[None, :] * lg_f[:, None])
    xi_b = jnp.exp((c - i)[None, :] * lg_b[:, None])
    zeta_f = jnp.exp((c - 1.0 - i)[None, :] * lg_f[:, None])
    zeta_b = jnp.exp(i[None, :] * lg_b[:, None])
    gc = jnp.stack([jnp.exp(c * lg_f), jnp.exp(c * lg_b)], axis=1)

    def lanes(tab, width):
        return jnp.broadcast_to(tab[:, :, None], tab.shape + (width,))

    def pair(tab):
        t64 = lanes(tab, RET_DK).reshape(H_RET // 2, 2, c, RET_DK)
        return jnp.concatenate([t64[:, 0], t64[:, 1]], axis=-1)

    return gc, pair(zeta_f), pair(zeta_b), dmat, lanes(xi_f, RET_DV), lanes(xi_b, RET_DV)


def _diff_cache_body(ck_ref, cv_ref, kt_ref, vt_ref):
    dv = 2 * DIFF_HD
    ones = jnp.ones((SUM_ROWS, vt_ref.shape[-1]), BF16)
    for h in range(H_DIFF):
        cols = slice(dv * h, dv * (h + 1))
        kt_ref[0, h, 0] = (ck_ref[0, 0, :, cols] * (DIFF_HD ** -0.5)).astype(BF16)
        vt_ref[0, h, 0, :dv, :] = cv_ref[0, 0, :, cols].T.astype(BF16)
        vt_ref[0, h, 0, dv:, :] = ones


def _diff_cache_prep(cache_k, cache_v, layer, tk):
    b, _, past, width = cache_k.shape
    n = past // tk
    spec = pl.BlockSpec((1, 1, tk, width), lambda bi, c: (bi, layer, c, 0))
    return pl.pallas_call(
        _diff_cache_body,
        out_shape=(jax.ShapeDtypeStruct((b, H_DIFF, n, tk, LANES), BF16),
                   jax.ShapeDtypeStruct((b, H_DIFF, n, 2 * DIFF_HD + SUM_ROWS, tk), BF16)),
        grid=(b, n),
        in_specs=[spec, spec],
        out_specs=(pl.BlockSpec((1, H_DIFF, 1, tk, LANES), lambda bi, c: (bi, 0, c, 0, 0)),
                   pl.BlockSpec((1, H_DIFF, 1, 2 * DIFF_HD + SUM_ROWS, tk), lambda bi, c: (bi, 0, c, 0, 0))),
        compiler_params=_cparams("parallel", "parallel"),
        name="diff_cache_prep",
    )(cache_k, cache_v)


def _mla_cache_body(cc_ref, cp_ref, wk_ref, wv_ref, kt_ref, vt_ref):
    cb = cc_ref[0, 0].astype(BF16)
    kfull = _dot(cb, wk_ref[...])
    vt = _dot(cb, wv_ref[...]).T
    kp = cp_ref[0, 0]
    ones = jnp.ones((SUM_ROWS, vt.shape[-1]), BF16)
    for h in range(H_MLA):
        kt_ref[0, h, 0] = (kfull[:, LANES * h:LANES * (h + 1)] + kp).astype(BF16)
        vt_ref[0, h, 0, :MLA_V, :] = vt[MLA_V * h:MLA_V * (h + 1)].astype(BF16)
        vt_ref[0, h, 0, MLA_V:, :] = ones


def _mla_cache_prep(cache_ckv, cache_kpe, layer, wk, wv, tk):
    b, _, past, _ = cache_ckv.shape
    n = past // tk
    return pl.pallas_call(
        _mla_cache_body,
        out_shape=(jax.ShapeDtypeStruct((b, H_MLA, n, tk, LANES), BF16),
                   jax.ShapeDtypeStruct((b, H_MLA, n, MLA_V + SUM_ROWS, tk), BF16)),
        grid=(b, n),
        in_specs=[pl.BlockSpec((1, 1, tk, KV_LORA), lambda bi, c: (bi, layer, c, 0)),
                  pl.BlockSpec((1, 1, tk, LANES), lambda bi, c: (bi, layer, c, 0)),
                  pl.BlockSpec(wk.shape, lambda bi, c: (0, 0)),
                  pl.BlockSpec(wv.shape, lambda bi, c: (0, 0))],
        out_specs=(pl.BlockSpec((1, H_MLA, 1, tk, LANES), lambda bi, c: (bi, 0, c, 0, 0)),
                   pl.BlockSpec((1, H_MLA, 1, MLA_V + SUM_ROWS, tk), lambda bi, c: (bi, 0, c, 0, 0))),
        compiler_params=_cparams("parallel", "parallel"),
        name="mla_cache_prep",
    )(cache_ckv, cache_kpe, wk, wv)


def _flash_keymajor(chains, nk, dv):
    m, acc, s_next = [], [], []
    for q_t, k_tile, _ in chains:
        n = q_t.shape[1]
        m.append(jnp.full((1, n), NEG_BIG, F32))
        acc.append(jnp.zeros((dv + SUM_ROWS, n), F32))
        s_next.append(_dot(k_tile(0), q_t))
    for i in range(nk):
        for ci, (q_t, k_tile, vt_tile) in enumerate(chains):
            s = s_next[ci]
            if i + 1 < nk:
                s_next[ci] = _dot(k_tile(i + 1), q_t)
            m_new = jnp.maximum(m[ci], jnp.max(s, axis=0, keepdims=True))
            alpha = jnp.exp2(m[ci] - m_new)
            p = jnp.exp2(s - m_new)
            acc[ci] = alpha * acc[ci] + _dot(vt_tile(i), p.astype(BF16))
            m[ci] = m_new
    return [(a[dv:dv + 1], a[:dv]) for a in acc]


def _key_tiles(kv_refs, counts, head):
    index = [(part, i) for part, n in enumerate(counts) for i in range(n)]
    k_tile = lambda i: kv_refs[2 * index[i][0]][0, head, index[i][1]]
    vt_tile = lambda i: kv_refs[2 * index[i][0] + 1][0, head, index[i][1]]
    return k_tile, vt_tile


def _kv_specs(parts, heads, dv, head_block):
    specs, args = [], []
    for kt, vt in parts:
        n, tk = kt.shape[2], kt.shape[3]
        specs += [pl.BlockSpec((1, heads, n, tk, LANES), lambda bi, h, qi: (bi, head_block(h), 0, 0, 0)),
                  pl.BlockSpec((1, heads, n, dv + SUM_ROWS, tk), lambda bi, h, qi: (bi, head_block(h), 0, 0, 0))]
        args += [kt, vt]
    return specs, args


def _diff_attn_body(q_ref, dl_ref, *refs, counts, tq, heads, lam_init):
    kv_refs, y_ref = refs[:-1], refs[-1]
    cq = min(tq, DIFF_CHAIN_Q)
    lane = lax.broadcasted_iota(jnp.int32, (cq, 2 * DIFF_HD), 1)
    chains, where = [], []
    for h in range(heads):
        k_tile, vt_tile = _key_tiles(kv_refs, counts, h)
        for r in range(0, tq, cq):
            q = q_ref[r:r + cq, LANES * h:LANES * (h + 1)].astype(F32)
            q1 = jnp.where(lane < DIFF_HD, q, 0.0)
            q2 = jnp.where(lane >= DIFF_HD, q, 0.0)
            q_t = jnp.concatenate([q1, q2], axis=0).T.astype(BF16)
            chains.append((q_t, k_tile, vt_tile))
            where.append((r, h))
    results = _flash_keymajor(chains, sum(counts), 2 * DIFF_HD)
    dl = dl_ref[...]
    lam = (jnp.exp(jnp.sum(dl[0:1] * dl[1:2], axis=-1, keepdims=True))
           - jnp.exp(jnp.sum(dl[2:3] * dl[3:4], axis=-1, keepdims=True)) + lam_init)
    for (r, h), (l, acc) in zip(where, results):
        o = acc * (1.0 / l)
        o = o[:, :cq] - lam * o[:, cq:]
        o = o * lax.rsqrt(jnp.mean(o * o, axis=0, keepdims=True) + EPS) * (1.0 - lam_init)
        y_ref[r:r + cq, LANES * h:LANES * (h + 1)] = o.T.astype(y_ref.dtype)


def _diff_attention(seg, b, s, kv_parts, dl, lam_init):
    tq = min(DIFF_Q_TILE, s)
    nq = s // tq
    heads = max(1, min(H_DIFF, ATTN_CHAINS * DIFF_CHAIN_Q // tq))
    width = LANES * heads
    q0 = COL_DIFF // width
    kv_specs, kv_args = _kv_specs(kv_parts, heads, 2 * DIFF_HD, lambda h: h)
    in_specs = [pl.BlockSpec((tq, width), lambda bi, h, qi: (bi * nq + qi, q0 + h)),
                pl.BlockSpec((8, 128), lambda bi, h, qi: (0, 0))] + kv_specs
    args = [seg, dl] + kv_args
    return pl.pallas_call(
        functools.partial(_diff_attn_body, counts=tuple(kt.shape[2] for kt, _ in kv_parts), tq=tq, heads=heads,
                          lam_init=lam_init),
        out_shape=jax.ShapeDtypeStruct((b * s, H_DIFF * 2 * DIFF_HD), BF16),
        grid=(b, H_DIFF // heads, nq),
        in_specs=in_specs,
        out_specs=pl.BlockSpec((tq, width), lambda bi, h, qi: (bi * nq + qi, h)),
        compiler_params=_cparams("parallel", "parallel", "arbitrary"),
        name="diff_attn",
    )(*args)


def _mla_attn_body(q_ref, *refs, counts, heads):
    kv_refs, y_ref = refs[:-1], refs[-1]
    tq = q_ref.shape[0]
    cq = min(tq, MLA_CHAIN_Q)
    chains, where = [], []
    for r in range(0, tq, cq):
        for j in range(heads):
            q_t = q_ref[r:r + cq, LANES * j:LANES * (j + 1)].astype(F32).T.astype(BF16)
            chains.append((q_t,) + _key_tiles(kv_refs, counts, j))
        where += [(r, j) for j in range(0, heads, 2)]
    outs = [acc * (1.0 / l) for l, acc in _flash_keymajor(chains, sum(counts), MLA_V)]
    for ci, (r, j) in enumerate(where):
        pair = jnp.concatenate(outs[2 * ci:2 * ci + 2], axis=0)
        y_ref[r:r + cq, MLA_V * j:MLA_V * (j + 2)] = pair.T.astype(y_ref.dtype)


def _mla_attention(q_all, b, s, kv_parts):
    tq = min(MLA_Q_TILE, s)
    nq = s // tq
    heads = max(2, min(H_MLA, ATTN_CHAINS * MLA_CHAIN_Q // tq))
    kv_specs, kv_args = _kv_specs(kv_parts, heads, MLA_V, lambda p: p)
    in_specs = [pl.BlockSpec((tq, heads * LANES), lambda bi, p, qi: (bi * nq + qi, p))] + kv_specs
    args = [q_all] + kv_args
    return pl.pallas_call(
        functools.partial(_mla_attn_body, counts=tuple(kt.shape[2] for kt, _ in kv_parts), heads=heads),
        out_shape=jax.ShapeDtypeStruct((b * s, H_MLA * MLA_V), BF16),
        grid=(b, H_MLA // heads, nq),
        in_specs=in_specs,
        out_specs=pl.BlockSpec((tq, heads * MLA_V), lambda bi, p, qi: (bi * nq + qi, p)),
        compiler_params=_cparams("parallel", "parallel", "arbitrary"),
        name="mla_attn",
    )(*args)


def _merge_body(x_ref, yr_ref, yd_ref, ym_ref, gt_ref, mod_ref, wbr_ref, wo_ref, o_ref):
    merged = None
    for gi, y_ref in enumerate((yr_ref, yd_ref, ym_ref)):
        br = _dot(y_ref[...], wbr_ref[gi])
        term = _sigmoid(gt_ref[:, D_MODEL * gi:D_MODEL * (gi + 1)].astype(F32)) * br
        merged = term if merged is None else merged + term
    out = _dot(merged.astype(BF16), wo_ref[...])
    o_ref[...] = x_ref[...] + mod_ref[0][2:3, :] * out


def _merge(x, yr, yd, ym, seg, mod, modsel, wbr, wo):
    t = x.shape[0]
    tm = min(TOKEN_TILE, t)
    mod_row = _mod_row(modsel, tm)
    row = lambda i: (i, 0)
    return pl.pallas_call(
        _merge_body,
        out_shape=jax.ShapeDtypeStruct((t, D_MODEL), F32),
        grid=(t // tm,),
        in_specs=[pl.BlockSpec((tm, D_MODEL), row),
                  pl.BlockSpec((tm, 512), row), pl.BlockSpec((tm, 512), row), pl.BlockSpec((tm, 512), row),
                  pl.BlockSpec((tm, 3 * D_MODEL), lambda i: (i, COL_GATE // (3 * D_MODEL))),
                  pl.BlockSpec((1, 8, D_MODEL), lambda i: (mod_row(i), 0, 0)),
                  pl.BlockSpec(wbr.shape, lambda i: (0, 0, 0)),
                  pl.BlockSpec(wo.shape, lambda i: (0, 0))],
        out_specs=pl.BlockSpec((tm, D_MODEL), row),
        compiler_params=_cparams("parallel"),
        name="merge",
    )(x, yr, yd, ym, seg, mod, wbr, wo)


def _ffn_body(*refs, final):
    if final:
        x_ref, g_ref, mod_ref, wa_ref, wb_ref, wo_ref, fg_ref, o_ref = refs
    else:
        x_ref, g_ref, mod_ref, wa_ref, wb_ref, wo_ref, o_ref = refs
    m = mod_ref[0]
    x = x_ref[...]
    h = ((_rms(x) * g_ref[...]) * (1.0 + m[4:5, :]) + m[3:4, :]).astype(BF16)
    d_ff = wa_ref.shape[1]
    chunks = [(lo, min(lo + FFN_CHUNK, d_ff)) for lo in range(0, d_ff, FFN_CHUNK)]

    def up(k):
        lo, hi = chunks[k]
        return _dot(h, wa_ref[:, lo:hi]), _dot(h, wb_ref[:, lo:hi])

    nxt = up(0)
    y = None
    for k, (lo, hi) in enumerate(chunks):
        a, b = nxt
        if k + 1 < len(chunks):
            nxt = up(k + 1)
        d = _dot((_silu(a) * b).astype(BF16), wo_ref[lo:hi, :])
        y = d if y is None else y + d
    xn = x + m[5:6, :] * y
    if final:
        xn = _rms(xn) * fg_ref[...]
    o_ref[...] = xn


def _ffn(x, g, mod, modsel, wa, wb, wo, final_g):
    t = x.shape[0]
    tm = min(TOKEN_TILE, t)
    mod_row = _mod_row(modsel, tm)
    final = final_g is not None
    resident = lambda a: pl.BlockSpec(a.shape, lambda i: (0, 0), pipeline_mode=pl.Buffered(1))
    in_specs = [pl.BlockSpec((tm, D_MODEL), lambda i: (i, 0)),
                pl.BlockSpec((1, D_MODEL), lambda i: (0, 0)),
                pl.BlockSpec((1, 8, D_MODEL), lambda i: (mod_row(i), 0, 0)),
                resident(wa), resident(wb), resident(wo)]
    args = [x, g.reshape(1, D_MODEL), mod, wa, wb, wo]
    if final:
        in_specs.append(pl.BlockSpec((1, D_MODEL), lambda i: (0, 0)))
        args.append(final_g.reshape(1, D_MODEL))
    return pl.pallas_call(
        functools.partial(_ffn_body, final=final),
        out_shape=jax.ShapeDtypeStruct((t, D_MODEL), F32),
        grid=(t // tm,),
        in_specs=in_specs,
        out_specs=pl.BlockSpec((tm, D_MODEL), lambda i: (i, 0)),
        compiler_params=_cparams("parallel"),
        name="ffn",
    )(*args)


def _axial_angles(n_tokens, rot_dim):
    t = jnp.arange(n_tokens)
    row = (t // GRID_W).astype(F32)
    col = (t % GRID_W).astype(F32)
    nf = rot_dim // 4
    inv = ROPE_BASE ** (-jnp.arange(nf, dtype=F32) / nf)
    ang = jnp.concatenate([row[:, None] * inv, col[:, None] * inv], axis=-1)
    return jnp.cos(ang), jnp.sin(ang)


def _rope_tables_head64(n_tokens):
    cos, sin = _axial_angles(n_tokens, 64)
    zero = jnp.zeros_like(sin)
    c = jnp.tile(jnp.concatenate([cos, cos], axis=-1), (1, 2))
    sa = jnp.tile(jnp.concatenate([-sin, zero], axis=-1), (1, 2))
    sb = jnp.tile(jnp.concatenate([zero, sin], axis=-1), (1, 2))
    return c, sa, sb


def _rope_tables_mla(n_tokens):
    cos, sin = _axial_angles(n_tokens, MLA_ROPE)
    z16 = jnp.zeros_like(sin)
    one64 = jnp.ones((n_tokens, MLA_NOPE), F32)
    z64 = jnp.zeros((n_tokens, MLA_NOPE), F32)
    z32 = jnp.zeros((n_tokens, 32), F32)
    c = jnp.concatenate([one64, cos, cos, z32], axis=-1)
    sa = jnp.concatenate([z64, -sin, z16, z32], axis=-1)
    sb = jnp.concatenate([z64, z16, sin, z32], axis=-1)
    return c, sa, sb


def _layer_weights(l, w_in, w_uq, w_ukv, w_branch, w_out, w_ffn_in, w_ffn_out):
    wi = w_in[l]
    d = wi.shape[0]
    w_ret = wi[:, 0:1536]
    w_diff = wi[:, 1536:3072]
    w_mla = jnp.concatenate([wi[:, 3456:3712], jnp.zeros((d, 64), wi.dtype), wi[:, 3712:3744],
                             jnp.zeros((d, 32), wi.dtype), wi[:, 3072:3456]], axis=1)
    w_gate = wi[:, 3744:]
    w_proj = jnp.concatenate([w_gate, w_ret, w_diff, w_mla], axis=1)
    assert w_proj.shape[1] == N_PROJ
    hd = MLA_NOPE + MLA_ROPE
    wq = jnp.pad(w_uq[l].reshape(Q_LORA, H_MLA, hd), ((0, 0), (0, 0), (0, LANES - hd))).reshape(Q_LORA, H_MLA * LANES)
    wkv = w_ukv[l].reshape(KV_LORA, H_MLA, MLA_NOPE + MLA_V)
    wk = jnp.pad(wkv[:, :, :MLA_NOPE], ((0, 0), (0, 0), (0, LANES - MLA_NOPE))).reshape(KV_LORA, H_MLA * LANES)
    wv = wkv[:, :, MLA_NOPE:].reshape(KV_LORA, H_MLA * MLA_V)
    wa, wb = w_ffn_in[l][:, :D_FF], w_ffn_in[l][:, D_FF:]
    cast = lambda a: a.astype(BF16)
    return dict(proj=cast(w_proj), wq=cast(wq), wk=cast(wk), wv=cast(wv), wbr=cast(w_branch[l]), wo=cast(w_out[l]),
                wa=cast(wa), wb=cast(wb), wf=cast(w_ffn_out[l]))


def _mixer_layer(x, b, s, l, lw, mod, modsel, norm1_g, norm2_g, tabs, mla_q_norm, mla_kv_norm, dl, lam_init,
                 s0f, s0b, caches, ropes, ctx, final_g):
    tk = min(KEY_TILE, s)
    (seg, q_mla, kd, vd, km, vm), ctx_out = _input_proj(x, b, s, tk, norm1_g, mod, modsel, lw, mla_q_norm,
                                                       mla_kv_norm, ropes, ctx)

    y_ret, fin_f, fin_b = _retention(seg, b, s, s0f, s0b, tabs)

    diff_parts, mla_parts = [(kd, vd)], [(km, vm)]
    if caches is not None:
        cache_k, cache_v, cache_ckv, cache_kpe = caches
        diff_parts.insert(0, _diff_cache_prep(cache_k, cache_v, l, tk))
        mla_parts.insert(0, _mla_cache_prep(cache_ckv, cache_kpe, l, lw["wk"], lw["wv"], tk))
    y_diff = _diff_attention(seg, b, s, diff_parts, dl, lam_init)
    y_mla = _mla_attention(q_mla, b, s, mla_parts)

    x = _merge(x, y_ret, y_diff, y_mla, seg, mod, modsel, lw["wbr"], lw["wo"])
    x = _ffn(x, norm2_g, mod, modsel, lw["wa"], lw["wb"], lw["wf"], final_g)
    return x, fin_f, fin_b, ctx_out


def kernel(x_prompt, x_sample, state_ret_fwd, state_ret_bwd, cache_diff_k, cache_diff_v, cache_mla_ckv,
           cache_mla_kpe, c, c_ctx, norm1_g, norm2_g, w_ada, b_ada, w_in, ret_decay_fwd, ret_decay_bwd,
           diff_lambda, mla_q_norm, mla_kv_norm, w_uq, w_ukv, w_branch, w_out, w_ffn_in, w_ffn_out, final_g):
    bp, sp, d = x_prompt.shape
    bs, ss, _ = x_sample.shape
    depth = w_in.shape[0]
    past = cache_diff_k.shape[2]

    n_rows = -(-(1 + bs) // 8) * 8
    cond = jnp.zeros((n_rows, d), F32).at[0].set(c_ctx).at[1:1 + bs].set(c)
    mod_all = _modulation(cond, w_ada, b_ada)
    mod_all = jnp.pad(mod_all.reshape(depth, n_rows, 6, d), ((0, 0), (0, 0), (0, 2), (0, 0)))

    sel_prompt = (0, bp * sp)
    sel_sample = (1, ss)

    ropes = (_rope_tables_head64(ss), _rope_tables_mla(ss))
    cache_k = cache_diff_k.reshape(bs, depth, past, H_DIFF * 2 * DIFF_HD)
    cache_v = cache_diff_v.reshape(bs, depth, past, H_DIFF * 2 * DIFF_HD)
    cache_kpe = jnp.pad(cache_mla_kpe, ((0, 0), (0, 0), (0, 0), (MLA_NOPE, LANES - MLA_NOPE - MLA_ROPE)))
    zero_state = jnp.zeros((bp, H_RET, RET_DK, RET_DV), F32)

    xp = x_prompt.reshape(bp * sp, d)
    xs = x_sample.reshape(bs * ss, d)
    ret_f, ret_b, ctx_arrays = [], [], None
    for l in range(depth):
        lw = _layer_weights(l, w_in, w_uq, w_ukv, w_branch, w_out, w_ffn_in, w_ffn_out)
        lam_init = 0.8 - 0.6 * math.exp(-0.3 * l)
        tabs = _retention_tables(ret_decay_fwd[l], ret_decay_bwd[l])
        dl = jnp.pad(diff_lambda[l], ((0, 4), (0, LANES - DIFF_HD)))
        fg = final_g if l == depth - 1 else None
        common = (norm1_g[l], norm2_g[l], tabs, mla_q_norm[l], mla_kv_norm[l], dl, lam_init)

        xp, fin_f, fin_b, ctx_arrays = _mixer_layer(xp, bp, sp, l, lw, mod_all[l], sel_prompt, *common, zero_state,
                                                    zero_state, None, None, (l, depth, ctx_arrays), fg)
        ret_f.append(fin_f)
        ret_b.append(fin_b)

        xs, _, _, _ = _mixer_layer(xs, bs, ss, l, lw, mod_all[l], sel_sample, *common, state_ret_fwd[:, l],
                                   state_ret_bwd[:, l], (cache_k, cache_v, cache_mla_ckv, cache_kpe), ropes, None, fg)

    dk_all, dv_all, ckv_all, kpe_all = ctx_arrays
    head_shape = (bp, depth, sp, H_DIFF, 2 * DIFF_HD)
    return (xp.reshape(bp, sp, d), xs.reshape(bs, ss, d), jnp.stack(ret_f, axis=1), jnp.stack(ret_b, axis=1),
            dk_all.reshape(head_shape), dv_all.reshape(head_shape), ckv_all, kpe_all)
```

```python
import functools
import math

import jax
import jax.numpy as jnp
from jax import lax
from jax.experimental import pallas as pl
from jax.experimental.pallas import tpu as pltpu

F32 = jnp.float32
BF16 = jnp.bfloat16

D_MODEL = 1024
GRID_W = 64
H_RET, RET_DK, RET_DV, RET_CHUNK = 4, 64, 128, 128
H_DIFF, DIFF_HD = 4, 64
H_MLA, MLA_NOPE, MLA_ROPE, MLA_V = 8, 64, 32, 64
Q_LORA, KV_LORA = 384, 256
D_FF = 2816
ROPE_BASE = 10000.0
EPS = 1e-6
MLA_SCALE = (MLA_NOPE + MLA_ROPE) ** -0.5
LANES = 128
NEG_BIG = -1e30
LOG2_E = math.log2(math.e)

TOKEN_TILE = 512
FFN_CHUNK = 1024
DIFF_Q_TILE = 2048
DIFF_CHAIN_Q = 256
MLA_Q_TILE = 2048
MLA_CHAIN_Q = 512
ATTN_CHAINS = 8
KEY_TILE = 512
RET_GROUP = 8
SUM_ROWS = 16
PROJ_TILE_N = 2304

N_PROJ = 6912
COL_GATE = 0
COL_RET = 3072
COL_DIFF = 4608
COL_MLA = 6144
VMEM_LIMIT = 56 * 1024 * 1024


def _cparams(*sem):
    return pltpu.CompilerParams(dimension_semantics=sem, vmem_limit_bytes=VMEM_LIMIT)


def _mod_row(modsel, tm):
    base, tokens_per_row = modsel
    assert tokens_per_row % tm == 0
    return lambda i: base + (i * tm) // tokens_per_row


def _dot(a, b):
    return jnp.dot(a, b, preferred_element_type=F32)


def _dot_nt(a, b):
    return lax.dot_general(a, b, (((1,), (1,)), ((), ())), preferred_element_type=F32)


def _rms(x):
    return x * lax.rsqrt(jnp.mean(x * x, axis=-1, keepdims=True) + EPS)


def _silu(x):
    return x * (1.0 / (1.0 + jnp.exp(-x)))


def _sigmoid(x):
    return 1.0 / (1.0 + jnp.exp(-x))


def _rope(x, c_ref, sa_ref, sb_ref, shift_a, shift_b, rows=slice(None)):
    return (x * c_ref[rows, :] + pltpu.roll(x, shift_a, 1) * sa_ref[rows, :]
            + pltpu.roll(x, shift_b, 1) * sb_ref[rows, :])


def _mod_body(c_ref, w_ref, b_ref, o_ref):
    o_ref[0] = _dot(_silu(c_ref[...]).astype(BF16), w_ref[0].astype(BF16)) + b_ref[0]


def _modulation(cond, w_ada, b_ada):
    depth, d, n = w_ada.shape
    rows = cond.shape[0]
    tn = 1536
    return pl.pallas_call(
        _mod_body,
        out_shape=jax.ShapeDtypeStruct((depth, rows, n), F32),
        grid=(depth, n // tn),
        in_specs=[pl.BlockSpec((rows, d), lambda l, j: (0, 0)),
                  pl.BlockSpec((1, d, tn), lambda l, j: (l, 0, j)),
                  pl.BlockSpec((1, 1, tn), lambda l, j: (l, 0, j))],
        out_specs=pl.BlockSpec((1, rows, tn), lambda l, j: (l, 0, j)),
        compiler_params=_cparams("parallel", "parallel"),
        name="adaln_mod",
    )(cond, w_ada, b_ada.reshape(depth, 1, n))


def _input_proj_body(*refs, has_rope, n_ctx_in, tn):
    refs = list(refs)
    x_ref, g_ref, mod_ref, w_ref, qn_ref, kvn_ref, wq_ref, wk_ref, wv_ref = refs[:9]
    refs = refs[9:]
    if has_rope:
        rope64, rope_mla = refs[:3], refs[3:6]
        refs = refs[6:]
    refs = refs[n_ctx_in:]
    seg_ref, qm_ref, kd_ref, vd_ref, km_ref, vm_ref = refs[:6]
    ctx_refs = refs[6:]

    def rot64(blk):
        return _rope(blk, *rope64, 96, 32) if has_rope else blk

    def rot_mla(blk):
        return _rope(blk, *rope_mla, 112, 16) if has_rope else blk

    m = mod_ref[0]
    h = ((_rms(x_ref[...]) * g_ref[...]) * (1.0 + m[1:2, :]) + m[0:1, :]).astype(BF16)
    tm = h.shape[0]
    dv = 2 * DIFF_HD
    ones = jnp.ones((SUM_ROWS, tm), BF16)

    tiles = [COL_DIFF] + [c0 for c0 in range(0, N_PROJ, tn) if c0 != COL_DIFF]
    r = {c0: _dot(h, w_ref[:, c0:c0 + tn]) for c0 in tiles[:2]}

    def take(col, width=LANES):
        c0 = col // tn * tn
        return r[c0][:, col - c0:col - c0 + width]

    def put(col, blk):
        seg_ref[:, col:col + LANES] = blk.astype(seg_ref.dtype)

    cn = _rms(take(COL_MLA, KV_LORA)) * kvn_ref[...]
    cb = cn.astype(BF16)
    kfull = _dot(cb, wk_ref[...])
    vt = _dot(cb, wv_ref[...]).T
    q = _dot((_rms(take(COL_MLA + KV_LORA + LANES, Q_LORA)) * qn_ref[...]).astype(BF16), wq_ref[...])
    for c0 in tiles[2:]:
        r[c0] = _dot(h, w_ref[:, c0:c0 + tn])

    for c0 in tiles:
        seg_ref[:, c0:c0 + tn] = r[c0].astype(seg_ref.dtype)
    for i in range(4):
        put(COL_RET + LANES * i, rot64(take(COL_RET + LANES * i)))
    for hd in range(H_DIFF):
        put(COL_DIFF + dv * hd, rot64(take(COL_DIFF + dv * hd)) * LOG2_E)
        k = rot64(take(COL_DIFF + 512 + dv * hd))
        put(COL_DIFF + 512 + dv * hd, k)
        kd_ref[0, hd, 0] = (k * (DIFF_HD ** -0.5)).astype(BF16)
        vd_ref[0, hd, 0, :dv, :] = take(COL_DIFF + 1024 + dv * hd).T.astype(BF16)
        vd_ref[0, hd, 0, dv:, :] = ones
    kp = rot_mla(take(COL_MLA + KV_LORA))
    put(COL_MLA + KV_LORA, kp)
    if ctx_refs:
        dk_ref, dv_ref, ckv_ref, kpe_ref = ctx_refs
        dk_ref[0, 0] = take(COL_DIFF + 512, 512)
        dv_ref[0, 0] = take(COL_DIFF + 1024, 512)
        ckv_ref[0, 0] = cn
        kpe_ref[0, 0] = kp[:, MLA_NOPE:MLA_NOPE + MLA_ROPE]
    for hm in range(H_MLA):
        cols = slice(LANES * hm, LANES * (hm + 1))
        km_ref[0, hm, 0] = (kfull[:, cols] + kp).astype(BF16)
        vm_ref[0, hm, 0, :MLA_V, :] = vt[MLA_V * hm:MLA_V * (hm + 1)].astype(BF16)
        vm_ref[0, hm, 0, MLA_V:, :] = ones
        qm_ref[:, cols] = (rot_mla(q[:, cols]) * (MLA_SCALE * LOG2_E)).astype(BF16)


def _input_proj(x, b, s, tk, g, mod, modsel, lw, qn, kvn, rope, ctx):
    t = b * s
    n_new = s // tk
    tn = PROJ_TILE_N
    assert COL_DIFF % tn == 0 and N_PROJ - COL_DIFF == tn and s % tk == 0
    mod_row = _mod_row(modsel, tk)
    resident = lambda a: pl.BlockSpec(a.shape, lambda i: (0,) * a.ndim, pipeline_mode=pl.Buffered(1))
    qn2, kvn2 = qn.reshape(1, Q_LORA), kvn.reshape(1, KV_LORA)
    in_specs = [pl.BlockSpec((tk, D_MODEL), lambda i: (i, 0)),
                pl.BlockSpec((1, D_MODEL), lambda i: (0, 0)),
                pl.BlockSpec((1, 8, D_MODEL), lambda i: (mod_row(i), 0, 0)),
                resident(lw["proj"]), resident(qn2), resident(kvn2),
                resident(lw["wq"]), resident(lw["wk"]), resident(lw["wv"])]
    args = [x, g.reshape(1, D_MODEL), mod, lw["proj"], qn2, kvn2, lw["wq"], lw["wk"], lw["wv"]]
    if rope is not None:
        in_specs += [pl.BlockSpec((tk, LANES), lambda i: (i % n_new, 0))] * 6
        args += list(rope[0]) + list(rope[1])
    aliases = {}
    n_ctx_in = 0
    if ctx is not None and ctx[2] is not None:
        n_ctx_in = len(ctx[2])
        aliases = {len(args) + k: 6 + k for k in range(n_ctx_in)}
        in_specs += [pl.BlockSpec(memory_space=pl.ANY)] * n_ctx_in
        args += list(ctx[2])
    tile = lambda i: (i // n_new, 0, i % n_new, 0, 0)
    out_shape = [jax.ShapeDtypeStruct((t, N_PROJ), BF16),
                 jax.ShapeDtypeStruct((t, H_MLA * LANES), BF16),
                 jax.ShapeDtypeStruct((b, H_DIFF, n_new, tk, LANES), BF16),
                 jax.ShapeDtypeStruct((b, H_DIFF, n_new, 2 * DIFF_HD + SUM_ROWS, tk), BF16),
                 jax.ShapeDtypeStruct((b, H_MLA, n_new, tk, LANES), BF16),
                 jax.ShapeDtypeStruct((b, H_MLA, n_new, MLA_V + SUM_ROWS, tk), BF16)]
    out_specs = [pl.BlockSpec((tk, N_PROJ), lambda i: (i, 0)),
                 pl.BlockSpec((tk, H_MLA * LANES), lambda i: (i, 0)),
                 pl.BlockSpec((1, H_DIFF, 1, tk, LANES), tile),
                 pl.BlockSpec((1, H_DIFF, 1, 2 * DIFF_HD + SUM_ROWS, tk), tile),
                 pl.BlockSpec((1, H_MLA, 1, tk, LANES), tile),
                 pl.BlockSpec((1, H_MLA, 1, MLA_V + SUM_ROWS, tk), tile)]
    if ctx is not None:
        layer, depth, _ = ctx
        for width in (H_DIFF * 2 * DIFF_HD, H_DIFF * 2 * DIFF_HD, KV_LORA, MLA_ROPE):
            out_shape.append(jax.ShapeDtypeStruct((b, depth, s, width), F32))
            out_specs.append(pl.BlockSpec((1, 1, tk, width), lambda i: (i // n_new, layer, i % n_new, 0)))
    outs = pl.pallas_call(
        functools.partial(_input_proj_body, has_rope=rope is not None, n_ctx_in=n_ctx_in, tn=tn),
        out_shape=tuple(out_shape),
        grid=(t // tk,),
        in_specs=in_specs,
        out_specs=tuple(out_specs),
        input_output_aliases=aliases,
        compiler_params=_cparams("parallel"),
        name="input_proj",
    )(*args)
    return tuple(outs[:6]), tuple(outs[6:])


def _ret_scan_body(gc_ref, kf_ref, vf_ref, kb_ref, vb_ref, zf_ref, zb_ref, s0f_ref, s0b_ref,
                   sf_ref, sb_ref, finf_ref, finb_ref, st_ref, *, n_steps, group):
    t = pl.program_id(1)
    c = RET_CHUNK

    @pl.when(t == 0)
    def _():
        st_ref[0] = s0f_ref[0]
        st_ref[1] = s0b_ref[0]

    def local_terms(gi, k_ref, v_ref, z_ref):
        rows = slice(gi * c, (gi + 1) * c)
        terms = []
        for p in range(H_RET // 2):
            k = k_ref[rows, LANES * p:LANES * (p + 1)].astype(F32)
            kz_t = (k * (RET_DK ** -0.5) * z_ref[p]).T.astype(BF16)
            for j in range(2):
                v = v_ref[rows, RET_DV * (2 * p + j):RET_DV * (2 * p + j + 1)].astype(BF16)
                terms.append(_dot(kz_t[RET_DK * j:RET_DK * (j + 1)], v))
        return terms

    loc_f = [local_terms(gi, kf_ref, vf_ref, zf_ref) for gi in range(group)]
    loc_b = [local_terms(gi, kb_ref, vb_ref, zb_ref) for gi in range(group)]
    ascending = list(range(group))
    for d, loc, out_ref, order in ((0, loc_f, sf_ref, ascending), (1, loc_b, sb_ref, ascending[::-1])):
        for h in range(H_RET):
            decay = gc_ref[h, d]
            st = st_ref[d, h]
            for gi in order:
                out_ref[0, h, gi] = st
                st = decay * st + loc[gi][h]
            st_ref[d, h] = st

    @pl.when(t == n_steps - 1)
    def _():
        finf_ref[0] = st_ref[0]
        finb_ref[0] = st_ref[1]


def _ret_out_body(q_ref, k_ref, v_ref, g_ref, sf_ref, sb_ref, dm_ref, xif_ref, xib_ref, y_ref, *, group):
    c = RET_CHUNK
    lane = lax.broadcasted_iota(jnp.int32, (c, 2 * RET_DK), 1)
    stage1 = []
    for gi in range(group):
        rows = slice(gi * c, (gi + 1) * c)
        for p in range(H_RET // 2):
            q = q_ref[rows, LANES * p:LANES * (p + 1)].astype(F32)
            k = k_ref[rows, LANES * p:LANES * (p + 1)].astype(F32)
            kb = (k * (RET_DK ** -0.5)).astype(BF16)
            s_f = sf_ref[0, 2 * p:2 * p + 2, gi].reshape(2 * RET_DK, RET_DV).astype(BF16)
            s_b = sb_ref[0, 2 * p:2 * p + 2, gi].reshape(2 * RET_DK, RET_DV).astype(BF16)
            for j in range(2):
                h = 2 * p + j
                in_head = (lane >= RET_DK * j) & (lane < RET_DK * (j + 1))
                qm = jnp.where(in_head, q, 0.0).astype(BF16)
                sc = (_dot_nt(qm, kb) * dm_ref[h]).astype(BF16)
                cross = _dot(qm, s_f) * xif_ref[h] + _dot(qm, s_b) * xib_ref[h]
                stage1.append((rows, h, sc, cross))
    for rows, h, sc, cross in stage1:
        cols = slice(RET_DV * h, RET_DV * (h + 1))
        o = _dot(sc, v_ref[rows, cols].astype(BF16)) + cross
        mu = jnp.mean(o, axis=-1, keepdims=True)
        oc = o - mu
        on = oc * lax.rsqrt(jnp.mean(oc * oc, axis=-1, keepdims=True) + EPS)
        y_ref[rows, cols] = (_silu(g_ref[rows, cols].astype(F32)) * on).astype(y_ref.dtype)


def _retention(seg, b, s, s0f, s0b, tabs):
    c = RET_CHUNK
    n = s // c
    group = math.gcd(n, RET_GROUP)
    ns = n // group
    rows = group * c
    gc, zf, zb, dmat, xif, xib = tabs
    qk_w, vg_w = H_RET * RET_DK, H_RET * RET_DV
    rq0, rk0 = COL_RET // qk_w, (COL_RET + qk_w) // qk_w
    rv0, rg0 = (COL_RET + 2 * qk_w) // vg_w, (COL_RET + 2 * qk_w + vg_w) // vg_w
    state_shape = jax.ShapeDtypeStruct((b, H_RET, n, RET_DK, RET_DV), F32)
    fin_shape = jax.ShapeDtypeStruct((b, H_RET, RET_DK, RET_DV), F32)
    state_block = (1, H_RET, group, RET_DK, RET_DV)
    fin_spec = pl.BlockSpec((1, H_RET, RET_DK, RET_DV), lambda bi, t: (bi, 0, 0, 0))
    table = lambda a: pl.BlockSpec(a.shape, lambda bi, t: (0,) * a.ndim)

    def fwd(bi, t):
        return bi * ns + t

    def bwd(bi, t):
        return bi * ns + (ns - 1 - t)

    in_specs = [
        pl.BlockSpec(memory_space=pltpu.SMEM),
        pl.BlockSpec((rows, qk_w), lambda bi, t: (fwd(bi, t), rk0)),
        pl.BlockSpec((rows, vg_w), lambda bi, t: (fwd(bi, t), rv0)),
        pl.BlockSpec((rows, qk_w), lambda bi, t: (bwd(bi, t), rk0)),
        pl.BlockSpec((rows, vg_w), lambda bi, t: (bwd(bi, t), rv0)),
        table(zf), table(zb), fin_spec, fin_spec,
    ]
    args = [gc, seg, seg, seg, seg, zf, zb, s0f, s0b]
    sf, sb, fin_f, fin_b = pl.pallas_call(
        functools.partial(_ret_scan_body, n_steps=ns, group=group),
        out_shape=(state_shape, state_shape, fin_shape, fin_shape),
        grid=(b, ns),
        in_specs=in_specs,
        out_specs=(pl.BlockSpec(state_block, lambda bi, t: (bi, 0, t, 0, 0)),
                   pl.BlockSpec(state_block, lambda bi, t: (bi, 0, ns - 1 - t, 0, 0)),
                   fin_spec, fin_spec),
        scratch_shapes=[pltpu.VMEM((2, H_RET, RET_DK, RET_DV), F32)],
        compiler_params=_cparams("parallel", "arbitrary"),
        name="ret_scan",
    )(*args)

    in_specs = [
        pl.BlockSpec((rows, qk_w), lambda bi, t: (fwd(bi, t), rq0)),
        pl.BlockSpec((rows, qk_w), lambda bi, t: (fwd(bi, t), rk0)),
        pl.BlockSpec((rows, vg_w), lambda bi, t: (fwd(bi, t), rv0)),
        pl.BlockSpec((rows, vg_w), lambda bi, t: (fwd(bi, t), rg0)),
        pl.BlockSpec(state_block, lambda bi, t: (bi, 0, t, 0, 0)),
        pl.BlockSpec(state_block, lambda bi, t: (bi, 0, t, 0, 0)),
        table(dmat), table(xif), table(xib),
    ]
    args = [seg, seg, seg, seg, sf, sb, dmat, xif, xib]
    y = pl.pallas_call(
        functools.partial(_ret_out_body, group=group),
        out_shape=jax.ShapeDtypeStruct((b * s, vg_w), BF16),
        grid=(b, ns),
        in_specs=in_specs,
        out_specs=pl.BlockSpec((rows, vg_w), lambda bi, t: (fwd(bi, t), 0)),
        compiler_params=_cparams("parallel", "parallel"),
        name="ret_out",
    )(*args)
    return y, fin_f, fin_b


def _retention_tables(decay_f, decay_b):
    c = RET_CHUNK
    lg_f = jax.nn.log_sigmoid(decay_f.astype(F32))
    lg_b = jax.nn.log_sigmoid(decay_b.astype(F32))
    i = jnp.arange(c, dtype=F32)
    dist = i[:, None] - i[None, :]
    d_f = jnp.where(dist >= 0, jnp.exp(jnp.maximum(dist, 0.0)[None] * lg_f[:, None, None]), 0.0)
    d_b = jnp.where(dist < 0, jnp.exp(jnp.maximum(-dist, 0.0)[None] * lg_b[:, None, None]), 0.0)
    dmat = d_f + d_b
    xi_f = jnp.exp((i + 1.0)[None, :] * lg_f[:, None])
    xi_b = jnp.exp((c - i)[None, :] * lg_b[:, None])
    zeta_f = jnp.exp((c - 1.0 - i)[None, :] * lg_f[:, None])
    zeta_b = jnp.exp(i[None, :] * lg_b[:, None])
    gc = jnp.stack([jnp.exp(c * lg_f), jnp.exp(c * lg_b)], axis=1)

    def lanes(tab, width):
        return jnp.broadcast_to(tab[:, :, None], tab.shape + (width,))

    def pair(tab):
        t64 = lanes(tab, RET_DK).reshape(H_RET // 2, 2, c, RET_DK)
        return jnp.concatenate([t64[:, 0], t64[:, 1]], axis=-1)

    return gc, pair(zeta_f), pair(zeta_b), dmat, lanes(xi_f, RET_DV), lanes(xi_b, RET_DV)


def _diff_cache_body(ck_ref, cv_ref, kt_ref, vt_ref):
    dv = 2 * DIFF_HD
    ones = jnp.ones((SUM_ROWS, vt_ref.shape[-1]), BF16)
    for h in range(H_DIFF):
        cols = slice(dv * h, dv * (h + 1))
        kt_ref[0, h, 0] = (ck_ref[0, 0, :, cols] * (DIFF_HD ** -0.5)).astype(BF16)
        vt_ref[0, h, 0, :dv, :] = cv_ref[0, 0, :, cols].T.astype(BF16)
        vt_ref[0, h, 0, dv:, :] = ones


def _diff_cache_prep(cache_k, cache_v, layer, tk):
    b, _, past, width = cache_k.shape
    n = past // tk
    spec = pl.BlockSpec((1, 1, tk, width), lambda bi, c: (bi, layer, c, 0))
    return pl.pallas_call(
        _diff_cache_body,
        out_shape=(jax.ShapeDtypeStruct((b, H_DIFF, n, tk, LANES), BF16),
                   jax.ShapeDtypeStruct((b, H_DIFF, n, 2 * DIFF_HD + SUM_ROWS, tk), BF16)),
        grid=(b, n),
        in_specs=[spec, spec],
        out_specs=(pl.BlockSpec((1, H_DIFF, 1, tk, LANES), lambda bi, c: (bi, 0, c, 0, 0)),
                   pl.BlockSpec((1, H_DIFF, 1, 2 * DIFF_HD + SUM_ROWS, tk), lambda bi, c: (bi, 0, c, 0, 0))),
        compiler_params=_cparams("parallel", "parallel"),
        name="diff_cache_prep",
    )(cache_k, cache_v)


def _mla_cache_body(cc_ref, cp_ref, wk_ref, wv_ref, kt_ref, vt_ref):
    cb = cc_ref[0, 0].astype(BF16)
    kfull = _dot(cb, wk_ref[...])
    vt = _dot(cb, wv_ref[...]).T
    kp = cp_ref[0, 0]
    ones = jnp.ones((SUM_ROWS, vt.shape[-1]), BF16)
    for h in range(H_MLA):
        kt_ref[0, h, 0] = (kfull[:, LANES * h:LANES * (h + 1)] + kp).astype(BF16)
        vt_ref[0, h, 0, :MLA_V, :] = vt[MLA_V * h:MLA_V * (h + 1)].astype(BF16)
        vt_ref[0, h, 0, MLA_V:, :] = ones


def _mla_cache_prep(cache_ckv, cache_kpe, layer, wk, wv, tk):
    b, _, past, _ = cache_ckv.shape
    n = past // tk
    return pl.pallas_call(
        _mla_cache_body,
        out_shape=(jax.ShapeDtypeStruct((b, H_MLA, n, tk, LANES), BF16),
                   jax.ShapeDtypeStruct((b, H_MLA, n, MLA_V + SUM_ROWS, tk), BF16)),
        grid=(b, n),
        in_specs=[pl.BlockSpec((1, 1, tk, KV_LORA), lambda bi, c: (bi, layer, c, 0)),
                  pl.BlockSpec((1, 1, tk, LANES), lambda bi, c: (bi, layer, c, 0)),
                  pl.BlockSpec(wk.shape, lambda bi, c: (0, 0)),
                  pl.BlockSpec(wv.shape, lambda bi, c: (0, 0))],
        out_specs=(pl.BlockSpec((1, H_MLA, 1, tk, LANES), lambda bi, c: (bi, 0, c, 0, 0)),
                   pl.BlockSpec((1, H_MLA, 1, MLA_V + SUM_ROWS, tk), lambda bi, c: (bi, 0, c, 0, 0))),
        compiler_params=_cparams("parallel", "parallel"),
        name="mla_cache_prep",
    )(cache_ckv, cache_kpe, wk, wv)


def _flash_keymajor(chains, nk, dv):
    m, acc, s_next = [], [], []
    for q_t, k_tile, _ in chains:
        n = q_t.shape[1]
        m.append(jnp.full((1, n), NEG_BIG, F32))
        acc.append(jnp.zeros((dv + SUM_ROWS, n), F32))
        s_next.append(_dot(k_tile(0), q_t))
    for i in range(nk):
        for ci, (q_t, k_tile, vt_tile) in enumerate(chains):
            s = s_next[ci]
            if i + 1 < nk:
                s_next[ci] = _dot(k_tile(i + 1), q_t)
            m_new = jnp.maximum(m[ci], jnp.max(s, axis=0, keepdims=True))
            alpha = jnp.exp2(m[ci] - m_new)
            p = jnp.exp2(s - m_new)
            acc[ci] = alpha * acc[ci] + _dot(vt_tile(i), p.astype(BF16))
            m[ci] = m_new
    return [(a[dv:dv + 1], a[:dv]) for a in acc]


def _key_tiles(kv_refs, counts, head):
    index = [(part, i) for part, n in enumerate(counts) for i in range(n)]
    k_tile = lambda i: kv_refs[2 * index[i][0]][0, head, index[i][1]]
    vt_tile = lambda i: kv_refs[2 * index[i][0] + 1][0, head, index[i][1]]
    return k_tile, vt_tile


def _kv_specs(parts, heads, dv, head_block):
    specs, args = [], []
    for kt, vt in parts:
        n, tk = kt.shape[2], kt.shape[3]
        specs += [pl.BlockSpec((1, heads, n, tk, LANES), lambda bi, h, qi: (bi, head_block(h), 0, 0, 0)),
                  pl.BlockSpec((1, heads, n, dv + SUM_ROWS, tk), lambda bi, h, qi: (bi, head_block(h), 0, 0, 0))]
        args += [kt, vt]
    return specs, args


def _diff_attn_body(q_ref, dl_ref, *refs, counts, tq, heads, lam_init):
    kv_refs, y_ref = refs[:-1], refs[-1]
    cq = min(tq, DIFF_CHAIN_Q)
    lane = lax.broadcasted_iota(jnp.int32, (cq, 2 * DIFF_HD), 1)
    chains, where = [], []
    for h in range(heads):
        k_tile, vt_tile = _key_tiles(kv_refs, counts, h)
        for r in range(0, tq, cq):
            q = q_ref[r:r + cq, LANES * h:LANES * (h + 1)].astype(F32)
            q1 = jnp.where(lane < DIFF_HD, q, 0.0)
            q2 = jnp.where(lane >= DIFF_HD, q, 0.0)
            q_t = jnp.concatenate([q1, q2], axis=0).T.astype(BF16)
            chains.append((q_t, k_tile, vt_tile))
            where.append((r, h))
    results = _flash_keymajor(chains, sum(counts), 2 * DIFF_HD)
    dl = dl_ref[...]
    lam = (jnp.exp(jnp.sum(dl[0:1] * dl[1:2], axis=-1, keepdims=True))
           - jnp.exp(jnp.sum(dl[2:3] * dl[3:4], axis=-1, keepdims=True)) + lam_init)
    for (r, h), (l, acc) in zip(where, results):
        o = acc * (1.0 / l)
        o = o[:, :cq] - lam * o[:, cq:]
        o = o * lax.rsqrt(jnp.mean(o * o, axis=0, keepdims=True) + EPS) * (1.0 - lam_init)
        y_ref[r:r + cq, LANES * h:LANES * (h + 1)] = o.T.astype(y_ref.dtype)


def _diff_attention(seg, b, s, kv_parts, dl, lam_init):
    tq = min(DIFF_Q_TILE, s)
    nq = s // tq
    heads = max(1, min(H_DIFF, ATTN_CHAINS * DIFF_CHAIN_Q // tq))
    width = LANES * heads
    q0 = COL_DIFF // width
    kv_specs, kv_args = _kv_specs(kv_parts, heads, 2 * DIFF_HD, lambda h: h)
    in_specs = [pl.BlockSpec((tq, width), lambda bi, h, qi: (bi * nq + qi, q0 + h)),
                pl.BlockSpec((8, 128), lambda bi, h, qi: (0, 0))] + kv_specs
    args = [seg, dl] + kv_args
    return pl.pallas_call(
        functools.partial(_diff_attn_body, counts=tuple(kt.shape[2] for kt, _ in kv_parts), tq=tq, heads=heads,
                          lam_init=lam_init),
        out_shape=jax.ShapeDtypeStruct((b * s, H_DIFF * 2 * DIFF_HD), BF16),
        grid=(b, H_DIFF // heads, nq),
        in_specs=in_specs,
        out_specs=pl.BlockSpec((tq, width), lambda bi, h, qi: (bi * nq + qi, h)),
        compiler_params=_cparams("parallel", "parallel", "arbitrary"),
        name="diff_attn",
    )(*args)


def _mla_attn_body(q_ref, *refs, counts, heads):
    kv_refs, y_ref = refs[:-1], refs[-1]
    tq = q_ref.shape[0]
    cq = min(tq, MLA_CHAIN_Q)
    chains, where = [], []
    for r in range(0, tq, cq):
        for j in range(heads):
            q_t = q_ref[r:r + cq, LANES * j:LANES * (j + 1)].astype(F32).T.astype(BF16)
            chains.append((q_t,) + _key_tiles(kv_refs, counts, j))
        where += [(r, j) for j in range(0, heads, 2)]
    outs = [acc * (1.0 / l) for l, acc in _flash_keymajor(chains, sum(counts), MLA_V)]
    for ci, (r, j) in enumerate(where):
        pair = jnp.concatenate(outs[2 * ci:2 * ci + 2], axis=0)
        y_ref[r:r + cq, MLA_V * j:MLA_V * (j + 2)] = pair.T.astype(y_ref.dtype)


def _mla_attention(q_all, b, s, kv_parts):
    tq = min(MLA_Q_TILE, s)
    nq = s // tq
    heads = max(2, min(H_MLA, ATTN_CHAINS * MLA_CHAIN_Q // tq))
    kv_specs, kv_args = _kv_specs(kv_parts, heads, MLA_V, lambda p: p)
    in_specs = [pl.BlockSpec((tq, heads * LANES), lambda bi, p, qi: (bi * nq + qi, p))] + kv_specs
    args = [q_all] + kv_args
    return pl.pallas_call(
        functools.partial(_mla_attn_body, counts=tuple(kt.shape[2] for kt, _ in kv_parts), heads=heads),
        out_shape=jax.ShapeDtypeStruct((b * s, H_MLA * MLA_V), BF16),
        grid=(b, H_MLA // heads, nq),
        in_specs=in_specs,
        out_specs=pl.BlockSpec((tq, heads * MLA_V), lambda bi, p, qi: (bi * nq + qi, p)),
        compiler_params=_cparams("parallel", "parallel", "arbitrary"),
        name="mla_attn",
    )(*args)


def _merge_body(x_ref, yr_ref, yd_ref, ym_ref, gt_ref, mod_ref, wbr_ref, wo_ref, o_ref):
    merged = None
    for gi, y_ref in enumerate((yr_ref, yd_ref, ym_ref)):
        br = _dot(y_ref[...], wbr_ref[gi])
        term = _sigmoid(gt_ref[:, D_MODEL * gi:D_MODEL * (gi + 1)].astype(F32)) * br
        merged = term if merged is None else merged + term
    out = _dot(merged.astype(BF16), wo_ref[...])
    o_ref[...] = x_ref[...] + mod_ref[0][2:3, :] * out


def _merge(x, yr, yd, ym, seg, mod, modsel, wbr, wo):
    t = x.shape[0]
    tm = min(TOKEN_TILE, t)
    mod_row = _mod_row(modsel, tm)
    row = lambda i: (i, 0)
    return pl.pallas_call(
        _merge_body,
        out_shape=jax.ShapeDtypeStruct((t, D_MODEL), F32),
        grid=(t // tm,),
        in_specs=[pl.BlockSpec((tm, D_MODEL), row),
                  pl.BlockSpec((tm, 512), row), pl.BlockSpec((tm, 512), row), pl.BlockSpec((tm, 512), row),
                  pl.BlockSpec((tm, 3 * D_MODEL), lambda i: (i, COL_GATE // (3 * D_MODEL))),
                  pl.BlockSpec((1, 8, D_MODEL), lambda i: (mod_row(i), 0, 0)),
                  pl.BlockSpec(wbr.shape, lambda i: (0, 0, 0)),
                  pl.BlockSpec(wo.shape, lambda i: (0, 0))],
        out_specs=pl.BlockSpec((tm, D_MODEL), row),
        compiler_params=_cparams("parallel"),
        name="merge",
    )(x, yr, yd, ym, seg, mod, wbr, wo)


def _ffn_body(*refs, final):
    if final:
        x_ref, g_ref, mod_ref, wa_ref, wb_ref, wo_ref, fg_ref, o_ref = refs
    else:
        x_ref, g_ref, mod_ref, wa_ref, wb_ref, wo_ref, o_ref = refs
    m = mod_ref[0]
    x = x_ref[...]
    h = ((_rms(x) * g_ref[...]) * (1.0 + m[4:5, :]) + m[3:4, :]).astype(BF16)
    d_ff = wa_ref.shape[1]
    chunks = [(lo, min(lo + FFN_CHUNK, d_ff)) for lo in range(0, d_ff, FFN_CHUNK)]

    def up(k):
        lo, hi = chunks[k]
        return _dot(h, wa_ref[:, lo:hi]), _dot(h, wb_ref[:, lo:hi])

    nxt = up(0)
    y = None
    for k, (lo, hi) in enumerate(chunks):
        a, b = nxt
        if k + 1 < len(chunks):
            nxt = up(k + 1)
        d = _dot((_silu(a) * b).astype(BF16), wo_ref[lo:hi, :])
        y = d if y is None else y + d
    xn = x + m[5:6, :] * y
    if final:
        xn = _rms(xn) * fg_ref[...]
    o_ref[...] = xn


def _ffn(x, g, mod, modsel, wa, wb, wo, final_g):
    t = x.shape[0]
    tm = min(TOKEN_TILE, t)
    mod_row = _mod_row(modsel, tm)
    final = final_g is not None
    resident = lambda a: pl.BlockSpec(a.shape, lambda i: (0, 0), pipeline_mode=pl.Buffered(1))
    in_specs = [pl.BlockSpec((tm, D_MODEL), lambda i: (i, 0)),
                pl.BlockSpec((1, D_MODEL), lambda i: (0, 0)),
                pl.BlockSpec((1, 8, D_MODEL), lambda i: (mod_row(i), 0, 0)),
                resident(wa), resident(wb), resident(wo)]
    args = [x, g.reshape(1, D_MODEL), mod, wa, wb, wo]
    if final:
        in_specs.append(pl.BlockSpec((1, D_MODEL), lambda i: (0, 0)))
        args.append(final_g.reshape(1, D_MODEL))
    return pl.pallas_call(
        functools.partial(_ffn_body, final=final),
        out_shape=jax.ShapeDtypeStruct((t, D_MODEL), F32),
        grid=(t // tm,),
        in_specs=in_specs,
        out_specs=pl.BlockSpec((tm, D_MODEL), lambda i: (i, 0)),
        compiler_params=_cparams("parallel"),
        name="ffn",
    )(*args)


def _axial_angles(n_tokens, rot_dim):
    t = jnp.arange(n_tokens)
    row = (t // GRID_W).astype(F32)
    col = (t % GRID_W).astype(F32)
    nf = rot_dim // 4
    inv = ROPE_BASE ** (-jnp.arange(nf, dtype=F32) / nf)
    ang = jnp.concatenate([row[:, None] * inv, col[:, None] * inv], axis=-1)
    return jnp.cos(ang), jnp.sin(ang)


def _rope_tables_head64(n_tokens):
    cos, sin = _axial_angles(n_tokens, 64)
    zero = jnp.zeros_like(sin)
    c = jnp.tile(jnp.concatenate([cos, cos], axis=-1), (1, 2))
    sa = jnp.tile(jnp.concatenate([-sin, zero], axis=-1), (1, 2))
    sb = jnp.tile(jnp.concatenate([zero, sin], axis=-1), (1, 2))
    return c, sa, sb


def _rope_tables_mla(n_tokens):
    cos, sin = _axial_angles(n_tokens, MLA_ROPE)
    z16 = jnp.zeros_like(sin)
    one64 = jnp.ones((n_tokens, MLA_NOPE), F32)
    z64 = jnp.zeros((n_tokens, MLA_NOPE), F32)
    z32 = jnp.zeros((n_tokens, 32), F32)
    c = jnp.concatenate([one64, cos, cos, z32], axis=-1)
    sa = jnp.concatenate([z64, -sin, z16, z32], axis=-1)
    sb = jnp.concatenate([z64, z16, sin, z32], axis=-1)
    return c, sa, sb


def _layer_weights(l, w_in, w_uq, w_ukv, w_branch, w_out, w_ffn_in, w_ffn_out):
    wi = w_in[l]
    d = wi.shape[0]
    w_ret = wi[:, 0:1536]
    w_diff = wi[:, 1536:3072]
    w_mla = jnp.concatenate([wi[:, 3456:3712], jnp.zeros((d, 64), wi.dtype), wi[:, 3712:3744],
                             jnp.zeros((d, 32), wi.dtype), wi[:, 3072:3456]], axis=1)
    w_gate = wi[:, 3744:]
    w_proj = jnp.concatenate([w_gate, w_ret, w_diff, w_mla], axis=1)
    assert w_proj.shape[1] == N_PROJ
    hd = MLA_NOPE + MLA_ROPE
    wq = jnp.pad(w_uq[l].reshape(Q_LORA, H_MLA, hd), ((0, 0), (0, 0), (0, LANES - hd))).reshape(Q_LORA, H_MLA * LANES)
    wkv = w_ukv[l].reshape(KV_LORA, H_MLA, MLA_NOPE + MLA_V)
    wk = jnp.pad(wkv[:, :, :MLA_NOPE], ((0, 0), (0, 0), (0, LANES - MLA_NOPE))).reshape(KV_LORA, H_MLA * LANES)
    wv = wkv[:, :, MLA_NOPE:].reshape(KV_LORA, H_MLA * MLA_V)
    wa, wb = w_ffn_in[l][:, :D_FF], w_ffn_in[l][:, D_FF:]
    cast = lambda a: a.astype(BF16)
    return dict(proj=cast(w_proj), wq=cast(wq), wk=cast(wk), wv=cast(wv), wbr=cast(w_branch[l]), wo=cast(w_out[l]),
                wa=cast(wa), wb=cast(wb), wf=cast(w_ffn_out[l]))


def _mixer_layer(x, b, s, l, lw, mod, modsel, norm1_g, norm2_g, tabs, mla_q_norm, mla_kv_norm, dl, lam_init,
                 s0f, s0b, caches, ropes, ctx, final_g):
    tk = min(KEY_TILE, s)
    (seg, q_mla, kd, vd, km, vm), ctx_out = _input_proj(x, b, s, tk, norm1_g, mod, modsel, lw, mla_q_norm,
                                                       mla_kv_norm, ropes, ctx)

    y_ret, fin_f, fin_b = _retention(seg, b, s, s0f, s0b, tabs)

    diff_parts, mla_parts = [(kd, vd)], [(km, vm)]
    if caches is not None:
        cache_k, cache_v, cache_ckv, cache_kpe = caches
        diff_parts.insert(0, _diff_cache_prep(cache_k, cache_v, l, tk))
        mla_parts.insert(0, _mla_cache_prep(cache_ckv, cache_kpe, l, lw["wk"], lw["wv"], tk))
    y_diff = _diff_attention(seg, b, s, diff_parts, dl, lam_init)
    y_mla = _mla_attention(q_mla, b, s, mla_parts)

    x = _merge(x, y_ret, y_diff, y_mla, seg, mod, modsel, lw["wbr"], lw["wo"])
    x = _ffn(x, norm2_g, mod, modsel, lw["wa"], lw["wb"], lw["wf"], final_g)
    return x, fin_f, fin_b, ctx_out


def kernel(x_prompt, x_sample, state_ret_fwd, state_ret_bwd, cache_diff_k, cache_diff_v, cache_mla_ckv,
           cache_mla_kpe, c, c_ctx, norm1_g, norm2_g, w_ada, b_ada, w_in, ret_decay_fwd, ret_decay_bwd,
           diff_lambda, mla_q_norm, mla_kv_norm, w_uq, w_ukv, w_branch, w_out, w_ffn_in, w_ffn_out, final_g):
    bp, sp, d = x_prompt.shape
    bs, ss, _ = x_sample.shape
    depth = w_in.shape[0]
    past = cache_diff_k.shape[2]

    n_rows = -(-(1 + bs) // 8) * 8
    cond = jnp.zeros((n_rows, d), F32).at[0].set(c_ctx).at[1:1 + bs].set(c)
    mod_all = _modulation(cond, w_ada, b_ada)
    mod_all = jnp.pad(mod_all.reshape(depth, n_rows, 6, d), ((0, 0), (0, 0), (0, 2), (0, 0)))

    sel_prompt = (0, bp * sp)
    sel_sample = (1, ss)

    ropes = (_rope_tables_head64(ss), _rope_tables_mla(ss))
    cache_k = cache_diff_k.reshape(bs, depth, past, H_DIFF * 2 * DIFF_HD)
    cache_v = cache_diff_v.reshape(bs, depth, past, H_DIFF * 2 * DIFF_HD)
    cache_kpe = jnp.pad(cache_mla_kpe, ((0, 0), (0, 0), (0, 0), (MLA_NOPE, LANES - MLA_NOPE - MLA_ROPE)))
    zero_state = jnp.zeros((bp, H_RET, RET_DK, RET_DV), F32)

    xp = x_prompt.reshape(bp * sp, d)
    xs = x_sample.reshape(bs * ss, d)
    ret_f, ret_b, ctx_arrays = [], [], None
    for l in range(depth):
        lw = _layer_weights(l, w_in, w_uq, w_ukv, w_branch, w_out, w_ffn_in, w_ffn_out)
        lam_init = 0.8 - 0.6 * math.exp(-0.3 * l)
        tabs = _retention_tables(ret_decay_fwd[l], ret_decay_bwd[l])
        dl = jnp.pad(diff_lambda[l], ((0, 4), (0, LANES - DIFF_HD)))
        fg = final_g if l == depth - 1 else None
        common = (norm1_g[l], norm2_g[l], tabs, mla_q_norm[l], mla_kv_norm[l], dl, lam_init)

        xp, fin_f, fin_b, ctx_arrays = _mixer_layer(xp, bp, sp, l, lw, mod_all[l], sel_prompt, *common, zero_state,
                                                    zero_state, None, None, (l, depth, ctx_arrays), fg)
        ret_f.append(fin_f)
        ret_b.append(fin_b)

        xs, _, _, _ = _mixer_layer(xs, bs, ss, l, lw, mod_all[l], sel_sample, *common, state_ret_fwd[:, l],
                                   state_ret_bwd[:, l], (cache_k, cache_v, cache_mla_ckv, cache_kpe), ropes, None, fg)

    dk_all, dv_all, ckv_all, kpe_all = ctx_arrays
    head_shape = (bp, depth, sp, H_DIFF, 2 * DIFF_HD)
    return (xp.reshape(bp, sp, d), xs.reshape(bs, ss, d), jnp.stack(ret_f, axis=1), jnp.stack(ret_b, axis=1),
            dk_all.reshape(head_shape), dv_all.reshape(head_shape), ckv_all, kpe_all)
```

```python
import functools
import math

import jax
import jax.numpy as jnp
from jax import lax
from jax.experimental import pallas as pl
from jax.experimental.pallas import tpu as pltpu

F32 = jnp.float32
BF16 = jnp.bfloat16

D_MODEL = 1024
GRID_W = 64
H_RET, RET_DK, RET_DV, RET_CHUNK = 4, 64, 128, 128
H_DIFF, DIFF_HD = 4, 64
H_MLA, MLA_NOPE, MLA_ROPE, MLA_V = 8, 64, 32, 64
Q_LORA, KV_LORA = 384, 256
D_FF = 2816
ROPE_BASE = 10000.0
EPS = 1e-6
MLA_SCALE = (MLA_NOPE + MLA_ROPE) ** -0.5
LANES = 128
NEG_BIG = -1e30
LOG2_E = math.log2(math.e)

TOKEN_TILE = 512
FFN_CHUNK = 1024
DIFF_Q_TILE = 2048
DIFF_CHAIN_Q = 256
MLA_Q_TILE = 2048
MLA_CHAIN_Q = 512
ATTN_CHAINS = 8
KEY_TILE = 512
RET_GROUP = 8
SUM_ROWS = 16
PROJ_TILE_N = 2304

N_PROJ = 6912
COL_GATE = 0
COL_RET = 3072
COL_DIFF = 4608
COL_MLA = 6144
VMEM_LIMIT = 56 * 1024 * 1024


def _cparams(*sem):
    return pltpu.CompilerParams(dimension_semantics=sem, vmem_limit_bytes=VMEM_LIMIT)


def _mod_row(modsel, tm):
    base, tokens_per_row = modsel
    assert tokens_per_row % tm == 0
    return lambda i: base + (i * tm) // tokens_per_row


def _dot(a, b):
    return jnp.dot(a, b, preferred_element_type=F32)


def _dot_nt(a, b):
    return lax.dot_general(a, b, (((1,), (1,)), ((), ())), preferred_element_type=F32)


def _rms(x):
    return x * lax.rsqrt(jnp.mean(x * x, axis=-1, keepdims=True) + EPS)


def _silu(x):
    return x * (1.0 / (1.0 + jnp.exp(-x)))


def _sigmoid(x):
    return 1.0 / (1.0 + jnp.exp(-x))


def _rope(x, c_ref, sa_ref, sb_ref, shift_a, shift_b, rows=slice(None)):
    return (x * c_ref[rows, :] + pltpu.roll(x, shift_a, 1) * sa_ref[rows, :]
            + pltpu.roll(x, shift_b, 1) * sb_ref[rows, :])


def _mod_body(c_ref, w_ref, b_ref, o_ref):
    o_ref[0] = _dot(_silu(c_ref[...]).astype(BF16), w_ref[0].astype(BF16)) + b_ref[0]


def _modulation(cond, w_ada, b_ada):
    depth, d, n = w_ada.shape
    rows = cond.shape[0]
    tn = 1536
    return pl.pallas_call(
        _mod_body,
        out_shape=jax.ShapeDtypeStruct((depth, rows, n), F32),
        grid=(depth, n // tn),
        in_specs=[pl.BlockSpec((rows, d), lambda l, j: (0, 0)),
                  pl.BlockSpec((1, d, tn), lambda l, j: (l, 0, j)),
                  pl.BlockSpec((1, 1, tn), lambda l, j: (l, 0, j))],
        out_specs=pl.BlockSpec((1, rows, tn), lambda l, j: (l, 0, j)),
        compiler_params=_cparams("parallel", "parallel"),
        name="adaln_mod",
    )(cond, w_ada, b_ada.reshape(depth, 1, n))


def _input_proj_body(*refs, has_rope, n_ctx_in, tn):
    refs = list(refs)
    x_ref, g_ref, mod_ref, w_ref, qn_ref, kvn_ref, wq_ref, wk_ref, wv_ref = refs[:9]
    refs = refs[9:]
    if has_rope:
        rope64, rope_mla = refs[:3], refs[3:6]
        refs = refs[6:]
    refs = refs[n_ctx_in:]
    seg_ref, qm_ref, kd_ref, vd_ref, km_ref, vm_ref = refs[:6]
    ctx_refs = refs[6:]

    def rot64(blk):
        return _rope(blk, *rope64, 96, 32) if has_rope else blk

    def rot_mla(blk):
        return _rope(blk, *rope_mla, 112, 16) if has_rope else blk

    m = mod_ref[0]
    h = ((_rms(x_ref[...]) * g_ref[...]) * (1.0 + m[1:2, :]) + m[0:1, :]).astype(BF16)
    tm = h.shape[0]
    dv = 2 * DIFF_HD
    ones = jnp.ones((SUM_ROWS, tm), BF16)

    tiles = [COL_DIFF] + [c0 for c0 in range(0, N_PROJ, tn) if c0 != COL_DIFF]
    r = {c0: _dot(h, w_ref[:, c0:c0 + tn]) for c0 in tiles[:2]}

    def take(col, width=LANES):
        c0 = col // tn * tn
        return r[c0][:, col - c0:col - c0 + width]

    def put(col, blk):
        seg_ref[:, col:col + LANES] = blk.astype(seg_ref.dtype)

    cn = _rms(take(COL_MLA, KV_LORA)) * kvn_ref[...]
    cb = cn.astype(BF16)
    kfull = _dot(cb, wk_ref[...])
    vt = _dot(cb, wv_ref[...]).T
    q = _dot((_rms(take(COL_MLA + KV_LORA + LANES, Q_LORA)) * qn_ref[...]).astype(BF16), wq_ref[...])
    for c0 in tiles[2:]:
        r[c0] = _dot(h, w_ref[:, c0:c0 + tn])

    for c0 in tiles:
        seg_ref[:, c0:c0 + tn] = r[c0].astype(seg_ref.dtype)
    for i in range(4):
        put(COL_RET + LANES * i, rot64(take(COL_RET + LANES * i)))
    for hd in range(H_DIFF):
        put(COL_DIFF + dv * hd, rot64(take(COL_DIFF + dv * hd)) * LOG2_E)
        k = rot64(take(COL_DIFF + 512 + dv * hd))
        put(COL_DIFF + 512 + dv * hd, k)
        kd_ref[0, hd, 0] = (k * (DIFF_HD ** -0.5)).astype(BF16)
        vd_ref[0, hd, 0, :dv, :] = take(COL_DIFF + 1024 + dv * hd).T.astype(BF16)
        vd_ref[0, hd, 0, dv:, :] = ones
    kp = rot_mla(take(COL_MLA + KV_LORA))
    put(COL_MLA + KV_LORA, kp)
    if ctx_refs:
        dk_ref, dv_ref, ckv_ref, kpe_ref = ctx_refs
        dk_ref[0, 0] = take(COL_DIFF + 512, 512)
        dv_ref[0, 0] = take(COL_DIFF + 1024, 512)
        ckv_ref[0, 0] = cn
        kpe_ref[0, 0] = kp[:, MLA_NOPE:MLA_NOPE + MLA_ROPE]
    for hm in range(H_MLA):
        cols = slice(LANES * hm, LANES * (hm + 1))
        km_ref[0, hm, 0] = (kfull[:, cols] + kp).astype(BF16)
        vm_ref[0, hm, 0, :MLA_V, :] = vt[MLA_V * hm:MLA_V * (hm + 1)].astype(BF16)
        vm_ref[0, hm, 0, MLA_V:, :] = ones
        qm_ref[:, cols] = (rot_mla(q[:, cols]) * (MLA_SCALE * LOG2_E)).astype(BF16)


def _input_proj(x, b, s, tk, g, mod, modsel, lw, qn, kvn, rope, ctx):
    t = b * s
    n_new = s // tk
    tn = PROJ_TILE_N
    assert COL_DIFF % tn == 0 and N_PROJ - COL_DIFF == tn and s % tk == 0
    mod_row = _mod_row(modsel, tk)
    resident = lambda a: pl.BlockSpec(a.shape, lambda i: (0,) * a.ndim, pipeline_mode=pl.Buffered(1))
    qn2, kvn2 = qn.reshape(1, Q_LORA), kvn.reshape(1, KV_LORA)
    in_specs = [pl.BlockSpec((tk, D_MODEL), lambda i: (i, 0)),
                pl.BlockSpec((1, D_MODEL), lambda i: (0, 0)),
                pl.BlockSpec((1, 8, D_MODEL), lambda i: (mod_row(i), 0, 0)),
                resident(lw["proj"]), resident(qn2), resident(kvn2),
                resident(lw["wq"]), resident(lw["wk"]), resident(lw["wv"])]
    args = [x, g.reshape(1, D_MODEL), mod, lw["proj"], qn2, kvn2, lw["wq"], lw["wk"], lw["wv"]]
    if rope is not None:
        in_specs += [pl.BlockSpec((tk, LANES), lambda i: (i % n_new, 0))] * 6
        args += list(rope[0]) + list(rope[1])
    aliases = {}
    n_ctx_in = 0
    if ctx is not None and ctx[2] is not None:
        n_ctx_in = len(ctx[2])
        aliases = {len(args) + k: 6 + k for k in range(n_ctx_in)}
        in_specs += [pl.BlockSpec(memory_space=pl.ANY)] * n_ctx_in
        args += list(ctx[2])
    tile = lambda i: (i // n_new, 0, i % n_new, 0, 0)
    out_shape = [jax.ShapeDtypeStruct((t, N_PROJ), BF16),
                 jax.ShapeDtypeStruct((t, H_MLA * LANES), BF16),
                 jax.ShapeDtypeStruct((b, H_DIFF, n_new, tk, LANES), BF16),
                 jax.ShapeDtypeStruct((b, H_DIFF, n_new, 2 * DIFF_HD + SUM_ROWS, tk), BF16),
                 jax.ShapeDtypeStruct((b, H_MLA, n_new, tk, LANES), BF16),
                 jax.ShapeDtypeStruct((b, H_MLA, n_new, MLA_V + SUM_ROWS, tk), BF16)]
    out_specs = [pl.BlockSpec((tk, N_PROJ), lambda i: (i, 0)),
                 pl.BlockSpec((tk, H_MLA * LANES), lambda i: (i, 0)),
                 pl.BlockSpec((1, H_DIFF, 1, tk, LANES), tile),
                 pl.BlockSpec((1, H_DIFF, 1, 2 * DIFF_HD + SUM_ROWS, tk), tile),
                 pl.BlockSpec((1, H_MLA, 1, tk, LANES), tile),
                 pl.BlockSpec((1, H_MLA, 1, MLA_V + SUM_ROWS, tk), tile)]
    if ctx is not None:
        layer, depth, _ = ctx
        for width in (H_DIFF * 2 * DIFF_HD, H_DIFF * 2 * DIFF_HD, KV_LORA, MLA_ROPE):
            out_shape.append(jax.ShapeDtypeStruct((b, depth, s, width), F32))
            out_specs.append(pl.BlockSpec((1, 1, tk, width), lambda i: (i // n_new, layer, i % n_new, 0)))
    outs = pl.pallas_call(
        functools.partial(_input_proj_body, has_rope=rope is not None, n_ctx_in=n_ctx_in, tn=tn),
        out_shape=tuple(out_shape),
        grid=(t // tk,),
        in_specs=in_specs,
        out_specs=tuple(out_specs),
        input_output_aliases=aliases,
        compiler_params=_cparams("parallel"),
        name="input_proj",
    )(*args)
    return tuple(outs[:6]), tuple(outs[6:])


def _ret_scan_body(gc_ref, kf_ref, vf_ref, kb_ref, vb_ref, zf_ref, zb_ref, s0f_ref, s0b_ref,
                   sf_ref, sb_ref, finf_ref, finb_ref, st_ref, *, n_steps, group):
    t = pl.program_id(1)
    c = RET_CHUNK

    @pl.when(t == 0)
    def _():
        st_ref[0] = s0f_ref[0]
        st_ref[1] = s0b_ref[0]

    def local_terms(gi, k_ref, v_ref, z_ref):
        rows = slice(gi * c, (gi + 1) * c)
        terms = []
        for p in range(H_RET // 2):
            k = k_ref[rows, LANES * p:LANES * (p + 1)].astype(F32)
            kz_t = (k * (RET_DK ** -0.5) * z_ref[p]).T.astype(BF16)
            for j in range(2):
                v = v_ref[rows, RET_DV * (2 * p + j):RET_DV * (2 * p + j + 1)].astype(BF16)
                terms.append(_dot(kz_t[RET_DK * j:RET_DK * (j + 1)], v))
        return terms

    loc_f = [local_terms(gi, kf_ref, vf_ref, zf_ref) for gi in range(group)]
    loc_b = [local_terms(gi, kb_ref, vb_ref, zb_ref) for gi in range(group)]
    ascending = list(range(group))
    for d, loc, out_ref, order in ((0, loc_f, sf_ref, ascending), (1, loc_b, sb_ref, ascending[::-1])):
        for h in range(H_RET):
            decay = gc_ref[h, d]
            st = st_ref[d, h]
            for gi in order:
                out_ref[0, h, gi] = st.astype(out_ref.dtype)
                st = decay * st + loc[gi][h]
            st_ref[d, h] = st

    @pl.when(t == n_steps - 1)
    def _():
        finf_ref[0] = st_ref[0]
        finb_ref[0] = st_ref[1]


def _ret_out_body(q_ref, k_ref, v_ref, g_ref, sf_ref, sb_ref, dm_ref, xif_ref, xib_ref, y_ref, *, group):
    c = RET_CHUNK
    lane = lax.broadcasted_iota(jnp.int32, (c, 2 * RET_DK), 1)
    stage1 = []
    for gi in range(group):
        rows = slice(gi * c, (gi + 1) * c)
        for p in range(H_RET // 2):
            q = q_ref[rows, LANES * p:LANES * (p + 1)].astype(F32)
            k = k_ref[rows, LANES * p:LANES * (p + 1)].astype(F32)
            kb = (k * (RET_DK ** -0.5)).astype(BF16)
            s_f = sf_ref[0, 2 * p:2 * p + 2, gi].reshape(2 * RET_DK, RET_DV).astype(BF16)
            s_b = sb_ref[0, 2 * p:2 * p + 2, gi].reshape(2 * RET_DK, RET_DV).astype(BF16)
            for j in range(2):
                h = 2 * p + j
                in_head = (lane >= RET_DK * j) & (lane < RET_DK * (j + 1))
                qm = jnp.where(in_head, q, 0.0).astype(BF16)
                sc = (_dot_nt(qm, kb) * dm_ref[h]).astype(BF16)
                cross = _dot(qm, s_f) * xif_ref[h] + _dot(qm, s_b) * xib_ref[h]
                stage1.append((rows, h, sc, cross))
    for rows, h, sc, cross in stage1:
        cols = slice(RET_DV * h, RET_DV * (h + 1))
        o = _dot(sc, v_ref[rows, cols].astype(BF16)) + cross
        mu = jnp.mean(o, axis=-1, keepdims=True)
        oc = o - mu
        on = oc * lax.rsqrt(jnp.mean(oc * oc, axis=-1, keepdims=True) + EPS)
        y_ref[rows, cols] = (_silu(g_ref[rows, cols].astype(F32)) * on).astype(y_ref.dtype)


def _retention(seg, b, s, s0f, s0b, tabs):
    c = RET_CHUNK
    n = s // c
    group = math.gcd(n, RET_GROUP)
    ns = n // group
    rows = group * c
    gc, zf, zb, dmat, xif, xib = tabs
    qk_w, vg_w = H_RET * RET_DK, H_RET * RET_DV
    rq0, rk0 = COL_RET // qk_w, (COL_RET + qk_w) // qk_w
    rv0, rg0 = (COL_RET + 2 * qk_w) // vg_w, (COL_RET + 2 * qk_w + vg_w) // vg_w
    state_shape = jax.ShapeDtypeStruct((b, H_RET, n, RET_DK, RET_DV), BF16)
    fin_shape = jax.ShapeDtypeStruct((b, H_RET, RET_DK, RET_DV), F32)
    state_block = (1, H_RET, group, RET_DK, RET_DV)
    fin_spec = pl.BlockSpec((1, H_RET, RET_DK, RET_DV), lambda bi, t: (bi, 0, 0, 0))
    table = lambda a: pl.BlockSpec(a.shape, lambda bi, t: (0,) * a.ndim)

    def fwd(bi, t):
        return bi * ns + t

    def bwd(bi, t):
        return bi * ns + (ns - 1 - t)

    in_specs = [
        pl.BlockSpec(memory_space=pltpu.SMEM),
        pl.BlockSpec((rows, qk_w), lambda bi, t: (fwd(bi, t), rk0)),
        pl.BlockSpec((rows, vg_w), lambda bi, t: (fwd(bi, t), rv0)),
        pl.BlockSpec((rows, qk_w), lambda bi, t: (bwd(bi, t), rk0)),
        pl.BlockSpec((rows, vg_w), lambda bi, t: (bwd(bi, t), rv0)),
        table(zf), table(zb), fin_spec, fin_spec,
    ]
    args = [gc, seg, seg, seg, seg, zf, zb, s0f, s0b]
    sf, sb, fin_f, fin_b = pl.pallas_call(
        functools.partial(_ret_scan_body, n_steps=ns, group=group),
        out_shape=(state_shape, state_shape, fin_shape, fin_shape),
        grid=(b, ns),
        in_specs=in_specs,
        out_specs=(pl.BlockSpec(state_block, lambda bi, t: (bi, 0, t, 0, 0)),
                   pl.BlockSpec(state_block, lambda bi, t: (bi, 0, ns - 1 - t, 0, 0)),
                   fin_spec, fin_spec),
        scratch_shapes=[pltpu.VMEM((2, H_RET, RET_DK, RET_DV), F32)],
        compiler_params=_cparams("parallel", "arbitrary"),
        name="ret_scan",
    )(*args)

    in_specs = [
        pl.BlockSpec((rows, qk_w), lambda bi, t: (fwd(bi, t), rq0)),
        pl.BlockSpec((rows, qk_w), lambda bi, t: (fwd(bi, t), rk0)),
        pl.BlockSpec((rows, vg_w), lambda bi, t: (fwd(bi, t), rv0)),
        pl.BlockSpec((rows, vg_w), lambda bi, t: (fwd(bi, t), rg0)),
        pl.BlockSpec(state_block, lambda bi, t: (bi, 0, t, 0, 0)),
        pl.BlockSpec(state_block, lambda bi, t: (bi, 0, t, 0, 0)),
        table(dmat), table(xif), table(xib),
    ]
    args = [seg, seg, seg, seg, sf, sb, dmat, xif, xib]
    y = pl.pallas_call(
        functools.partial(_ret_out_body, group=group),
        out_shape=jax.ShapeDtypeStruct((b * s, vg_w), BF16),
        grid=(b, ns),
        in_specs=in_specs,
        out_specs=pl.BlockSpec((rows, vg_w), lambda bi, t: (fwd(bi, t), 0)),
        compiler_params=_cparams("parallel", "parallel"),
        name="ret_out",
    )(*args)
    return y, fin_f, fin_b


def _retention_tables(decay_f, decay_b):
    c = RET_CHUNK
    lg_f = jax.nn.log_sigmoid(decay_f.astype(F32))
    lg_b = jax.nn.log_sigmoid(decay_b.astype(F32))
    i = jnp.arange(c, dtype=F32)
    dist = i[:, None] - i[None, :]
    d_f = jnp.where(dist >= 0, jnp.exp(jnp.maximum(dist, 0.0)[None] * lg_f[:, None, None]), 0.0)
    d_b = jnp.where(dist < 0, jnp.exp(jnp.maximum(-dist, 0.0)[None] * lg_b[:, None, None]), 0.0)
    dmat = d_f + d_b
    xi_f = jnp.exp((i + 1.0)[None, :] * lg_f[:, None])
    xi_b = jnp.exp((c - i)[None, :] * lg_b[:, None])
    zeta_f = jnp.exp((c - 1.0 - i)[None, :] * lg_f[:, None])
    zeta_b = jnp.exp(i[None, :] * lg_b[:, None])
    gc = jnp.stack([jnp.exp(c * lg_f), jnp.exp(c * lg_b)], axis=1)

    def lanes(tab, width):
        return jnp.broadcast_to(tab[:, :, None], tab.shape + (width,))

    def pair(tab):
        t64 = lanes(tab, RET_DK).reshape(H_RET // 2, 2, c, RET_DK)
        return jnp.concatenate([t64[:, 0], t64[:, 1]], axis=-1)

    return gc, pair(zeta_f), pair(zeta_b), dmat, lanes(xi_f, RET_DV), lanes(xi_b, RET_DV)


def _diff_cache_body(ck_ref, cv_ref, kt_ref, vt_ref):
    dv = 2 * DIFF_HD
    ones = jnp.ones((SUM_ROWS, vt_ref.shape[-1]), BF16)
    for h in range(H_DIFF):
        cols = slice(dv * h, dv * (h + 1))
        kt_ref[0, h, 0] = (ck_ref[0, 0, :, cols] * (DIFF_HD ** -0.5)).astype(BF16)
        vt_ref[0, h, 0, :dv, :] = cv_ref[0, 0, :, cols].T.astype(BF16)
        vt_ref[0, h, 0, dv:, :] = ones


def _diff_cache_prep(cache_k, cache_v, layer, tk):
    b, _, past, width = cache_k.shape
    n = past // tk
    spec = pl.BlockSpec((1, 1, tk, width), lambda bi, c: (bi, layer, c, 0))
    return pl.pallas_call(
        _diff_cache_body,
        out_shape=(jax.ShapeDtypeStruct((b, H_DIFF, n, tk, LANES), BF16),
                   jax.ShapeDtypeStruct((b, H_DIFF, n, 2 * DIFF_HD + SUM_ROWS, tk), BF16)),
        grid=(b, n),
        in_specs=[spec, spec],
        out_specs=(pl.BlockSpec((1, H_DIFF, 1, tk, LANES), lambda bi, c: (bi, 0, c, 0, 0)),
                   pl.BlockSpec((1, H_DIFF, 1, 2 * DIFF_HD + SUM_ROWS, tk), lambda bi, c: (bi, 0, c, 0, 0))),
        compiler_params=_cparams("parallel", "parallel"),
        name="diff_cache_prep",
    )(cache_k, cache_v)


def _mla_cache_body(cc_ref, cp_ref, wk_ref, wv_ref, kt_ref, vt_ref):
    cb = cc_ref[0, 0].astype(BF16)
    kfull = _dot(cb, wk_ref[...])
    vt = _dot(cb, wv_ref[...]).T
    kp = cp_ref[0, 0]
    ones = jnp.ones((SUM_ROWS, vt.shape[-1]), BF16)
    for h in range(H_MLA):
        kt_ref[0, h, 0] = (kfull[:, LANES * h:LANES * (h + 1)] + kp).astype(BF16)
        vt_ref[0, h, 0, :MLA_V, :] = vt[MLA_V * h:MLA_V * (h + 1)].astype(BF16)
        vt_ref[0, h, 0, MLA_V:, :] = ones


def _mla_cache_prep(cache_ckv, cache_kpe, layer, wk, wv, tk):
    b, _, past, _ = cache_ckv.shape
    n = past // tk
    return pl.pallas_call(
        _mla_cache_body,
        out_shape=(jax.ShapeDtypeStruct((b, H_MLA, n, tk, LANES), BF16),
                   jax.ShapeDtypeStruct((b, H_MLA, n, MLA_V + SUM_ROWS, tk), BF16)),
        grid=(b, n),
        in_specs=[pl.BlockSpec((1, 1, tk, KV_LORA), lambda bi, c: (bi, layer, c, 0)),
                  pl.BlockSpec((1, 1, tk, LANES), lambda bi, c: (bi, layer, c, 0)),
                  pl.BlockSpec(wk.shape, lambda bi, c: (0, 0)),
                  pl.BlockSpec(wv.shape, lambda bi, c: (0, 0))],
        out_specs=(pl.BlockSpec((1, H_MLA, 1, tk, LANES), lambda bi, c: (bi, 0, c, 0, 0)),
                   pl.BlockSpec((1, H_MLA, 1, MLA_V + SUM_ROWS, tk), lambda bi, c: (bi, 0, c, 0, 0))),
        compiler_params=_cparams("parallel", "parallel"),
        name="mla_cache_prep",
    )(cache_ckv, cache_kpe, wk, wv)


def _flash_keymajor(chains, nk, dv):
    m, acc, s_next = [], [], []
    for q_t, k_tile, _ in chains:
        n = q_t.shape[1]
        m.append(jnp.full((1, n), NEG_BIG, F32))
        acc.append(jnp.zeros((dv + SUM_ROWS, n), F32))
        s_next.append(_dot(k_tile(0), q_t))
    for i in range(nk):
        for ci, (q_t, k_tile, vt_tile) in enumerate(chains):
            s = s_next[ci]
            if i + 1 < nk:
                s_next[ci] = _dot(k_tile(i + 1), q_t)
            m_new = jnp.maximum(m[ci], jnp.max(s, axis=0, keepdims=True))
            alpha = jnp.exp2(m[ci] - m_new)
            p = jnp.exp2(s - m_new)
            acc[ci] = alpha * acc[ci] + _dot(vt_tile(i), p.astype(BF16))
            m[ci] = m_new
    return [(a[dv:dv + 1], a[:dv]) for a in acc]


def _key_tiles(kv_refs, counts, head):
    index = [(part, i) for part, n in enumerate(counts) for i in range(n)]
    k_tile = lambda i: kv_refs[2 * index[i][0]][0, head, index[i][1]]
    vt_tile = lambda i: kv_refs[2 * index[i][0] + 1][0, head, index[i][1]]
    return k_tile, vt_tile


def _kv_specs(parts, heads, dv, head_block):
    specs, args = [], []
    for kt, vt in parts:
        n, tk = kt.shape[2], kt.shape[3]
        specs += [pl.BlockSpec((1, heads, n, tk, LANES), lambda bi, h, qi: (bi, head_block(h), 0, 0, 0)),
                  pl.BlockSpec((1, heads, n, dv + SUM_ROWS, tk), lambda bi, h, qi: (bi, head_block(h), 0, 0, 0))]
        args += [kt, vt]
    return specs, args


def _diff_attn_body(q_ref, dl_ref, *refs, counts, tq, heads, lam_init):
    kv_refs, y_ref = refs[:-1], refs[-1]
    cq = min(tq, DIFF_CHAIN_Q)
    lane = lax.broadcasted_iota(jnp.int32, (cq, 2 * DIFF_HD), 1)
    chains, where = [], []
    for h in range(heads):
        k_tile, vt_tile = _key_tiles(kv_refs, counts, h)
        for r in range(0, tq, cq):
            q = q_ref[r:r + cq, LANES * h:LANES * (h + 1)].astype(F32)
            q1 = jnp.where(lane < DIFF_HD, q, 0.0)
            q2 = jnp.where(lane >= DIFF_HD, q, 0.0)
            q_t = jnp.concatenate([q1, q2], axis=0).T.astype(BF16)
            chains.append((q_t, k_tile, vt_tile))
            where.append((r, h))
    results = _flash_keymajor(chains, sum(counts), 2 * DIFF_HD)
    dl = dl_ref[...]
    lam = (jnp.exp(jnp.sum(dl[0:1] * dl[1:2], axis=-1, keepdims=True))
           - jnp.exp(jnp.sum(dl[2:3] * dl[3:4], axis=-1, keepdims=True)) + lam_init)
    for (r, h), (l, acc) in zip(where, results):
        o = acc * (1.0 / l)
        o = o[:, :cq] - lam * o[:, cq:]
        o = o * lax.rsqrt(jnp.mean(o * o, axis=0, keepdims=True) + EPS) * (1.0 - lam_init)
        y_ref[r:r + cq, LANES * h:LANES * (h + 1)] = o.T.astype(y_ref.dtype)


def _diff_attention(seg, b, s, kv_parts, dl, lam_init):
    tq = min(DIFF_Q_TILE, s)
    nq = s // tq
    heads = max(1, min(H_DIFF, ATTN_CHAINS * DIFF_CHAIN_Q // tq))
    width = LANES * heads
    q0 = COL_DIFF // width
    kv_specs, kv_args = _kv_specs(kv_parts, heads, 2 * DIFF_HD, lambda h: h)
    in_specs = [pl.BlockSpec((tq, width), lambda bi, h, qi: (bi * nq + qi, q0 + h)),
                pl.BlockSpec((8, 128), lambda bi, h, qi: (0, 0))] + kv_specs
    args = [seg, dl] + kv_args
    return pl.pallas_call(
        functools.partial(_diff_attn_body, counts=tuple(kt.shape[2] for kt, _ in kv_parts), tq=tq, heads=heads,
                          lam_init=lam_init),
        out_shape=jax.ShapeDtypeStruct((b * s, H_DIFF * 2 * DIFF_HD), BF16),
        grid=(b, H_DIFF // heads, nq),
        in_specs=in_specs,
        out_specs=pl.BlockSpec((tq, width), lambda bi, h, qi: (bi * nq + qi, h)),
        compiler_params=_cparams("parallel", "parallel", "arbitrary"),
        name="diff_attn",
    )(*args)


def _mla_attn_body(q_ref, *refs, counts, heads):
    kv_refs, y_ref = refs[:-1], refs[-1]
    tq = q_ref.shape[0]
    cq = min(tq, MLA_CHAIN_Q)
    chains, where = [], []
    for r in range(0, tq, cq):
        for j in range(heads):
            q_t = q_ref[r:r + cq, LANES * j:LANES * (j + 1)].astype(F32).T.astype(BF16)
            chains.append((q_t,) + _key_tiles(kv_refs, counts, j))
        where += [(r, j) for j in range(0, heads, 2)]
    outs = [acc * (1.0 / l) for l, acc in _flash_keymajor(chains, sum(counts), MLA_V)]
    for ci, (r, j) in enumerate(where):
        pair = jnp.concatenate(outs[2 * ci:2 * ci + 2], axis=0)
        y_ref[r:r + cq, MLA_V * j:MLA_V * (j + 2)] = pair.T.astype(y_ref.dtype)


def _mla_attention(q_all, b, s, kv_parts):
    tq = min(MLA_Q_TILE, s)
    nq = s // tq
    heads = max(2, min(H_MLA, ATTN_CHAINS * MLA_CHAIN_Q // tq))
    kv_specs, kv_args = _kv_specs(kv_parts, heads, MLA_V, lambda p: p)
    in_specs = [pl.BlockSpec((tq, heads * LANES), lambda bi, p, qi: (bi * nq + qi, p))] + kv_specs
    args = [q_all] + kv_args
    return pl.pallas_call(
        functools.partial(_mla_attn_body, counts=tuple(kt.shape[2] for kt, _ in kv_parts), heads=heads),
        out_shape=jax.ShapeDtypeStruct((b * s, H_MLA * MLA_V), BF16),
        grid=(b, H_MLA // heads, nq),
        in_specs=in_specs,
        out_specs=pl.BlockSpec((tq, heads * MLA_V), lambda bi, p, qi: (bi * nq + qi, p)),
        compiler_params=_cparams("parallel", "parallel", "arbitrary"),
        name="mla_attn",
    )(*args)


def _merge_body(x_ref, yr_ref, yd_ref, ym_ref, gt_ref, mod_ref, wbr_ref, wo_ref, o_ref):
    merged = None
    for gi, y_ref in enumerate((yr_ref, yd_ref, ym_ref)):
        br = _dot(y_ref[...], wbr_ref[gi])
        term = _sigmoid(gt_ref[:, D_MODEL * gi:D_MODEL * (gi + 1)].astype(F32)) * br
        merged = term if merged is None else merged + term
    out = _dot(merged.astype(BF16), wo_ref[...])
    o_ref[...] = x_ref[...] + mod_ref[0][2:3, :] * out


def _merge(x, yr, yd, ym, seg, mod, modsel, wbr, wo):
    t = x.shape[0]
    tm = min(TOKEN_TILE, t)
    mod_row = _mod_row(modsel, tm)
    row = lambda i: (i, 0)
    return pl.pallas_call(
        _merge_body,
        out_shape=jax.ShapeDtypeStruct((t, D_MODEL), F32),
        grid=(t // tm,),
        in_specs=[pl.BlockSpec((tm, D_MODEL), row),
                  pl.BlockSpec((tm, 512), row), pl.BlockSpec((tm, 512), row), pl.BlockSpec((tm, 512), row),
                  pl.BlockSpec((tm, 3 * D_MODEL), lambda i: (i, COL_GATE // (3 * D_MODEL))),
                  pl.BlockSpec((1, 8, D_MODEL), lambda i: (mod_row(i), 0, 0)),
                  pl.BlockSpec(wbr.shape, lambda i: (0, 0, 0)),
                  pl.BlockSpec(wo.shape, lambda i: (0, 0))],
        out_specs=pl.BlockSpec((tm, D_MODEL), row),
        compiler_params=_cparams("parallel"),
        name="merge",
    )(x, yr, yd, ym, seg, mod, wbr, wo)


def _ffn_body(*refs, final):
    if final:
        x_ref, g_ref, mod_ref, wa_ref, wb_ref, wo_ref, fg_ref, o_ref = refs
    else:
        x_ref, g_ref, mod_ref, wa_ref, wb_ref, wo_ref, o_ref = refs
    m = mod_ref[0]
    x = x_ref[...]
    h = ((_rms(x) * g_ref[...]) * (1.0 + m[4:5, :]) + m[3:4, :]).astype(BF16)
    d_ff = wa_ref.shape[1]
    chunks = [(lo, min(lo + FFN_CHUNK, d_ff)) for lo in range(0, d_ff, FFN_CHUNK)]

    def up(k):
        lo, hi = chunks[k]
        return _dot(h, wa_ref[:, lo:hi]), _dot(h, wb_ref[:, lo:hi])

    nxt = up(0)
    y = None
    for k, (lo, hi) in enumerate(chunks):
        a, b = nxt
        if k + 1 < len(chunks):
            nxt = up(k + 1)
        d = _dot((_silu(a) * b).astype(BF16), wo_ref[lo:hi, :])
        y = d if y is None else y + d
    xn = x + m[5:6, :] * y
    if final:
        xn = _rms(xn) * fg_ref[...]
    o_ref[...] = xn


def _ffn(x, g, mod, modsel, wa, wb, wo, final_g):
    t = x.shape[0]
    tm = min(TOKEN_TILE, t)
    mod_row = _mod_row(modsel, tm)
    final = final_g is not None
    resident = lambda a: pl.BlockSpec(a.shape, lambda i: (0, 0), pipeline_mode=pl.Buffered(1))
    in_specs = [pl.BlockSpec((tm, D_MODEL), lambda i: (i, 0)),
                pl.BlockSpec((1, D_MODEL), lambda i: (0, 0)),
                pl.BlockSpec((1, 8, D_MODEL), lambda i: (mod_row(i), 0, 0)),
                resident(wa), resident(wb), resident(wo)]
    args = [x, g.reshape(1, D_MODEL), mod, wa, wb, wo]
    if final:
        in_specs.append(pl.BlockSpec((1, D_MODEL), lambda i: (0, 0)))
        args.append(final_g.reshape(1, D_MODEL))
    return pl.pallas_call(
        functools.partial(_ffn_body, final=final),
        out_shape=jax.ShapeDtypeStruct((t, D_MODEL), F32),
        grid=(t // tm,),
        in_specs=in_specs,
        out_specs=pl.BlockSpec((tm, D_MODEL), lambda i: (i, 0)),
        compiler_params=_cparams("parallel"),
        name="ffn",
    )(*args)


def _axial_angles(n_tokens, rot_dim):
    t = jnp.arange(n_tokens)
    row = (t // GRID_W).astype(F32)
    col = (t % GRID_W).astype(F32)
    nf = rot_dim // 4
    inv = ROPE_BASE ** (-jnp.arange(nf, dtype=F32) / nf)
    ang = jnp.concatenate([row[:, None] * inv, col[:, None] * inv], axis=-1)
    return jnp.cos(ang), jnp.sin(ang)


def _rope_tables_head64(n_tokens):
    cos, sin = _axial_angles(n_tokens, 64)
    zero = jnp.zeros_like(sin)
    c = jnp.tile(jnp.concatenate([cos, cos], axis=-1), (1, 2))
    sa = jnp.tile(jnp.concatenate([-sin, zero], axis=-1), (1, 2))
    sb = jnp.tile(jnp.concatenate([zero, sin], axis=-1), (1, 2))
    return c, sa, sb


def _rope_tables_mla(n_tokens):
    cos, sin = _axial_angles(n_tokens, MLA_ROPE)
    z16 = jnp.zeros_like(sin)
    one64 = jnp.ones((n_tokens, MLA_NOPE), F32)
    z64 = jnp.zeros((n_tokens, MLA_NOPE), F32)
    z32 = jnp.zeros((n_tokens, 32), F32)
    c = jnp.concatenate([one64, cos, cos, z32], axis=-1)
    sa = jnp.concatenate([z64, -sin, z16, z32], axis=-1)
    sb = jnp.concatenate([z64, z16, sin, z32], axis=-1)
    return c, sa, sb


def _layer_weights(l, w_in, w_uq, w_ukv, w_branch, w_out, w_ffn_in, w_ffn_out):
    wi = w_in[l]
    d = wi.shape[0]
    w_ret = wi[:, 0:1536]
    w_diff = wi[:, 1536:3072]
    w_mla = jnp.concatenate([wi[:, 3456:3712], jnp.zeros((d, 64), wi.dtype), wi[:, 3712:3744],
                             jnp.zeros((d, 32), wi.dtype), wi[:, 3072:3456]], axis=1)
    w_gate = wi[:, 3744:]
    w_proj = jnp.concatenate([w_gate, w_ret, w_diff, w_mla], axis=1)
    assert w_proj.shape[1] == N_PROJ
    hd = MLA_NOPE + MLA_ROPE
    wq = jnp.pad(w_uq[l].reshape(Q_LORA, H_MLA, hd), ((0, 0), (0, 0), (0, LANES - hd))).reshape(Q_LORA, H_MLA * LANES)
    wkv = w_ukv[l].reshape(KV_LORA, H_MLA, MLA_NOPE + MLA_V)
    wk = jnp.pad(wkv[:, :, :MLA_NOPE], ((0, 0), (0, 0), (0, LANES - MLA_NOPE))).reshape(KV_LORA, H_MLA * LANES)
    wv = wkv[:, :, MLA_NOPE:].reshape(KV_LORA, H_MLA * MLA_V)
    wa, wb = w_ffn_in[l][:, :D_FF], w_ffn_in[l][:, D_FF:]
    cast = lambda a: a.astype(BF16)
    return dict(proj=cast(w_proj), wq=cast(wq), wk=cast(wk), wv=cast(wv), wbr=cast(w_branch[l]), wo=cast(w_out[l]),
                wa=cast(wa), wb=cast(wb), wf=cast(w_ffn_out[l]))


def _mixer_layer(x, b, s, l, lw, mod, modsel, norm1_g, norm2_g, tabs, mla_q_norm, mla_kv_norm, dl, lam_init,
                 s0f, s0b, caches, ropes, ctx, final_g):
    tk = min(KEY_TILE, s)
    (seg, q_mla, kd, vd, km, vm), ctx_out = _input_proj(x, b, s, tk, norm1_g, mod, modsel, lw, mla_q_norm,
                                                       mla_kv_norm, ropes, ctx)

    y_ret, fin_f, fin_b = _retention(seg, b, s, s0f, s0b, tabs)

    diff_parts, mla_parts = [(kd, vd)], [(km, vm)]
    if caches is not None:
        cache_k, cache_v, cache_ckv, cache_kpe = caches
        diff_parts.insert(0, _diff_cache_prep(cache_k, cache_v, l, tk))
        mla_parts.insert(0, _mla_cache_prep(cache_ckv, cache_kpe, l, lw["wk"], lw["wv"], tk))
    y_diff = _diff_attention(seg, b, s, diff_parts, dl, lam_init)
    y_mla = _mla_attention(q_mla, b, s, mla_parts)

    x = _merge(x, y_ret, y_diff, y_mla, seg, mod, modsel, lw["wbr"], lw["wo"])
    x = _ffn(x, norm2_g, mod, modsel, lw["wa"], lw["wb"], lw["wf"], final_g)
    return x, fin_f, fin_b, ctx_out


def kernel(x_prompt, x_sample, state_ret_fwd, state_ret_bwd, cache_diff_k, cache_diff_v, cache_mla_ckv,
           cache_mla_kpe, c, c_ctx, norm1_g, norm2_g, w_ada, b_ada, w_in, ret_decay_fwd, ret_decay_bwd,
           diff_lambda, mla_q_norm, mla_kv_norm, w_uq, w_ukv, w_branch, w_out, w_ffn_in, w_ffn_out, final_g):
    bp, sp, d = x_prompt.shape
    bs, ss, _ = x_sample.shape
    depth = w_in.shape[0]
    past = cache_diff_k.shape[2]

    n_rows = -(-(1 + bs) // 8) * 8
    cond = jnp.zeros((n_rows, d), F32).at[0].set(c_ctx).at[1:1 + bs].set(c)
    mod_all = _modulation(cond, w_ada, b_ada)
    mod_all = jnp.pad(mod_all.reshape(depth, n_rows, 6, d), ((0, 0), (0, 0), (0, 2), (0, 0)))

    sel_prompt = (0, bp * sp)
    sel_sample = (1, ss)

    ropes = (_rope_tables_head64(ss), _rope_tables_mla(ss))
    cache_k = cache_diff_k.reshape(bs, depth, past, H_DIFF * 2 * DIFF_HD)
    cache_v = cache_diff_v.reshape(bs, depth, past, H_DIFF * 2 * DIFF_HD)
    cache_kpe = jnp.pad(cache_mla_kpe, ((0, 0), (0, 0), (0, 0), (MLA_NOPE, LANES - MLA_NOPE - MLA_ROPE)))
    zero_state = jnp.zeros((bp, H_RET, RET_DK, RET_DV), F32)

    xp = x_prompt.reshape(bp * sp, d)
    xs = x_sample.reshape(bs * ss, d)
    ret_f, ret_b, ctx_arrays = [], [], None
    for l in range(depth):
        lw = _layer_weights(l, w_in, w_uq, w_ukv, w_branch, w_out, w_ffn_in, w_ffn_out)
        lam_init = 0.8 - 0.6 * math.exp(-0.3 * l)
        tabs = _retention_tables(ret_decay_fwd[l], ret_decay_bwd[l])
        dl = jnp.pad(diff_lambda[l], ((0, 4), (0, LANES - DIFF_HD)))
        fg = final_g if l == depth - 1 else None
        common = (norm1_g[l], norm2_g[l], tabs, mla_q_norm[l], mla_kv_norm[l], dl, lam_init)

        xp, fin_f, fin_b, ctx_arrays = _mixer_layer(xp, bp, sp, l, lw, mod_all[l], sel_prompt, *common, zero_state,
                                                    zero_state, None, None, (l, depth, ctx_arrays), fg)
        ret_f.append(fin_f)
        ret_b.append(fin_b)

        xs, _, _, _ = _mixer_layer(xs, bs, ss, l, lw, mod_all[l], sel_sample, *common, state_ret_fwd[:, l],
                                   state_ret_bwd[:, l], (cache_k, cache_v, cache_mla_ckv, cache_kpe), ropes, None, fg)

    dk_all, dv_all, ckv_all, kpe_all = ctx_arrays
    head_shape = (bp, depth, sp, H_DIFF, 2 * DIFF_HD)
    return (xp.reshape(bp, sp, d), xs.reshape(bs, ss, d), jnp.stack(ret_f, axis=1), jnp.stack(ret_b, axis=1),
            dk_all.reshape(head_shape), dv_all.reshape(head_shape), ckv_all, kpe_all)
```
